```python
import math
import jax, jax.numpy as jnp
from jax import lax
import numpy as np

D_MODEL = 1024
BATCH = 2
SEQ = 8192
DEPTH = 2

GRID_W = 64
CTX_LEN = 256
EPS = 1e-6

N_BRANCH = 4
BRANCH_W = 512

POOL_GROUPS = 4
POOL_GC = BRANCH_W // POOL_GROUPS
POOL_WINDOWS = (2, 4, 8, 16)

DIFF_HEADS = 4
DIFF_DH = 64
DIFF_DV = BRANCH_W // DIFF_HEADS
DIFF_QBLOCK = 128
ROPE_BASE = 10000.0

SGU_GROUPS = 4
SGU_GC = BRANCH_W // SGU_GROUPS
SGU_CHUNK = 128

NA_HEADS = 8
NA_DH = BRANCH_W // NA_HEADS
NA_WIN_R = 8
NA_WIN_C = 16

N_EXPERTS = 16
EC_CAPACITY = 2
D_EXPERT = 1024

COL_SIZES = (
    DIFF_HEADS * 2 * DIFF_DH,
    BRANCH_W,
    BRANCH_W,
    BRANCH_W,
    DIFF_HEADS * 2 * DIFF_DH,
    BRANCH_W,
    BRANCH_W,
    BRANCH_W,
    BRANCH_W,
    N_BRANCH * D_MODEL,
)
KV_COLS = sum(COL_SIZES[:4])
IN_COLS = sum(COL_SIZES)

kernel_name = "hybrid_gated_branch_dit_block"


def rmsnorm(x, g):
    xf = x.astype(jnp.float32)
    y = xf * lax.rsqrt(jnp.mean(xf * xf, axis=-1, keepdims=True) + EPS)
    return y.astype(x.dtype) * g


def layernorm(x, g):
    xf = x.astype(jnp.float32)
    xc = xf - jnp.mean(xf, axis=-1, keepdims=True)
    y = xc * lax.rsqrt(jnp.mean(xc * xc, axis=-1, keepdims=True) + EPS)
    return y.astype(x.dtype) * g


def modulate(h, shift, scale):
    return h * (1 + scale) + shift


def split_cols(z, sizes):
    offs = np.cumsum(sizes)[:-1]
    return jnp.split(z, [int(o) for o in offs], axis=-1)


def rope_tables(T):
    t = jnp.arange(T)
    row = (t // GRID_W).astype(jnp.float32)
    col = (t % GRID_W).astype(jnp.float32)
    nf = DIFF_DH // 4
    inv = ROPE_BASE ** (-jnp.arange(nf, dtype=jnp.float32) / nf)
    ar = row[:, None] * inv
    ac = col[:, None] * inv
    return (jnp.cos(ar), jnp.sin(ar), jnp.cos(ac), jnp.sin(ac))


def rotate(x, cos, sin):
    x1, x2 = jnp.split(x, 2, axis=-1)
    return jnp.concatenate([x1 * cos - x2 * sin, x1 * sin + x2 * cos], axis=-1).astype(x.dtype)


def rope2d(x, tabs):
    cr, sr, cc, sc = [a[None, :, None, None, :] for a in tabs]
    xr, xcol = jnp.split(x, 2, axis=-1)
    return jnp.concatenate([rotate(xr, cr, sr), rotate(xcol, cc, sc)], axis=-1)


def pool_mixer(z, w_pool, scale):
    B, T, _ = z.shape
    zf = z.reshape(B, T, POOL_GROUPS, POOL_GC).astype(jnp.float32)
    cs = jnp.concatenate([jnp.zeros((B, 1, POOL_GROUPS, POOL_GC), jnp.float32),
                          jnp.cumsum(zf, axis=1)], axis=1)
    t = jnp.arange(T)
    means = []
    for g, w in enumerate(POOL_WINDOWS):
        lo = jnp.clip(t - w // 2, 0, T)
        hi = jnp.clip(t + w // 2, 0, T)
        cnt = (hi - lo).astype(jnp.float32)[None, :, None]
        means.append((cs[:, hi, g] - cs[:, lo, g]) / cnt)
    pooled = jnp.stack(means, axis=2)
    d = (pooled - zf).astype(z.dtype)
    y = jnp.einsum('btgc,gce->btge', d, w_pool)
    return y.reshape(B, T, BRANCH_W) * scale


def compute_lambda(lam_p, lam_init):
    lp = lam_p.astype(jnp.float32)
    return jnp.exp(jnp.sum(lp[0] * lp[1])) - jnp.exp(jnp.sum(lp[2] * lp[3])) + lam_init


def diff_attend(q, k, v, lam, lam_init, g):
    s = jnp.einsum('bqhmd,bkhmd->bhmqk', q, k).astype(jnp.float32) * (DIFF_DH ** -0.5)
    p = jax.nn.softmax(s, axis=-1)
    a = (p[:, :, 0] - lam * p[:, :, 1]).astype(v.dtype)
    o = jnp.einsum('bhqk,bkhe->bqhe', a, v)
    o = rmsnorm(o, g) * (1.0 - lam_init)
    return o.reshape(o.shape[0], o.shape[1], BRANCH_W)


def diff_latent(q, k, v, lam, lam_init, g):
    B, T = q.shape[0], q.shape[1]
    nb = T // DIFF_QBLOCK
    qb = jnp.moveaxis(q.reshape(B, nb, DIFF_QBLOCK, DIFF_HEADS, 2, DIFF_DH), 1, 0)
    ob = lax.map(lambda qi: diff_attend(qi, k, v, lam, lam_init, g), qb)
    return jnp.moveaxis(ob, 0, 1).reshape(B, T, BRANCH_W)


def dense_attend(q, k, v, heads, dh):
    B, Q, _ = q.shape
    K = k.shape[1]
    qh = q.reshape(B, Q, heads, dh)
    kh = k.reshape(B, K, heads, dh)
    vh = v.reshape(B, K, heads, dh)
    s = jnp.einsum('bqhd,bkhd->bhqk', qh, kh).astype(jnp.float32) * (dh ** -0.5)
    p = jax.nn.softmax(s, axis=-1).astype(v.dtype)
    return jnp.einsum('bhqk,bkhd->bqhd', p, vh).reshape(B, Q, heads * dh)


def na_latent(q, k, v, k_ctx, v_ctx, rpb):
    B, T, _ = q.shape
    rows = T // GRID_W
    kr = min(NA_WIN_R, rows)
    grid = lambda a: a.reshape(B, rows, GRID_W, NA_HEADS, NA_DH)
    qg, kg, vg = grid(q), grid(k), grid(v)
    kc = k_ctx.reshape(B, -1, NA_HEADS, NA_DH)
    vc = v_ctx.reshape(B, -1, NA_HEADS, NA_DH)
    cols = np.arange(GRID_W)
    col_start = np.clip(cols - NA_WIN_C // 2, 0, GRID_W - NA_WIN_C)
    col_idx = col_start[:, None] + np.arange(NA_WIN_C)[None, :]
    dc_idx = col_idx - cols[:, None] + (NA_WIN_C - 1)
    nloc = kr * NA_WIN_C
    scale = NA_DH ** -0.5

    def row_fn(args):
        r, q_row = args
        r0 = jnp.clip(r - kr // 2, 0, rows - kr)
        kwin = lax.dynamic_slice_in_dim(kg, r0, kr, axis=1)[:, :, col_idx]
        vwin = lax.dynamic_slice_in_dim(vg, r0, kr, axis=1)[:, :, col_idx]
        dr_idx = r0 + jnp.arange(kr) - r + (NA_WIN_R - 1)
        bias = rpb[:, dr_idx][:, :, dc_idx].transpose(0, 2, 1, 3)
        s_loc = jnp.einsum('bqhd,biqjhd->bhqij', q_row, kwin).astype(jnp.float32) * scale \
            + bias.astype(jnp.float32)[None]
        s_ctx = jnp.einsum('bqhd,bkhd->bhqk', q_row, kc).astype(jnp.float32) * scale
        s = jnp.concatenate([s_loc.reshape(B, NA_HEADS, GRID_W, nloc), s_ctx], axis=-1)
        p = jax.nn.softmax(s, axis=-1).astype(v.dtype)
        p_loc = p[..., :nloc].reshape(B, NA_HEADS, GRID_W, kr, NA_WIN_C)
        p_ctx = p[..., nloc:]
        return jnp.einsum('bhqij,biqjhd->bqhd', p_loc, vwin) + jnp.einsum('bhqk,bkhd->bqhd', p_ctx, vc)

    out = lax.map(row_fn, (jnp.arange(rows), jnp.moveaxis(qg, 1, 0)))
    return jnp.moveaxis(out, 0, 1).reshape(B, T, BRANCH_W)


def sgu_mixer(u, v, g, w_s, b):
    B, T, _ = u.shape
    n = T // SGU_CHUNK
    u = jax.nn.gelu(u)
    v = layernorm(jax.nn.gelu(v), g)
    vr = v.reshape(B, n, SGU_CHUNK, SGU_GROUPS, SGU_GC)
    f = jnp.einsum('gpq,bnqgc->bnpgc', w_s, vr) + b.T[None, None, :, :, None]
    return u * f.reshape(B, T, BRANCH_W)


def merge(branches, gates, w_br, w_out):
    B, T, _ = gates.shape
    br = jnp.stack(branches, axis=2)
    proj = jnp.einsum('btie,ied->btid', br, w_br)
    gsig = jax.nn.sigmoid(gates.reshape(B, T, N_BRANCH, D_MODEL))
    return jnp.sum(gsig * proj, axis=2) @ w_out


def token_mixer(hx, hc, tabs, p, lam, lam_init, with_ctx_out):
    B, T, _ = hx.shape
    w_in = p['w_in']
    dk, dv, nk, nv, dq, nq, pz, su, sv, gz = split_cols(hx @ w_in, COL_SIZES)
    dk_c, dv_c, nk_c, nv_c = split_cols(hc @ w_in[:, :KV_COLS], COL_SIZES[:4])
    dshape = lambda a: a.reshape(a.shape[0], a.shape[1], DIFF_HEADS, 2, DIFF_DH)
    vshape = lambda a: a.reshape(a.shape[0], a.shape[1], DIFF_HEADS, DIFF_DV)
    k_all = jnp.concatenate([dshape(dk_c), rope2d(dshape(dk), tabs)], axis=1)
    v_all = jnp.concatenate([vshape(dv_c), vshape(dv)], axis=1)
    diff_x = diff_latent(rope2d(dshape(dq), tabs), k_all, v_all, lam, lam_init, p['diff_subln_g'])
    na_x = na_latent(nq, nk, nv, nk_c, nv_c, p['na_rpb'])
    pool_x = pool_mixer(pz, p['pool_w'], p['pool_scale'])
    sgu_x = sgu_mixer(su, sv, p['sgu_norm_g'], p['sgu_w'], p['sgu_b'])
    out_x = merge([pool_x, diff_x, sgu_x, na_x], gz, p['w_br'], p['w_out'])
    if not with_ctx_out:
        return out_x, None
    dq_c, nq_c, pz_c, su_c, sv_c, gz_c = split_cols(hc @ w_in[:, KV_COLS:], COL_SIZES[4:])
    diff_c = diff_attend(dshape(dq_c), dshape(dk_c), vshape(dv_c), lam, lam_init, p['diff_subln_g'])
    na_c = dense_attend(nq_c, nk_c, nv_c, NA_HEADS, NA_DH)
    pool_c = pool_mixer(pz_c, p['pool_w'], p['pool_scale'])
    sgu_c = sgu_mixer(su_c, sv_c, p['sgu_norm_g'], p['sgu_w'], p['sgu_b'])
    out_c = merge([pool_c, diff_c, sgu_c, na_c], gz_c, p['w_br'], p['w_out'])
    return out_x, out_c


def ec_moe(h, router_w, w1, w3, w2):
    B, T, D = h.shape
    cap = EC_CAPACITY * T // N_EXPERTS
    aff = jax.nn.softmax((h @ router_w).astype(jnp.float32), axis=-1)
    gate, idx = lax.top_k(jnp.swapaxes(aff, 1, 2), cap)
    xs = jax.vmap(lambda hb, ib: hb[ib])(h, idx)
    hid = jax.nn.silu(jnp.einsum('becd,edf->becf', xs, w1)) * jnp.einsum('becd,edf->becf', xs, w3)
    y = jnp.einsum('becf,efd->becd', hid, w2) * gate[..., None].astype(h.dtype)
    return jax.vmap(lambda ib, yb: jnp.zeros((T, D), h.dtype).at[ib.reshape(-1)].add(yb.reshape(-1, D)))(idx, y)


def setup_inputs(seed: int = 0) -> dict:
    key = jax.random.key(seed)
    ks = jax.random.split(key, 24)
    nrm = lambda k, shape, s: jax.random.normal(k, shape, jnp.float32) * s
    D, L, E, F = D_MODEL, DEPTH, N_EXPERTS, D_EXPERT
    return {
        'x': nrm(ks[0], (BATCH, SEQ, D), 1.0),
        'c': nrm(ks[1], (BATCH, D), 1.0),
        'ctx': nrm(ks[2], (BATCH, CTX_LEN, D), 1.0),
        'c_ctx': nrm(ks[3], (D,), 1.0),
        'w_mod': nrm(ks[4], (L, D, 6 * D), 0.5 * D ** -0.5),
        'b_mod': nrm(ks[5], (L, 6 * D), 0.02),
        'norm1_g': 1.0 + nrm(ks[6], (L, D), 0.02),
        'norm2_g': 1.0 + nrm(ks[7], (L, D), 0.02),
        'w_in': nrm(ks[8], (L, D, IN_COLS), D ** -0.5),
        'pool_w': nrm(ks[9], (L, POOL_GROUPS, POOL_GC, POOL_GC), POOL_GC ** -0.5),
        'pool_scale': 1.0 + nrm(ks[10], (L, BRANCH_W), 0.1),
        'diff_lambda': nrm(ks[11], (L, 4, DIFF_DH), 0.1),
        'diff_subln_g': 1.0 + nrm(ks[12], (L, DIFF_DV), 0.02),
        'sgu_norm_g': 1.0 + nrm(ks[13], (L, BRANCH_W), 0.02),
        'sgu_w': nrm(ks[14], (L, SGU_GROUPS, SGU_CHUNK, SGU_CHUNK), SGU_CHUNK ** -0.5),
        'sgu_b': 1.0 + nrm(ks[15], (L, SGU_GROUPS, SGU_CHUNK), 0.1),
        'na_rpb': nrm(ks[16], (L, NA_HEADS, 2 * NA_WIN_R - 1, 2 * NA_WIN_C - 1), 0.1),
        'w_br': nrm(ks[17], (L, N_BRANCH, BRANCH_W, D), BRANCH_W ** -0.5),
        'w_out': nrm(ks[18], (L, D, D), D ** -0.5),
        'router_w': nrm(ks[19], (L, D, E), D ** -0.5),
        'moe_w1': nrm(ks[20], (L, E, D, F), D ** -0.5),
        'moe_w3': nrm(ks[21], (L, E, D, F), D ** -0.5),
        'moe_w2': nrm(ks[22], (L, E, F, D), F ** -0.5),
        'final_g': 1.0 + nrm(ks[23], (D,), 0.02),
    }


def reference(x, c, ctx, c_ctx, w_mod, b_mod, norm1_g, norm2_g, w_in, pool_w, pool_scale,
              diff_lambda, diff_subln_g, sgu_norm_g, sgu_w, sgu_b, na_rpb, w_br, w_out,
              router_w, moe_w1, moe_w3, moe_w2, final_g):
    T = x.shape[1]
    tabs = rope_tables(T)
    silu_c = jax.nn.silu(c)
    silu_cc = jax.nn.silu(c_ctx)
    x_lat, x_ctx = x, ctx
    for l in range(DEPTH):
        last = l == DEPTH - 1
        lam_init = 0.8 - 0.6 * math.exp(-0.3 * l)
        p = {'w_in': w_in[l], 'pool_w': pool_w[l], 'pool_scale': pool_scale[l],
             'diff_subln_g': diff_subln_g[l], 'sgu_norm_g': sgu_norm_g[l], 'sgu_w': sgu_w[l],
             'sgu_b': sgu_b[l], 'na_rpb': na_rpb[l], 'w_br': w_br[l], 'w_out': w_out[l]}
        lam = compute_lambda(diff_lambda[l], lam_init)
        sh1, sc1, g1, sh2, sc2, g2 = jnp.split(silu_c @ w_mod[l] + b_mod[l], 6, axis=-1)
        csh1, csc1, cg1, csh2, csc2, cg2 = jnp.split(silu_cc @ w_mod[l] + b_mod[l], 6, axis=-1)
        hx = modulate(rmsnorm(x_lat, norm1_g[l]), sh1[:, None], sc1[:, None])
        hc = modulate(rmsnorm(x_ctx, norm1_g[l]), csh1, csc1)
        mix_x, mix_c = token_mixer(hx, hc, tabs, p, lam, lam_init, not last)
        x_lat = x_lat + g1[:, None] * mix_x
        hx2 = modulate(rmsnorm(x_lat, norm2_g[l]), sh2[:, None], sc2[:, None])
        x_lat = x_lat + g2[:, None] * ec_moe(hx2, router_w[l], moe_w1[l], moe_w3[l], moe_w2[l])
        if not last:
            x_ctx = x_ctx + cg1 * mix_c
            hc2 = modulate(rmsnorm(x_ctx, norm2_g[l]), csh2, csc2)
            x_ctx = x_ctx + cg2 * ec_moe(hc2, router_w[l], moe_w1[l], moe_w3[l], moe_w2[l])
    return rmsnorm(x_lat, final_g)
```

```python
import functools
import math

import numpy as np
import jax
import jax.numpy as jnp
from jax import lax
from jax.experimental import pallas as pl
from jax.experimental.pallas import tpu as pltpu

F32 = jnp.float32
BF16 = jnp.bfloat16

EPS = 1e-6
GRID_W = 64
LANES = 128
HALF = 64
N_BRANCH = 4
BRANCH_W = 512
POOL_WINDOWS = (2, 4, 8, 16)
POOL_HALO = 16
SGU_CHUNK = 128
NA_WIN_R = 8
NA_WIN_C = 16
NA_QROWS = 8
NA_KROWS = 16
N_EXPERTS = 16
EC_CAPACITY = 2
ROPE_BASE = 10000.0
NEG = -1e30
VMEM_LIMIT = 56 * 1024 * 1024

_NT = (((1,), (1,)), ((), ()))


def _params(*sem):
    return pltpu.CompilerParams(dimension_semantics=sem, vmem_limit_bytes=VMEM_LIMIT)


def _pick(n, cands):
    for c in cands:
        if n % c == 0:
            return c
    raise ValueError(f"no tile for {n}")


def _mod_kernel(c_ref, w_ref, b_ref, o_ref):
    c = c_ref[...]
    s = c * jax.nn.sigmoid(c)
    o_ref[...] = jnp.dot(s, w_ref[...], preferred_element_type=F32,
                         precision=lax.Precision.HIGHEST) + b_ref[...]


def _mod_vectors(cvec, w, b):
    D, N = w.shape
    tn = _pick(N, (1536, 1024, 512, 128))
    return pl.pallas_call(
        _mod_kernel,
        grid=(N // tn,),
        in_specs=[pl.BlockSpec((8, D), lambda j: (0, 0)),
                  pl.BlockSpec((D, tn), lambda j: (0, j)),
                  pl.BlockSpec((1, tn), lambda j: (0, j))],
        out_specs=pl.BlockSpec((8, tn), lambda j: (0, j)),
        out_shape=jax.ShapeDtypeStruct((8, N), F32),
        compiler_params=_params("parallel"),
        name="mod_vectors",
    )(cvec, w, b.reshape(1, N))


def _rope_store(acc, cos, sin, o_ref):
    lane = lax.broadcasted_iota(jnp.int32, (1, LANES), 1)
    first = (lane % 32) < 16
    for h in range(acc.shape[1] // LANES):
        a = acc[:, h * LANES:(h + 1) * LANES]
        partner = jnp.where(first, pltpu.roll(a, LANES - 16, 1), pltpu.roll(a, 16, 1))
        o_ref[:, h * LANES:(h + 1) * LANES] = (a * cos + partner * sin).astype(o_ref.dtype)


def _norm_proj_kernel(x_ref, g_ref, sh_ref, sc_ref, w_ref, *rest, rope_tiles):
    if rope_tiles:
        cos_ref, sin_ref, o_ref, h_ref = rest
    else:
        o_ref, h_ref = rest
    j = pl.program_id(1)

    @pl.when(j == 0)
    def _():
        x = x_ref[...]
        y = x * lax.rsqrt(jnp.mean(x * x, axis=-1, keepdims=True) + EPS) * g_ref[...]
        h_ref[...] = (y * (1.0 + sc_ref[...]) + sh_ref[...]).astype(BF16)

    acc = jnp.dot(h_ref[...], w_ref[...], preferred_element_type=F32)
    if rope_tiles:
        is_rope = functools.reduce(jnp.logical_or, [j == t for t in rope_tiles])

        @pl.when(is_rope)
        def _():
            _rope_store(acc, cos_ref[...], sin_ref[...], o_ref)

        @pl.when(jnp.logical_not(is_rope))
        def _():
            o_ref[...] = acc.astype(o_ref.dtype)
    else:
        o_ref[...] = acc.astype(o_ref.dtype)


def _norm_proj(x2, g, shift, scale, w, out_dtype, rope=None, rope_tiles=()):
    M, D = x2.shape
    N = w.shape[1]
    G = shift.shape[0]
    rows_per_group = M // G
    tm = _pick(rows_per_group, (1024, 512, 256))
    tn = 512
    assert N % tn == 0
    in_specs = [pl.BlockSpec((tm, D), lambda i, j: (i, 0)),
                pl.BlockSpec((1, D), lambda i, j: (0, 0)),
                pl.BlockSpec((None, 1, D), lambda i, j: (i * tm // rows_per_group, 0, 0)),
                pl.BlockSpec((None, 1, D), lambda i, j: (i * tm // rows_per_group, 0, 0)),
                pl.BlockSpec((D, tn), lambda i, j: (0, j))]
    args = [x2, g.reshape(1, D), shift, scale, w]
    if rope_tiles:
        cos, sin = rope
        nt = cos.shape[0] // tm
        in_specs += [pl.BlockSpec((tm, LANES), lambda i, j: (i % nt, 0)),
                     pl.BlockSpec((tm, LANES), lambda i, j: (i % nt, 0))]
        args += [cos, sin]
    return pl.pallas_call(
        functools.partial(_norm_proj_kernel, rope_tiles=tuple(rope_tiles)),
        grid=(M // tm, N // tn),
        in_specs=in_specs,
        out_specs=pl.BlockSpec((tm, tn), lambda i, j: (i, j)),
        out_shape=jax.ShapeDtypeStruct((M, N), out_dtype),
        scratch_shapes=[pltpu.VMEM((tm, D), BF16)],
        compiler_params=_params("parallel", "arbitrary"),
        name="norm_proj",
    )(*args)


def _rope_tables(T):
    t = jnp.arange(T)
    row = (t // GRID_W).astype(F32)
    col = (t % GRID_W).astype(F32)
    nf = HALF // 4
    inv = ROPE_BASE ** (-jnp.arange(nf, dtype=F32) / nf)
    ar = row[:, None] * inv
    ac = col[:, None] * inv
    cos = jnp.concatenate([jnp.cos(ar), jnp.cos(ar), jnp.cos(ac), jnp.cos(ac)], axis=-1)
    sin = jnp.concatenate([-jnp.sin(ar), jnp.sin(ar), -jnp.sin(ac), jnp.sin(ac)], axis=-1)
    return jnp.tile(cos, (1, 2)), jnp.tile(sin, (1, 2))


def _flash_kernel(q_ref, k_ref, v_ref, lam_ref, g_ref, o_ref, m_ref, l_ref, acc_ref, *,
                  mode, tk, lam_init):
    tq = q_ref.shape[0]
    nk = k_ref.shape[0] // tk
    lane = lax.broadcasted_iota(jnp.int32, (1, LANES), 1)
    lo = lane < HALF
    if mode == "diff":
        lp = lam_ref[...]
        d1 = jnp.sum(lp[0:1, :] * lp[1:2, :], keepdims=True)
        d2 = jnp.sum(lp[2:3, :] * lp[3:4, :], keepdims=True)
        lam = jnp.exp(d1) - jnp.exp(d2) + lam_init
    for h in range(q_ref.shape[1] // LANES):
        cs = slice(h * LANES, (h + 1) * LANES)
        qh = q_ref[:, cs]
        zero = jnp.zeros_like(qh)
        q2 = jnp.concatenate([jnp.where(lo, qh, zero), jnp.where(lo, zero, qh)], axis=0)
        q2 = q2 * jnp.asarray(HALF ** -0.5, BF16)
        m_ref[...] = jnp.full(m_ref.shape, -jnp.inf, F32)
        l_ref[...] = jnp.zeros(l_ref.shape, F32)
        acc_ref[...] = jnp.zeros(acc_ref.shape, F32)

        def body(c, carry):
            off = pl.multiple_of(c * tk, tk)
            kc = k_ref[pl.ds(off, tk), cs]
            vc = v_ref[pl.ds(off, tk), cs]
            s = lax.dot_general(q2, kc, _NT, preferred_element_type=F32)
            m_old = m_ref[...]
            m_new = jnp.maximum(m_old, jnp.max(s, axis=-1, keepdims=True))
            alpha = jnp.exp(m_old - m_new)
            p = jnp.exp(s - m_new)
            l_ref[...] = alpha * l_ref[...] + jnp.sum(p, axis=-1, keepdims=True)
            acc_ref[...] = alpha * acc_ref[...] + jnp.dot(p.astype(BF16), vc,
                                                          preferred_element_type=F32)
            m_ref[...] = m_new
            return carry

        lax.fori_loop(0, nk, body, 0)
        a = acc_ref[...] / l_ref[...]
        a0 = a[:tq]
        a1 = a[tq:]
        if mode == "diff":
            o = a0 - lam * a1
            o = o * lax.rsqrt(jnp.mean(o * o, axis=-1, keepdims=True) + EPS)
            o = o * g_ref[...] * (1.0 - lam_init)
        else:
            o = jnp.where(lo, a0, a1)
        o_ref[:, cs] = o.astype(o_ref.dtype)


def _flash(q, k, v, lam_p, g, *, mode, lam_init=0.0):
    B, Tq, W = q.shape
    Tk = k.shape[1]
    tq = _pick(Tq, (256, 128))
    tk = _pick(Tk, (768, 512, 384, 256, 128))
    return pl.pallas_call(
        functools.partial(_flash_kernel, mode=mode, tk=tk, lam_init=lam_init),
        grid=(B, Tq // tq),
        in_specs=[pl.BlockSpec((None, tq, W), lambda b, i: (b, i, 0)),
                  pl.BlockSpec((None, Tk, W), lambda b, i: (b, 0, 0)),
                  pl.BlockSpec((None, Tk, W), lambda b, i: (b, 0, 0)),
                  pl.BlockSpec(lam_p.shape, lambda b, i: (0, 0)),
                  pl.BlockSpec((1, LANES), lambda b, i: (0, 0))],
        out_specs=pl.BlockSpec((None, tq, W), lambda b, i: (b, i, 0)),
        out_shape=jax.ShapeDtypeStruct((B, Tq, W), BF16),
        scratch_shapes=[pltpu.VMEM((2 * tq, 1), F32), pltpu.VMEM((2 * tq, 1), F32),
                        pltpu.VMEM((2 * tq, LANES), F32)],
        compiler_params=_params("parallel", "parallel"),
        name="flash_" + mode,
    )(q, k, v, lam_p, g.reshape(1, LANES))


def _na_kernel(q_ref, k_ref, v_ref, kc_ref, vc_ref, b_ref, o_ref, *, rows):
    rb = pl.program_id(2)
    nq = NA_QROWS * GRID_W
    nkk = NA_KROWS * GRID_W
    k0 = jnp.clip(rb * NA_QROWS - NA_WIN_R // 2, 0, rows - NA_KROWS) * GRID_W
    k0 = pl.multiple_of(k0, 4 * GRID_W)
    kl = k_ref[pl.ds(k0, nkk), :]
    vl = v_ref[pl.ds(k0, nkk), :]
    kc = kc_ref[...]
    vc = vc_ref[...]
    q = q_ref[...]
    zero = jnp.zeros_like(q)
    lane = lax.broadcasted_iota(jnp.int32, (1, LANES), 1)
    lo = lane < HALF
    scale = jnp.asarray(HALF ** -0.5, BF16)
    outs = []
    for hh in range(2):
        qm = (jnp.where(lo, q, zero) if hh == 0 else jnp.where(lo, zero, q)) * scale
        s_loc = lax.dot_general(qm, kl, _NT, preferred_element_type=F32) + b_ref[hh]
        s_ctx = lax.dot_general(qm, kc, _NT, preferred_element_type=F32)
        m = jnp.maximum(jnp.max(s_loc, axis=-1, keepdims=True), jnp.max(s_ctx, axis=-1, keepdims=True))
        p_loc = jnp.exp(s_loc - m)
        p_ctx = jnp.exp(s_ctx - m)
        l = jnp.sum(p_loc, axis=-1, keepdims=True) + jnp.sum(p_ctx, axis=-1, keepdims=True)
        o = jnp.dot(p_loc.astype(BF16), vl, preferred_element_type=F32) \
            + jnp.dot(p_ctx.astype(BF16), vc, preferred_element_type=F32)
        outs.append(o / l)
    del nq
    o_ref[...] = jnp.where(lo, outs[0], outs[1]).astype(o_ref.dtype)


def _na_tables(rpb, rows):
    H = rpb.shape[0]
    c = np.arange(GRID_W)
    cstart = np.clip(c - NA_WIN_C // 2, 0, GRID_W - NA_WIN_C)
    kcol = np.arange(GRID_W)
    col_ok = (kcol[None, :] >= cstart[:, None]) & (kcol[None, :] < cstart[:, None] + NA_WIN_C)
    dc = np.clip(kcol[None, :] - c[:, None] + NA_WIN_C - 1, 0, 2 * NA_WIN_C - 2)
    cm = (np.arange(2 * NA_WIN_C - 1)[None, None, :] == dc[:, :, None]) & col_ok[:, :, None]
    tabs = []
    for q0, k0, clamp in ((0, 0, True), (NA_QROWS, NA_QROWS - NA_WIN_R // 2, False),
                          (rows - NA_QROWS, rows - NA_KROWS, True)):
        r = q0 + np.arange(NA_QROWS)
        kr = k0 + np.arange(NA_KROWS)
        r0 = r - NA_WIN_R // 2
        if clamp:
            r0 = np.clip(r0, 0, rows - NA_WIN_R)
        row_ok = (kr[None, :] >= r0[:, None]) & (kr[None, :] < r0[:, None] + NA_WIN_R)
        dr = np.clip(kr[None, :] - r[:, None] + NA_WIN_R - 1, 0, 2 * NA_WIN_R - 2)
        rm = (np.arange(2 * NA_WIN_R - 1)[None, None, :] == dr[:, :, None]) & row_ok[:, :, None]
        t = jnp.einsum('rki,hij,cqj->hrckq', jnp.asarray(rm, F32), rpb, jnp.asarray(cm, F32),
                       precision=lax.Precision.HIGHEST)
        ok = row_ok[:, None, :, None] & col_ok[None, :, None, :]
        t = jnp.where(jnp.asarray(ok)[None], t, NEG)
        tabs.append(t.reshape(H, NA_QROWS * GRID_W, NA_KROWS * GRID_W))
    return jnp.stack(tabs)


def _na_attn(z, zc, tables, *, qcol, kcol, vcol):
    B, T, _ = z.shape
    Lc = zc.shape[1]
    rows = T // GRID_W
    nrb = rows // NA_QROWS
    nq = NA_QROWS * GRID_W
    nkk = NA_KROWS * GRID_W
    nhp = BRANCH_W // LANES

    def var(rb):
        return jnp.where(rb == 0, 0, jnp.where(rb == nrb - 1, 2, 1))

    return pl.pallas_call(
        functools.partial(_na_kernel, rows=rows),
        grid=(nhp, B, nrb),
        in_specs=[pl.BlockSpec((None, nq, LANES), lambda h, b, r: (b, r, qcol + h)),
                  pl.BlockSpec((None, T, LANES), lambda h, b, r: (b, 0, kcol + h)),
                  pl.BlockSpec((None, T, LANES), lambda h, b, r: (b, 0, vcol + h)),
                  pl.BlockSpec((None, Lc, LANES), lambda h, b, r: (b, 0, kcol + h)),
                  pl.BlockSpec((None, Lc, LANES), lambda h, b, r: (b, 0, vcol + h)),
                  pl.BlockSpec((None, 2, nq, nkk), lambda h, b, r: (var(r), h, 0, 0))],
        out_specs=pl.BlockSpec((None, nq, LANES), lambda h, b, r: (b, r, h)),
        out_shape=jax.ShapeDtypeStruct((B, T, BRANCH_W), BF16),
        compiler_params=_params("parallel", "parallel", "parallel"),
        name="na_attn",
    )(z, z, z, zc, zc, tables)


def _pool_kernel(prev_ref, cur_ref, next_ref, w_ref, sc_ref, o_ref, *, T):
    i = pl.program_id(1)
    tt = cur_ref.shape[0]
    E = tt + 2 * POOL_HALO
    ext = jnp.concatenate([prev_ref[...], cur_ref[...], next_ref[...]], axis=0).astype(F32)
    gi = i * tt - POOL_HALO + lax.broadcasted_iota(jnp.int32, (E, 1), 0)
    ext = jnp.where((gi >= 0) & (gi < T), ext, 0.0)
    t = i * tt + lax.broadcasted_iota(jnp.int32, (tt, 1), 0)
    for g, w in enumerate(POOL_WINDOWS):
        cs = slice(g * LANES, (g + 1) * LANES)
        zg = ext[:, cs]
        s = zg + pltpu.roll(zg, 1, 0)
        half = 1
        while 2 * half < w:
            s = pltpu.roll(s, half, 0) + pltpu.roll(s, E - half, 0)
            half *= 2
        cnt = (jnp.minimum(t + w // 2, T) - jnp.maximum(t - w // 2, 0)).astype(F32)
        c0 = slice(POOL_HALO, POOL_HALO + tt)
        d = s[c0] / cnt - zg[c0]
        y = jnp.dot(d.astype(BF16), w_ref[g].astype(BF16), preferred_element_type=F32)
        o_ref[:, cs] = (y * sc_ref[:, cs]).astype(o_ref.dtype)


def _pool(z, w_pool, scale, *, col):
    B, T, _ = z.shape
    tt = _pick(T, (512, 256))
    hb = tt // POOL_HALO
    nh = T // POOL_HALO
    return pl.pallas_call(
        functools.partial(_pool_kernel, T=T),
        grid=(B, T // tt),
        in_specs=[pl.BlockSpec((None, POOL_HALO, BRANCH_W), lambda b, i: (b, jnp.maximum(i * hb - 1, 0), col)),
                  pl.BlockSpec((None, tt, BRANCH_W), lambda b, i: (b, i, col)),
                  pl.BlockSpec((None, POOL_HALO, BRANCH_W), lambda b, i: (b, jnp.minimum((i + 1) * hb, nh - 1), col)),
                  pl.BlockSpec(w_pool.shape, lambda b, i: (0, 0, 0)),
                  pl.BlockSpec((1, BRANCH_W), lambda b, i: (0, 0))],
        out_specs=pl.BlockSpec((None, tt, BRANCH_W), lambda b, i: (b, i, 0)),
        out_shape=jax.ShapeDtypeStruct((B, T, BRANCH_W), BF16),
        compiler_params=_params("parallel", "parallel"),
        name="pool_mixer",
    )(z, z, z, w_pool, scale.reshape(1, BRANCH_W))


def _sgu_kernel(u_ref, v_ref, g_ref, w_ref, bt_ref, o_ref):
    ts = u_ref.shape[0]
    u = jax.nn.gelu(u_ref[...].astype(F32))
    v = jax.nn.gelu(v_ref[...].astype(F32))
    vc = v - jnp.mean(v, axis=-1, keepdims=True)
    v = vc * lax.rsqrt(jnp.mean(vc * vc, axis=-1, keepdims=True) + EPS) * g_ref[...]
    vb = v.astype(BF16)
    for g in range(BRANCH_W // LANES):
        cs = slice(g * LANES, (g + 1) * LANES)
        wg = w_ref[g].astype(BF16)
        bg = bt_ref[:, g:g + 1]
        for ck in range(ts // SGU_CHUNK):
            rs = slice(ck * SGU_CHUNK, (ck + 1) * SGU_CHUNK)
            f = jnp.dot(wg, vb[rs, cs], preferred_element_type=F32) + bg
            o_ref[rs, cs] = (u[rs, cs] * f).astype(o_ref.dtype)


def _sgu(z, g, w_s, b, *, ucol, vcol):
    B, T, _ = z.shape
    ts = _pick(T, (512, 256, 128))
    return pl.pallas_call(
        _sgu_kernel,
        grid=(B, T // ts),
        in_specs=[pl.BlockSpec((None, ts, BRANCH_W), lambda b_, i: (b_, i, ucol)),
                  pl.BlockSpec((None, ts, BRANCH_W), lambda b_, i: (b_, i, vcol)),
                  pl.BlockSpec((1, BRANCH_W), lambda b_, i: (0, 0)),
                  pl.BlockSpec(w_s.shape, lambda b_, i: (0, 0, 0)),
                  pl.BlockSpec((SGU_CHUNK, BRANCH_W // LANES), lambda b_, i: (0, 0))],
        out_specs=pl.BlockSpec((None, ts, BRANCH_W), lambda b_, i: (b_, i, 0)),
        out_shape=jax.ShapeDtypeStruct((B, T, BRANCH_W), BF16),
        compiler_params=_params("parallel", "parallel"),
        name="sgu_mixer",
    )(z, z, g.reshape(1, BRANCH_W), w_s, b.T)


def _merge_kernel(p_ref, d_ref, s_ref, n_ref, gz_ref, wbr_ref, wout_ref, x_ref, g1_ref,
                  n2_ref, sh_ref, sc_ref, rw_ref, xo_ref, h_ref, lg_ref):
    D = x_ref.shape[1]
    acc = None
    for i, br in enumerate((p_ref, d_ref, s_ref, n_ref)):
        proj = jnp.dot(br[...], wbr_ref[i], preferred_element_type=F32)
        t = jax.nn.sigmoid(gz_ref[:, i * D:(i + 1) * D].astype(F32)) * proj
        acc = t if acc is None else acc + t
    mix = jnp.dot(acc.astype(BF16), wout_ref[...], preferred_element_type=F32)
    x = x_ref[...] + g1_ref[...] * mix
    xo_ref[...] = x
    y = x * lax.rsqrt(jnp.mean(x * x, axis=-1, keepdims=True) + EPS) * n2_ref[...]
    h = y * (1.0 + sc_ref[...]) + sh_ref[...]
    h_ref[...] = h.astype(BF16)
    lg_ref[...] = jnp.dot(h, rw_ref[...], preferred_element_type=F32,
                          precision=lax.Precision.HIGHEST)


def _merge(branches, zr, w_br, w_out, x2, g1, n2g, sh2, sc2, router_w):
    M, D = x2.shape
    G = g1.shape[0]
    rpg = M // G
    tm = _pick(rpg, (256,))
    E = router_w.shape[1]
    gb = 0
    grp = lambda i: (i * tm // rpg, 0, 0)
    row = lambda i: (i, 0)
    return pl.pallas_call(
        _merge_kernel,
        grid=(M // tm,),
        in_specs=[pl.BlockSpec((tm, BRANCH_W), row)] * 4 + [
            pl.BlockSpec((tm, N_BRANCH * D), lambda i: (i, gb)),
            pl.BlockSpec(w_br.shape, lambda i: (0, 0, 0)),
            pl.BlockSpec(w_out.shape, lambda i: (0, 0)),
            pl.BlockSpec((tm, D), row),
            pl.BlockSpec((None, 1, D), grp),
            pl.BlockSpec((1, D), lambda i: (0, 0)),
            pl.BlockSpec((None, 1, D), grp),
            pl.BlockSpec((None, 1, D), grp),
            pl.BlockSpec(router_w.shape, lambda i: (0, 0))],
        out_specs=[pl.BlockSpec((tm, D), row), pl.BlockSpec((tm, D), row), pl.BlockSpec((tm, E), row)],
        out_shape=[jax.ShapeDtypeStruct((M, D), F32), jax.ShapeDtypeStruct((M, D), BF16),
                   jax.ShapeDtypeStruct((M, E), F32)],
        compiler_params=_params("parallel"),
        name="merge_out",
    )(*branches, zr, w_br, w_out, x2, g1, n2g.reshape(1, D), sh2, sc2, router_w)


def _ffn_kernel(xs_ref, w1_ref, w3_ref, w2_ref, gate_ref, y_ref, w1b, w3b, w2b):
    @pl.when(pl.program_id(1) == 0)
    def _():
        w1b[...] = w1_ref[...].astype(BF16)
        w3b[...] = w3_ref[...].astype(BF16)
        w2b[...] = w2_ref[...].astype(BF16)

    xs = xs_ref[...]
    a = jnp.dot(xs, w1b[...], preferred_element_type=F32)
    b = jnp.dot(xs, w3b[...], preferred_element_type=F32)
    hid = (a * jax.nn.sigmoid(a) * b).astype(BF16)
    y = jnp.dot(hid, w2b[...], preferred_element_type=F32) * gate_ref[...]
    y_ref[...] = y.astype(y_ref.dtype)


def _ffn(xs, w1, w3, w2, gate):
    B, E, C, D = xs.shape
    Fd = w1.shape[2]
    return pl.pallas_call(
        _ffn_kernel,
        grid=(E, B),
        in_specs=[pl.BlockSpec((None, None, C, D), lambda e, b: (b, e, 0, 0)),
                  pl.BlockSpec((None, D, Fd), lambda e, b: (e, 0, 0)),
                  pl.BlockSpec((None, D, Fd), lambda e, b: (e, 0, 0)),
                  pl.BlockSpec((None, Fd, D), lambda e, b: (e, 0, 0)),
                  pl.BlockSpec((None, None, C, 1), lambda e, b: (b, e, 0, 0))],
        out_specs=pl.BlockSpec((None, None, C, D), lambda e, b: (b, e, 0, 0)),
        out_shape=jax.ShapeDtypeStruct((B, E, C, D), BF16),
        scratch_shapes=[pltpu.VMEM((D, Fd), BF16), pltpu.VMEM((D, Fd), BF16), pltpu.VMEM((Fd, D), BF16)],
        compiler_params=_params("parallel", "arbitrary"),
        name="expert_ffn",
    )(xs, w1, w3, w2, gate)


def _combine_kernel(starts_ref, x_ref, g2_ref, fg_ref, idx_ref, *rest, wb, nwin, final):
    ys = rest[:-2]
    o_ref, acc_ref = rest[-2:]
    b = pl.program_id(0)
    t = pl.program_id(1)
    tt = x_ref.shape[0]
    E = idx_ref.shape[0]
    C = idx_ref.shape[1]
    tok = t * tt + lax.broadcasted_iota(jnp.int32, (tt, 1), 0)
    acc_ref[...] = jnp.zeros(acc_ref.shape, F32)
    for e in range(E):
        s0 = starts_ref[b, e, t]
        s1 = starts_ref[b, e, t + 1]
        w0 = s0 // wb
        for k in range(nwin):
            wk = jnp.minimum(w0 + k, C // wb - 1)
            need = (s1 > s0) if k == 0 else ((s1 > s0) & ((s1 - 1) // wb > w0))

            @pl.when(need)
            def _(wk=wk, e=e, k=k):
                if wb == C:
                    ids = idx_ref[e:e + 1, :]
                else:
                    ids = idx_ref[e:e + 1, pl.ds(pl.multiple_of(wk * wb, wb), wb)]
                onehot = jnp.where(ids == tok, 1.0, 0.0).astype(BF16)
                acc_ref[...] += jnp.dot(onehot, ys[e * nwin + k][...], preferred_element_type=F32)

    x = x_ref[...] + g2_ref[...] * acc_ref[...]
    if final:
        x = x * lax.rsqrt(jnp.mean(x * x, axis=-1, keepdims=True) + EPS) * fg_ref[...]
    o_ref[...] = x


def _combine(x3, y, idx, g2, final_g, *, final):
    B, T, D = x3.shape
    E, C = idx.shape[1:]
    if C >= LANES:
        wb, nwin, tt = LANES, 2, LANES
    else:
        wb, nwin, tt = C, 1, T
    nt = T // tt
    G = g2.shape[0]
    bounds = jnp.arange(nt + 1, dtype=jnp.int32) * tt
    starts = jax.vmap(jax.vmap(lambda row: jnp.searchsorted(row, bounds, side='left')))(idx).astype(jnp.int32)

    def ymap(e, k):
        return lambda b, t, st: (b, e, jnp.minimum(st[b, e, t] // wb + k, C // wb - 1), 0)

    y_specs = [pl.BlockSpec((None, None, wb, D), ymap(e, k)) for e in range(E) for k in range(nwin)]
    grid_spec = pltpu.PrefetchScalarGridSpec(
        num_scalar_prefetch=1,
        grid=(B, nt),
        in_specs=[pl.BlockSpec((None, tt, D), lambda b, t, st: (b, t, 0)),
                  pl.BlockSpec((None, 1, D), lambda b, t, st: (b * G // B, 0, 0)),
                  pl.BlockSpec((1, D), lambda b, t, st: (0, 0)),
                  pl.BlockSpec((None, E, C), lambda b, t, st: (b, 0, 0))] + y_specs,
        out_specs=pl.BlockSpec((None, tt, D), lambda b, t, st: (b, t, 0)),
        scratch_shapes=[pltpu.VMEM((tt, D), F32)],
    )
    return pl.pallas_call(
        functools.partial(_combine_kernel, wb=wb, nwin=nwin, final=final),
        grid_spec=grid_spec,
        out_shape=jax.ShapeDtypeStruct((B, T, D), F32),
        compiler_params=_params("parallel", "arbitrary"),
        name="moe_combine",
    )(starts, x3, g2, final_g.reshape(1, D), idx, *([y] * (E * nwin)))


def _route(logits, B, T):
    E = logits.shape[-1]
    cap = EC_CAPACITY * T // E
    aff = jax.nn.softmax(logits.reshape(B, T, E), axis=-1)
    gate, idx = lax.top_k(jnp.swapaxes(aff, 1, 2), cap)
    order = jnp.argsort(idx, axis=-1)
    idx = jnp.take_along_axis(idx, order, axis=-1)
    gate = jnp.take_along_axis(gate, order, axis=-1)
    return idx.astype(jnp.int32), gate


def _moe(x3, h2, logits, w1, w3, w2, g2, final_g, *, final):
    B, T, D = x3.shape
    idx, gate = _route(logits, B, T)
    xs = jax.vmap(lambda hb, ib: hb[ib])(h2.reshape(B, T, D), idx)
    y = _ffn(xs, w1, w3, w2, gate[..., None])
    return _combine(x3, y, idx, g2, final_g, final=final)


_C_DK, _C_DV, _C_NK, _C_NV, _C_DQ, _C_NQ = 0, 4, 8, 12, 16, 20
ATT_COLS = 6 * BRANCH_W
KV_COLS = 4 * BRANCH_W
GATE_START = 9 * BRANCH_W
_R_POOL, _R_SU, _R_SV = 8, 9, 10


def _mixers(za, zr, zc_kv, T, p, tables, lam_init, *, latent):
    B = za.shape[0]
    sl = lambda a, c: a[:, :, c * LANES:c * LANES + BRANCH_W]
    if latent:
        k_all = jnp.concatenate([sl(zc_kv, _C_DK), sl(za, _C_DK)], axis=1)
        v_all = jnp.concatenate([sl(zc_kv, _C_DV), sl(za, _C_DV)], axis=1)
        diff = _flash(sl(za, _C_DQ), k_all, v_all, p['lam'], p['subln'], mode="diff", lam_init=lam_init)
        na = _na_attn(za, zc_kv, tables, qcol=_C_NQ, kcol=_C_NK, vcol=_C_NV)
    else:
        diff = _flash(sl(za, _C_DQ), sl(za, _C_DK), sl(za, _C_DV), p['lam'], p['subln'],
                      mode="diff", lam_init=lam_init)
        na = _flash(sl(za, _C_NQ), sl(za, _C_NK), sl(za, _C_NV), p['lam'], p['subln'], mode="dense")
    pool = _pool(zr, p['pool_w'], p['pool_scale'], col=_R_POOL)
    sgu = _sgu(zr, p['sgu_g'], p['sgu_w'], p['sgu_b'], ucol=_R_SU, vcol=_R_SV)
    M = B * T
    f2 = lambda a: a.reshape(M, a.shape[-1])
    return [f2(pool), f2(diff), f2(sgu), f2(na)]


def kernel(x, c, ctx, c_ctx, w_mod, b_mod, norm1_g, norm2_g, w_in, pool_w, pool_scale, diff_lambda,
           diff_subln_g, sgu_norm_g, sgu_w, sgu_b, na_rpb, w_br, w_out, router_w, moe_w1, moe_w3,
           moe_w2, final_g):
    B, T, D = x.shape
    Lc = ctx.shape[1]
    depth = w_in.shape[0]
    rows = T // GRID_W
    rope = _rope_tables(T)
    cvec = jnp.concatenate([c, c_ctx[None], jnp.zeros((8 - B - 1, D), F32)], axis=0)
    x_lat = x.reshape(B * T, D)
    x_ctx = ctx.reshape(B * Lc, D)
    for l in range(depth):
        last = l == depth - 1
        lam_init = 0.8 - 0.6 * math.exp(-0.3 * l)
        mod = _mod_vectors(cvec, w_mod[l], b_mod[l])
        mx = mod[:B].reshape(B, 6, 1, D)
        mc = mod[B:B + 1].reshape(1, 6, 1, D)
        sh1, sc1, g1, sh2, sc2, g2 = [mx[:, i] for i in range(6)]
        csh1, csc1, cg1, csh2, csc2, cg2 = [mc[:, i] for i in range(6)]
        w_in_b = w_in[l].astype(BF16)
        w_att = w_in_b[:, :ATT_COLS]
        w_rest = jnp.concatenate([w_in_b[:, GATE_START:], w_in_b[:, ATT_COLS:GATE_START]], axis=1)
        p = {'lam': diff_lambda[l], 'subln': diff_subln_g[l], 'pool_w': pool_w[l],
             'pool_scale': pool_scale[l], 'sgu_g': sgu_norm_g[l], 'sgu_w': sgu_w[l], 'sgu_b': sgu_b[l]}
        w_br_b = w_br[l].astype(BF16)
        w_out_b = w_out[l].astype(BF16)
        tables = _na_tables(na_rpb[l], rows)

        if last:
            zc_a = _norm_proj(x_ctx, norm1_g[l], csh1, csc1, w_att[:, :KV_COLS], BF16)
        else:
            zc_a = _norm_proj(x_ctx, norm1_g[l], csh1, csc1, w_att, BF16)
        zc_a = zc_a.reshape(B, Lc, -1)

        za = _norm_proj(x_lat, norm1_g[l], sh1, sc1, w_att, BF16, rope=rope,
                        rope_tiles=(_C_DK // 4, _C_DQ // 4)).reshape(B, T, ATT_COLS)
        zr = _norm_proj(x_lat, norm1_g[l], sh1, sc1, w_rest, F32).reshape(B, T, -1)
        br = _mixers(za, zr, zc_a, T, p, tables, lam_init, latent=True)
        x_lat, h2, logits = _merge(br, zr.reshape(B * T, -1), w_br_b, w_out_b, x_lat, g1,
                                   norm2_g[l], sh2, sc2, router_w[l])
        x_lat = _moe(x_lat.reshape(B, T, D), h2, logits, moe_w1[l], moe_w3[l], moe_w2[l], g2,
                     final_g, final=last).reshape(B * T, D)

        if not last:
            zc_r = _norm_proj(x_ctx, norm1_g[l], csh1, csc1, w_rest, F32).reshape(B, Lc, -1)
            brc = _mixers(zc_a, zc_r, None, Lc, p, None, lam_init, latent=False)
            x_ctx, hc2, lgc = _merge(brc, zc_r.reshape(B * Lc, -1), w_br_b, w_out_b, x_ctx, cg1,
                                     norm2_g[l], csh2, csc2, router_w[l])
            x_ctx = _moe(x_ctx.reshape(B, Lc, D), hc2, lgc, moe_w1[l], moe_w3[l], moe_w2[l], cg2,
                         final_g, final=False).reshape(B * Lc, D)
    return x_lat.reshape(B, T, D)
```

```python
import functools
import math

import numpy as np
import jax
import jax.numpy as jnp
from jax import lax
from jax.experimental import pallas as pl
from jax.experimental.pallas import tpu as pltpu

F32 = jnp.float32
BF16 = jnp.bfloat16

EPS = 1e-6
GRID_W = 64
LANES = 128
HALF = 64
ONES_ROWS = 16
N_BRANCH = 4
BRANCH_W = 512
POOL_WINDOWS = (2, 4, 8, 16)
POOL_HALO = 16
SGU_CHUNK = 128
NA_WIN_R = 8
NA_WIN_C = 16
NA_QROWS = 8
NA_KROWS = 16
N_EXPERTS = 16
EC_CAPACITY = 2
ROPE_BASE = 10000.0
NEG = -1e30
VMEM_LIMIT = 56 * 1024 * 1024

_NT = (((1,), (1,)), ((), ()))


def _params(*sem):
    return pltpu.CompilerParams(dimension_semantics=sem, vmem_limit_bytes=VMEM_LIMIT)


def _pick(n, cands):
    for c in cands:
        if n % c == 0:
            return c
    raise ValueError(f"no tile for {n}")


def _mod_kernel(c_ref, w_ref, b_ref, o_ref):
    c = c_ref[...]
    s = c * jax.nn.sigmoid(c)
    o_ref[...] = jnp.dot(s, w_ref[...], preferred_element_type=F32,
                         precision=lax.Precision.HIGHEST) + b_ref[...]


def _mod_vectors(cvec, w, b):
    D, N = w.shape
    tn = _pick(N, (1536, 1024, 512, 128))
    return pl.pallas_call(
        _mod_kernel,
        grid=(N // tn,),
        in_specs=[pl.BlockSpec((8, D), lambda j: (0, 0)),
                  pl.BlockSpec((D, tn), lambda j: (0, j)),
                  pl.BlockSpec((1, tn), lambda j: (0, j))],
        out_specs=pl.BlockSpec((8, tn), lambda j: (0, j)),
        out_shape=jax.ShapeDtypeStruct((8, N), F32),
        compiler_params=_params("parallel"),
        name="mod_vectors",
    )(cvec, w, b.reshape(1, N))


def _rope_store(acc, cos, sin, o_ref):
    lane = lax.broadcasted_iota(jnp.int32, (1, LANES), 1)
    first = (lane % 32) < 16
    for h in range(acc.shape[1] // LANES):
        a = acc[:, h * LANES:(h + 1) * LANES]
        partner = jnp.where(first, pltpu.roll(a, LANES - 16, 1), pltpu.roll(a, 16, 1))
        o_ref[:, h * LANES:(h + 1) * LANES] = (a * cos + partner * sin).astype(o_ref.dtype)


def _norm_proj_kernel(x_ref, g_ref, sh_ref, sc_ref, w_ref, *rest, rope_tiles):
    if rope_tiles:
        cos_ref, sin_ref, o_ref, h_ref = rest
    else:
        o_ref, h_ref = rest
    j = pl.program_id(1)

    @pl.when(j == 0)
    def _():
        x = x_ref[...]
        y = x * lax.rsqrt(jnp.mean(x * x, axis=-1, keepdims=True) + EPS) * g_ref[...]
        h_ref[...] = (y * (1.0 + sc_ref[...]) + sh_ref[...]).astype(BF16)

    acc = jnp.dot(h_ref[...], w_ref[...], preferred_element_type=F32)
    if rope_tiles:
        is_rope = functools.reduce(jnp.logical_or, [j == t for t in rope_tiles])

        @pl.when(is_rope)
        def _():
            _rope_store(acc, cos_ref[...], sin_ref[...], o_ref)

        @pl.when(jnp.logical_not(is_rope))
        def _():
            o_ref[...] = acc.astype(o_ref.dtype)
    else:
        o_ref[...] = acc.astype(o_ref.dtype)


def _norm_proj(x2, g, shift, scale, w, out_dtype, rope=None, rope_tiles=()):
    M, D = x2.shape
    N = w.shape[1]
    G = shift.shape[0]
    rows_per_group = M // G
    tm = _pick(rows_per_group, (1024, 512, 256))
    tn = 512
    assert N % tn == 0
    in_specs = [pl.BlockSpec((tm, D), lambda i, j: (i, 0)),
                pl.BlockSpec((1, D), lambda i, j: (0, 0)),
                pl.BlockSpec((None, 1, D), lambda i, j: (i * tm // rows_per_group, 0, 0)),
                pl.BlockSpec((None, 1, D), lambda i, j: (i * tm // rows_per_group, 0, 0)),
                pl.BlockSpec((D, tn), lambda i, j: (0, j))]
    args = [x2, g.reshape(1, D), shift, scale, w]
    if rope_tiles:
        cos, sin = rope
        nt = cos.shape[0] // tm
        in_specs += [pl.BlockSpec((tm, LANES), lambda i, j: (i % nt, 0)),
                     pl.BlockSpec((tm, LANES), lambda i, j: (i % nt, 0))]
        args += [cos, sin]
    return pl.pallas_call(
        functools.partial(_norm_proj_kernel, rope_tiles=tuple(rope_tiles)),
        grid=(M // tm, N // tn),
        in_specs=in_specs,
        out_specs=pl.BlockSpec((tm, tn), lambda i, j: (i, j)),
        out_shape=jax.ShapeDtypeStruct((M, N), out_dtype),
        scratch_shapes=[pltpu.VMEM((tm, D), BF16)],
        compiler_params=_params("parallel", "arbitrary"),
        name="norm_proj",
    )(*args)


def _rope_tables(T):
    t = jnp.arange(T)
    row = (t // GRID_W).astype(F32)
    col = (t % GRID_W).astype(F32)
    nf = HALF // 4
    inv = ROPE_BASE ** (-jnp.arange(nf, dtype=F32) / nf)
    ar = row[:, None] * inv
    ac = col[:, None] * inv
    cos = jnp.concatenate([jnp.cos(ar), jnp.cos(ar), jnp.cos(ac), jnp.cos(ac)], axis=-1)
    sin = jnp.concatenate([-jnp.sin(ar), jnp.sin(ar), -jnp.sin(ac), jnp.sin(ac)], axis=-1)
    return jnp.tile(cos, (1, 2)), jnp.tile(sin, (1, 2))


def _flash_kernel(q_ref, k_ref, v_ref, lam_ref, g_ref, o_ref, m_ref, acc_ref, sa_ref, sb_ref, *,
                  mode, tk, lam_init):
    tq = q_ref.shape[1]
    nk = k_ref.shape[0] // tk
    ones = jnp.ones((ONES_ROWS, tk), BF16)
    sub = lax.broadcasted_iota(jnp.int32, (LANES, 1), 0)
    lo = sub < HALF
    if mode == "diff":
        lp = lam_ref[...]
        d1 = jnp.sum(lp[0:1, :] * lp[1:2, :], keepdims=True)
        d2 = jnp.sum(lp[2:3, :] * lp[3:4, :], keepdims=True)
        lam = jnp.exp(d1) - jnp.exp(d2) + lam_init
    for h in range(q_ref.shape[0] // LANES):
        cs = slice(h * LANES, (h + 1) * LANES)
        qh = q_ref[cs, :]
        zero = jnp.zeros_like(qh)
        q2 = jnp.concatenate([jnp.where(lo, qh, zero), jnp.where(lo, zero, qh)], axis=1)
        q2 = q2 * jnp.asarray(HALF ** -0.5, BF16)
        m_ref[...] = jnp.full(m_ref.shape, -jnp.inf, F32)
        acc_ref[...] = jnp.zeros(acc_ref.shape, F32)

        def scores(c, s_ref, q2=q2, cs=cs):
            off = pl.multiple_of(c * tk, tk)
            s_ref[...] = jnp.dot(k_ref[pl.ds(off, tk), cs], q2, preferred_element_type=F32)

        def update(c, s_ref, cs=cs):
            off = pl.multiple_of(c * tk, tk)
            vc = jnp.concatenate([v_ref[cs, pl.ds(off, tk)], ones], axis=0)
            s = s_ref[...]
            m_old = m_ref[...]
            m_new = jnp.maximum(m_old, jnp.max(s, axis=0, keepdims=True))
            alpha = jnp.exp(m_old - m_new)
            p = jnp.exp(s - m_new).astype(BF16)
            acc_ref[...] = alpha * acc_ref[...] + jnp.dot(vc, p, preferred_element_type=F32)
            m_ref[...] = m_new

        scores(0, sa_ref)

        def body(j, carry):
            scores(2 * j + 1, sb_ref)
            update(2 * j, sa_ref)
            scores(2 * j + 2, sa_ref)
            update(2 * j + 1, sb_ref)
            return carry

        lax.fori_loop(0, (nk - 1) // 2, body, 0)
        update(nk - 1, sa_ref)
        a = acc_ref[0:LANES, :] / acc_ref[LANES:LANES + 1, :]
        a0 = a[:, :tq]
        a1 = a[:, tq:]
        if mode == "diff":
            o = a0 - lam * a1
            o = o * lax.rsqrt(jnp.mean(o * o, axis=0, keepdims=True) + EPS)
            o = o * g_ref[...] * (1.0 - lam_init)
        else:
            o = jnp.where(lo, a0, a1)
        o_ref[cs, :] = o.astype(o_ref.dtype)


def _flash(q, k, v, lam_p, g, *, mode, lam_init=0.0):
    B, Tq, W = q.shape
    Tk = k.shape[1]
    tq = _pick(Tq, (256, 128))
    tk = next(c for c in (768, 512, 384, 256, 128) if Tk % c == 0 and (Tk // c) % 2 == 1)
    out_t = pl.pallas_call(
        functools.partial(_flash_kernel, mode=mode, tk=tk, lam_init=lam_init),
        grid=(B, Tq // tq),
        in_specs=[pl.BlockSpec((None, W, tq), lambda b, i: (b, 0, i)),
                  pl.BlockSpec((None, Tk, W), lambda b, i: (b, 0, 0)),
                  pl.BlockSpec((None, W, Tk), lambda b, i: (b, 0, 0)),
                  pl.BlockSpec(lam_p.shape, lambda b, i: (0, 0)),
                  pl.BlockSpec((LANES, 1), lambda b, i: (0, 0))],
        out_specs=pl.BlockSpec((None, W, tq), lambda b, i: (b, 0, i)),
        out_shape=jax.ShapeDtypeStruct((B, W, Tq), BF16),
        scratch_shapes=[pltpu.VMEM((1, 2 * tq), F32), pltpu.VMEM((LANES + ONES_ROWS, 2 * tq), F32),
                        pltpu.VMEM((tk, 2 * tq), F32), pltpu.VMEM((tk, 2 * tq), F32)],
        compiler_params=_params("parallel", "parallel"),
        name="flash_" + mode,
    )(jnp.swapaxes(q, 1, 2), k, jnp.swapaxes(v, 1, 2), lam_p, g.reshape(LANES, 1))
    return jnp.swapaxes(out_t, 1, 2)


def _na_kernel(q_ref, k_ref, v_ref, kc_ref, vc_ref, b_ref, o_ref, *, rows):
    rb = pl.program_id(2)
    nq = NA_QROWS * GRID_W
    nkk = NA_KROWS * GRID_W
    k0 = jnp.clip(rb * NA_QROWS - NA_WIN_R // 2, 0, rows - NA_KROWS) * GRID_W
    k0 = pl.multiple_of(k0, 4 * GRID_W)
    kl = k_ref[pl.ds(k0, nkk), :]
    vl = v_ref[pl.ds(k0, nkk), :]
    kc = kc_ref[...]
    vc = vc_ref[...]
    q = q_ref[...]
    zero = jnp.zeros_like(q)
    lane = lax.broadcasted_iota(jnp.int32, (1, LANES), 1)
    lo = lane < HALF
    scale = jnp.asarray(HALF ** -0.5, BF16)
    outs = []
    for hh in range(2):
        qm = (jnp.where(lo, q, zero) if hh == 0 else jnp.where(lo, zero, q)) * scale
        s_loc = lax.dot_general(qm, kl, _NT, preferred_element_type=F32) + b_ref[hh]
        s_ctx = lax.dot_general(qm, kc, _NT, preferred_element_type=F32)
        m = jnp.maximum(jnp.max(s_loc, axis=-1, keepdims=True), jnp.max(s_ctx, axis=-1, keepdims=True))
        p_loc = jnp.exp(s_loc - m)
        p_ctx = jnp.exp(s_ctx - m)
        l = jnp.sum(p_loc, axis=-1, keepdims=True) + jnp.sum(p_ctx, axis=-1, keepdims=True)
        o = jnp.dot(p_loc.astype(BF16), vl, preferred_element_type=F32) \
            + jnp.dot(p_ctx.astype(BF16), vc, preferred_element_type=F32)
        outs.append(o / l)
    del nq
    o_ref[...] = jnp.where(lo, outs[0], outs[1]).astype(o_ref.dtype)


def _na_tables(rpb, rows):
    H = rpb.shape[0]
    c = np.arange(GRID_W)
    cstart = np.clip(c - NA_WIN_C // 2, 0, GRID_W - NA_WIN_C)
    kcol = np.arange(GRID_W)
    col_ok = (kcol[None, :] >= cstart[:, None]) & (kcol[None, :] < cstart[:, None] + NA_WIN_C)
    dc = np.clip(kcol[None, :] - c[:, None] + NA_WIN_C - 1, 0, 2 * NA_WIN_C - 2)
    cm = (np.arange(2 * NA_WIN_C - 1)[None, None, :] == dc[:, :, None]) & col_ok[:, :, None]
    tabs = []
    for q0, k0, clamp in ((0, 0, True), (NA_QROWS, NA_QROWS - NA_WIN_R // 2, False),
                          (rows - NA_QROWS, rows - NA_KROWS, True)):
        r = q0 + np.arange(NA_QROWS)
        kr = k0 + np.arange(NA_KROWS)
        r0 = r - NA_WIN_R // 2
        if clamp:
            r0 = np.clip(r0, 0, rows - NA_WIN_R)
        row_ok = (kr[None, :] >= r0[:, None]) & (kr[None, :] < r0[:, None] + NA_WIN_R)
        dr = np.clip(kr[None, :] - r[:, None] + NA_WIN_R - 1, 0, 2 * NA_WIN_R - 2)
        rm = (np.arange(2 * NA_WIN_R - 1)[None, None, :] == dr[:, :, None]) & row_ok[:, :, None]
        t = jnp.einsum('rki,hij,cqj->hrckq', jnp.asarray(rm, F32), rpb, jnp.asarray(cm, F32),
                       precision=lax.Precision.HIGHEST)
        ok = row_ok[:, None, :, None] & col_ok[None, :, None, :]
        t = jnp.where(jnp.asarray(ok)[None], t, NEG)
        tabs.append(t.reshape(H, NA_QROWS * GRID_W, NA_KROWS * GRID_W))
    return jnp.stack(tabs)


def _na_attn(z, zc, tables, *, qcol, kcol, vcol):
    B, T, _ = z.shape
    Lc = zc.shape[1]
    rows = T // GRID_W
    nrb = rows // NA_QROWS
    nq = NA_QROWS * GRID_W
    nkk = NA_KROWS * GRID_W
    nhp = BRANCH_W // LANES

    def var(rb):
        return jnp.where(rb == 0, 0, jnp.where(rb == nrb - 1, 2, 1))

    return pl.pallas_call(
        functools.partial(_na_kernel, rows=rows),
        grid=(nhp, B, nrb),
        in_specs=[pl.BlockSpec((None, nq, LANES), lambda h, b, r: (b, r, qcol + h)),
                  pl.BlockSpec((None, T, LANES), lambda h, b, r: (b, 0, kcol + h)),
                  pl.BlockSpec((None, T, LANES), lambda h, b, r: (b, 0, vcol + h)),
                  pl.BlockSpec((None, Lc, LANES), lambda h, b, r: (b, 0, kcol + h)),
                  pl.BlockSpec((None, Lc, LANES), lambda h, b, r: (b, 0, vcol + h)),
                  pl.BlockSpec((None, 2, nq, nkk), lambda h, b, r: (var(r), h, 0, 0))],
        out_specs=pl.BlockSpec((None, nq, LANES), lambda h, b, r: (b, r, h)),
        out_shape=jax.ShapeDtypeStruct((B, T, BRANCH_W), BF16),
        compiler_params=_params("parallel", "parallel", "parallel"),
        name="na_attn",
    )(z, z, z, zc, zc, tables)


def _pool_kernel(prev_ref, cur_ref, next_ref, w_ref, sc_ref, o_ref, *, T):
    i = pl.program_id(1)
    tt = cur_ref.shape[0]
    E = tt + 2 * POOL_HALO
    ext = jnp.concatenate([prev_ref[...], cur_ref[...], next_ref[...]], axis=0).astype(F32)
    gi = i * tt - POOL_HALO + lax.broadcasted_iota(jnp.int32, (E, 1), 0)
    ext = jnp.where((gi >= 0) & (gi < T), ext, 0.0)
    t = i * tt + lax.broadcasted_iota(jnp.int32, (tt, 1), 0)
    for g, w in enumerate(POOL_WINDOWS):
        cs = slice(g * LANES, (g + 1) * LANES)
        zg = ext[:, cs]
        s = zg + pltpu.roll(zg, 1, 0)
        half = 1
        while 2 * half < w:
            s = pltpu.roll(s, half, 0) + pltpu.roll(s, E - half, 0)
            half *= 2
        cnt = (jnp.minimum(t + w // 2, T) - jnp.maximum(t - w // 2, 0)).astype(F32)
        c0 = slice(POOL_HALO, POOL_HALO + tt)
        d = s[c0] / cnt - zg[c0]
        y = jnp.dot(d.astype(BF16), w_ref[g].astype(BF16), preferred_element_type=F32)
        o_ref[:, cs] = (y * sc_ref[:, cs]).astype(o_ref.dtype)


def _pool(z, w_pool, scale, *, col):
    B, T, _ = z.shape
    tt = _pick(T, (512, 256))
    hb = tt // POOL_HALO
    nh = T // POOL_HALO
    return pl.pallas_call(
        functools.partial(_pool_kernel, T=T),
        grid=(B, T // tt),
        in_specs=[pl.BlockSpec((None, POOL_HALO, BRANCH_W), lambda b, i: (b, jnp.maximum(i * hb - 1, 0), col)),
                  pl.BlockSpec((None, tt, BRANCH_W), lambda b, i: (b, i, col)),
                  pl.BlockSpec((None, POOL_HALO, BRANCH_W), lambda b, i: (b, jnp.minimum((i + 1) * hb, nh - 1), col)),
                  pl.BlockSpec(w_pool.shape, lambda b, i: (0, 0, 0)),
                  pl.BlockSpec((1, BRANCH_W), lambda b, i: (0, 0))],
        out_specs=pl.BlockSpec((None, tt, BRANCH_W), lambda b, i: (b, i, 0)),
        out_shape=jax.ShapeDtypeStruct((B, T, BRANCH_W), BF16),
        compiler_params=_params("parallel", "parallel"),
        name="pool_mixer",
    )(z, z, z, w_pool, scale.reshape(1, BRANCH_W))


def _sgu_kernel(u_ref, v_ref, g_ref, w_ref, bt_ref, o_ref):
    ts = u_ref.shape[0]
    u = jax.nn.gelu(u_ref[...].astype(F32))
    v = jax.nn.gelu(v_ref[...].astype(F32))
    vc = v - jnp.mean(v, axis=-1, keepdims=True)
    v = vc * lax.rsqrt(jnp.mean(vc * vc, axis=-1, keepdims=True) + EPS) * g_ref[...]
    vb = v.astype(BF16)
    for g in range(BRANCH_W // LANES):
        cs = slice(g * LANES, (g + 1) * LANES)
        wg = w_ref[g].astype(BF16)
        bg = bt_ref[:, g:g + 1]
        for ck in range(ts // SGU_CHUNK):
            rs = slice(ck * SGU_CHUNK, (ck + 1) * SGU_CHUNK)
            f = jnp.dot(wg, vb[rs, cs], preferred_element_type=F32) + bg
            o_ref[rs, cs] = (u[rs, cs] * f).astype(o_ref.dtype)


def _sgu(z, g, w_s, b, *, ucol, vcol):
    B, T, _ = z.shape
    ts = _pick(T, (512, 256, 128))
    return pl.pallas_call(
        _sgu_kernel,
        grid=(B, T // ts),
        in_specs=[pl.BlockSpec((None, ts, BRANCH_W), lambda b_, i: (b_, i, ucol)),
                  pl.BlockSpec((None, ts, BRANCH_W), lambda b_, i: (b_, i, vcol)),
                  pl.BlockSpec((1, BRANCH_W), lambda b_, i: (0, 0)),
                  pl.BlockSpec(w_s.shape, lambda b_, i: (0, 0, 0)),
                  pl.BlockSpec((SGU_CHUNK, BRANCH_W // LANES), lambda b_, i: (0, 0))],
        out_specs=pl.BlockSpec((None, ts, BRANCH_W), lambda b_, i: (b_, i, 0)),
        out_shape=jax.ShapeDtypeStruct((B, T, BRANCH_W), BF16),
        compiler_params=_params("parallel", "parallel"),
        name="sgu_mixer",
    )(z, z, g.reshape(1, BRANCH_W), w_s, b.T)


def _merge_kernel(p_ref, d_ref, s_ref, n_ref, gz_ref, wbr_ref, wout_ref, x_ref, g1_ref,
                  n2_ref, sh_ref, sc_ref, rw_ref, xo_ref, h_ref, lg_ref):
    D = x_ref.shape[1]
    acc = None
    for i, br in enumerate((p_ref, d_ref, s_ref, n_ref)):
        proj = jnp.dot(br[...], wbr_ref[i], preferred_element_type=F32)
        t = jax.nn.sigmoid(gz_ref[:, i * D:(i + 1) * D].astype(F32)) * proj
        acc = t if acc is None else acc + t
    mix = jnp.dot(acc.astype(BF16), wout_ref[...], preferred_element_type=F32)
    x = x_ref[...] + g1_ref[...] * mix
    xo_ref[...] = x
    y = x * lax.rsqrt(jnp.mean(x * x, axis=-1, keepdims=True) + EPS) * n2_ref[...]
    h = y * (1.0 + sc_ref[...]) + sh_ref[...]
    h_ref[...] = h.astype(BF16)
    lg_ref[...] = jnp.dot(h, rw_ref[...], preferred_element_type=F32,
                          precision=lax.Precision.HIGHEST)


def _merge(branches, zr, w_br, w_out, x2, g1, n2g, sh2, sc2, router_w):
    M, D = x2.shape
    G = g1.shape[0]
    rpg = M // G
    tm = _pick(rpg, (256,))
    E = router_w.shape[1]
    gb = 0
    grp = lambda i: (i * tm // rpg, 0, 0)
    row = lambda i: (i, 0)
    return pl.pallas_call(
        _merge_kernel,
        grid=(M // tm,),
        in_specs=[pl.BlockSpec((tm, BRANCH_W), row)] * 4 + [
            pl.BlockSpec((tm, N_BRANCH * D), lambda i: (i, gb)),
            pl.BlockSpec(w_br.shape, lambda i: (0, 0, 0)),
            pl.BlockSpec(w_out.shape, lambda i: (0, 0)),
            pl.BlockSpec((tm, D), row),
            pl.BlockSpec((None, 1, D), grp),
            pl.BlockSpec((1, D), lambda i: (0, 0)),
            pl.BlockSpec((None, 1, D), grp),
            pl.BlockSpec((None, 1, D), grp),
            pl.BlockSpec(router_w.shape, lambda i: (0, 0))],
        out_specs=[pl.BlockSpec((tm, D), row), pl.BlockSpec((tm, D), row), pl.BlockSpec((tm, E), row)],
        out_shape=[jax.ShapeDtypeStruct((M, D), F32), jax.ShapeDtypeStruct((M, D), BF16),
                   jax.ShapeDtypeStruct((M, E), F32)],
        compiler_params=_params("parallel"),
        name="merge_out",
    )(*branches, zr, w_br, w_out, x2, g1, n2g.reshape(1, D), sh2, sc2, router_w)


def _ffn_kernel(xs_ref, w1_ref, w3_ref, w2_ref, gate_ref, y_ref, w1b, w3b, w2b):
    @pl.when(pl.program_id(1) == 0)
    def _():
        w1b[...] = w1_ref[...].astype(BF16)
        w3b[...] = w3_ref[...].astype(BF16)
        w2b[...] = w2_ref[...].astype(BF16)

    xs = xs_ref[...]
    a = jnp.dot(xs, w1b[...], preferred_element_type=F32)
    b = jnp.dot(xs, w3b[...], preferred_element_type=F32)
    hid = (a * jax.nn.sigmoid(a) * b).astype(BF16)
    y = jnp.dot(hid, w2b[...], preferred_element_type=F32) * gate_ref[...]
    y_ref[...] = y.astype(y_ref.dtype)


def _ffn(xs, w1, w3, w2, gate):
    B, E, C, D = xs.shape
    Fd = w1.shape[2]
    return pl.pallas_call(
        _ffn_kernel,
        grid=(E, B),
        in_specs=[pl.BlockSpec((None, None, C, D), lambda e, b: (b, e, 0, 0)),
                  pl.BlockSpec((None, D, Fd), lambda e, b: (e, 0, 0)),
                  pl.BlockSpec((None, D, Fd), lambda e, b: (e, 0, 0)),
                  pl.BlockSpec((None, Fd, D), lambda e, b: (e, 0, 0)),
                  pl.BlockSpec((None, None, C, 1), lambda e, b: (b, e, 0, 0))],
        out_specs=pl.BlockSpec((None, None, C, D), lambda e, b: (b, e, 0, 0)),
        out_shape=jax.ShapeDtypeStruct((B, E, C, D), BF16),
        scratch_shapes=[pltpu.VMEM((D, Fd), BF16), pltpu.VMEM((D, Fd), BF16), pltpu.VMEM((Fd, D), BF16)],
        compiler_params=_params("parallel", "arbitrary"),
        name="expert_ffn",
    )(xs, w1, w3, w2, gate)


def _combine_kernel(starts_ref, x_ref, g2_ref, fg_ref, idx_ref, *rest, wb, nwin, final):
    ys = rest[:-2]
    o_ref, acc_ref = rest[-2:]
    b = pl.program_id(0)
    t = pl.program_id(1)
    tt = x_ref.shape[0]
    E = idx_ref.shape[0]
    C = idx_ref.shape[1]
    tok = t * tt + lax.broadcasted_iota(jnp.int32, (tt, 1), 0)
    acc_ref[...] = jnp.zeros(acc_ref.shape, F32)
    for e in range(E):
        s0 = starts_ref[b, e, t]
        s1 = starts_ref[b, e, t + 1]
        w0 = s0 // wb
        for k in range(nwin):
            wk = jnp.minimum(w0 + k, C // wb - 1)
            need = (s1 > s0) if k == 0 else ((s1 > s0) & ((s1 - 1) // wb > w0))

            @pl.when(need)
            def _(wk=wk, e=e, k=k):
                if wb == C:
                    ids = idx_ref[e:e + 1, :]
                else:
                    ids = idx_ref[e:e + 1, pl.ds(pl.multiple_of(wk * wb, wb), wb)]
                onehot = jnp.where(ids == tok, 1.0, 0.0).astype(BF16)
                acc_ref[...] += jnp.dot(onehot, ys[e * nwin + k][...], preferred_element_type=F32)

    x = x_ref[...] + g2_ref[...] * acc_ref[...]
    if final:
        x = x * lax.rsqrt(jnp.mean(x * x, axis=-1, keepdims=True) + EPS) * fg_ref[...]
    o_ref[...] = x


def _combine(x3, y, idx, g2, final_g, *, final):
    B, T, D = x3.shape
    E, C = idx.shape[1:]
    if C >= LANES:
        wb, nwin, tt = LANES, 2, LANES
    else:
        wb, nwin, tt = C, 1, T
    nt = T // tt
    G = g2.shape[0]
    bounds = jnp.arange(nt + 1, dtype=jnp.int32) * tt
    starts = jax.vmap(jax.vmap(lambda row: jnp.searchsorted(row, bounds, side='left')))(idx).astype(jnp.int32)

    def ymap(e, k):
        return lambda b, t, st: (b, e, jnp.minimum(st[b, e, t] // wb + k, C // wb - 1), 0)

    y_specs = [pl.BlockSpec((None, None, wb, D), ymap(e, k)) for e in range(E) for k in range(nwin)]
    grid_spec = pltpu.PrefetchScalarGridSpec(
        num_scalar_prefetch=1,
        grid=(B, nt),
        in_specs=[pl.BlockSpec((None, tt, D), lambda b, t, st: (b, t, 0)),
                  pl.BlockSpec((None, 1, D), lambda b, t, st: (b * G // B, 0, 0)),
                  pl.BlockSpec((1, D), lambda b, t, st: (0, 0)),
                  pl.BlockSpec((None, E, C), lambda b, t, st: (b, 0, 0))] + y_specs,
        out_specs=pl.BlockSpec((None, tt, D), lambda b, t, st: (b, t, 0)),
        scratch_shapes=[pltpu.VMEM((tt, D), F32)],
    )
    return pl.pallas_call(
        functools.partial(_combine_kernel, wb=wb, nwin=nwin, final=final),
        grid_spec=grid_spec,
        out_shape=jax.ShapeDtypeStruct((B, T, D), F32),
        compiler_params=_params("parallel", "arbitrary"),
        name="moe_combine",
    )(starts, x3, g2, final_g.reshape(1, D), idx, *([y] * (E * nwin)))


def _route(logits, B, T):
    E = logits.shape[-1]
    cap = EC_CAPACITY * T // E
    aff = jax.nn.softmax(logits.reshape(B, T, E), axis=-1)
    gate, idx = lax.top_k(jnp.swapaxes(aff, 1, 2), cap)
    order = jnp.argsort(idx, axis=-1)
    idx = jnp.take_along_axis(idx, order, axis=-1)
    gate = jnp.take_along_axis(gate, order, axis=-1)
    return idx.astype(jnp.int32), gate


def _moe(x3, h2, logits, w1, w3, w2, g2, final_g, *, final):
    B, T, D = x3.shape
    idx, gate = _route(logits, B, T)
    xs = jax.vmap(lambda hb, ib: hb[ib])(h2.reshape(B, T, D), idx)
    y = _ffn(xs, w1, w3, w2, gate[..., None])
    return _combine(x3, y, idx, g2, final_g, final=final)


_C_DK, _C_DV, _C_NK, _C_NV, _C_DQ, _C_NQ = 0, 4, 8, 12, 16, 20
ATT_COLS = 6 * BRANCH_W
KV_COLS = 4 * BRANCH_W
GATE_START = 9 * BRANCH_W
_R_POOL, _R_SU, _R_SV = 8, 9, 10


def _mixers(za, zr, zc_kv, T, p, tables, lam_init, *, latent):
    B = za.shape[0]
    sl = lambda a, c: a[:, :, c * LANES:c * LANES + BRANCH_W]
    if latent:
        k_all = jnp.concatenate([sl(zc_kv, _C_DK), sl(za, _C_DK)], axis=1)
        v_all = jnp.concatenate([sl(zc_kv, _C_DV), sl(za, _C_DV)], axis=1)
        diff = _flash(sl(za, _C_DQ), k_all, v_all, p['lam'], p['subln'], mode="diff", lam_init=lam_init)
        na = _na_attn(za, zc_kv, tables, qcol=_C_NQ, kcol=_C_NK, vcol=_C_NV)
    else:
        diff = _flash(sl(za, _C_DQ), sl(za, _C_DK), sl(za, _C_DV), p['lam'], p['subln'],
                      mode="diff", lam_init=lam_init)
        na = _flash(sl(za, _C_NQ), sl(za, _C_NK), sl(za, _C_NV), p['lam'], p['subln'], mode="dense")
    pool = _pool(zr, p['pool_w'], p['pool_scale'], col=_R_POOL)
    sgu = _sgu(zr, p['sgu_g'], p['sgu_w'], p['sgu_b'], ucol=_R_SU, vcol=_R_SV)
    M = B * T
    f2 = lambda a: a.reshape(M, a.shape[-1])
    return [f2(pool), f2(diff), f2(sgu), f2(na)]


def kernel(x, c, ctx, c_ctx, w_mod, b_mod, norm1_g, norm2_g, w_in, pool_w, pool_scale, diff_lambda,
           diff_subln_g, sgu_norm_g, sgu_w, sgu_b, na_rpb, w_br, w_out, router_w, moe_w1, moe_w3,
           moe_w2, final_g):
    B, T, D = x.shape
    Lc = ctx.shape[1]
    depth = w_in.shape[0]
    rows = T // GRID_W
    rope = _rope_tables(T)
    cvec = jnp.concatenate([c, c_ctx[None], jnp.zeros((8 - B - 1, D), F32)], axis=0)
    x_lat = x.reshape(B * T, D)
    x_ctx = ctx.reshape(B * Lc, D)
    for l in range(depth):
        last = l == depth - 1
        lam_init = 0.8 - 0.6 * math.exp(-0.3 * l)
        mod = _mod_vectors(cvec, w_mod[l], b_mod[l])
        mx = mod[:B].reshape(B, 6, 1, D)
        mc = mod[B:B + 1].reshape(1, 6, 1, D)
        sh1, sc1, g1, sh2, sc2, g2 = [mx[:, i] for i in range(6)]
        csh1, csc1, cg1, csh2, csc2, cg2 = [mc[:, i] for i in range(6)]
        w_in_b = w_in[l].astype(BF16)
        w_att = w_in_b[:, :ATT_COLS]
        w_rest = jnp.concatenate([w_in_b[:, GATE_START:], w_in_b[:, ATT_COLS:GATE_START]], axis=1)
        p = {'lam': diff_lambda[l], 'subln': diff_subln_g[l], 'pool_w': pool_w[l],
             'pool_scale': pool_scale[l], 'sgu_g': sgu_norm_g[l], 'sgu_w': sgu_w[l], 'sgu_b': sgu_b[l]}
        w_br_b = w_br[l].astype(BF16)
        w_out_b = w_out[l].astype(BF16)
        tables = _na_tables(na_rpb[l], rows)

        if last:
            zc_a = _norm_proj(x_ctx, norm1_g[l], csh1, csc1, w_att[:, :KV_COLS], BF16)
        else:
            zc_a = _norm_proj(x_ctx, norm1_g[l], csh1, csc1, w_att, BF16)
        zc_a = zc_a.reshape(B, Lc, -1)

        za = _norm_proj(x_lat, norm1_g[l], sh1, sc1, w_att, BF16, rope=rope,
                        rope_tiles=(_C_DK // 4, _C_DQ // 4)).reshape(B, T, ATT_COLS)
        zr = _norm_proj(x_lat, norm1_g[l], sh1, sc1, w_rest, F32).reshape(B, T, -1)
        br = _mixers(za, zr, zc_a, T, p, tables, lam_init, latent=True)
        x_lat, h2, logits = _merge(br, zr.reshape(B * T, -1), w_br_b, w_out_b, x_lat, g1,
                                   norm2_g[l], sh2, sc2, router_w[l])
        x_lat = _moe(x_lat.reshape(B, T, D), h2, logits, moe_w1[l], moe_w3[l], moe_w2[l], g2,
                     final_g, final=last).reshape(B * T, D)

        if not last:
            zc_r = _norm_proj(x_ctx, norm1_g[l], csh1, csc1, w_rest, F32).reshape(B, Lc, -1)
            brc = _mixers(zc_a, zc_r, None, Lc, p, None, lam_init, latent=False)
            x_ctx, hc2, lgc = _merge(brc, zc_r.reshape(B * Lc, -1), w_br_b, w_out_b, x_ctx, cg1,
                                     norm2_g[l], csh2, csc2, router_w[l])
            x_ctx = _moe(x_ctx.reshape(B, Lc, D), hc2, lgc, moe_w1[l], moe_w3[l], moe_w2[l], cg2,
                         final_g, final=False).reshape(B * Lc, D)
    return x_lat.reshape(B, T, D)
```

```python
import functools
import math

import numpy as np
import jax
import jax.numpy as jnp
from jax import lax
from jax.experimental import pallas as pl
from jax.experimental.pallas import tpu as pltpu

F32 = jnp.float32
BF16 = jnp.bfloat16

EPS = 1e-6
GRID_W = 64
LANES = 128
SUBLANES = 8
HALF = 64
ONES_ROWS = 16
N_BRANCH = 4
BRANCH_W = 512
POOL_WINDOWS = (2, 4, 8, 16)
POOL_HALO = 16
SGU_CHUNK = 128
NA_WIN_R = 8
NA_WIN_C = 16
NA_QROWS = 8
NA_KROWS = 16
N_EXPERTS = 16
EC_CAPACITY = 2
ROUTE_BLK = 128
F32_MANT_BITS = 23
F32_EXP_BIAS = 127
ROPE_BASE = 10000.0
NEG = -1e30
VMEM_LIMIT = 56 * 1024 * 1024

_NT = (((1,), (1,)), ((), ()))


def _params(*sem):
    return pltpu.CompilerParams(dimension_semantics=sem, vmem_limit_bytes=VMEM_LIMIT)


def _pick(n, cands):
    for c in cands:
        if n % c == 0:
            return c
    raise ValueError(f"no tile for {n}")


def _mod_kernel(c_ref, w_ref, b_ref, o_ref):
    c = c_ref[...]
    s = c * jax.nn.sigmoid(c)
    o_ref[...] = jnp.dot(s, w_ref[...], preferred_element_type=F32,
                         precision=lax.Precision.HIGHEST) + b_ref[...]


def _mod_vectors(cvec, w, b):
    D, N = w.shape
    tn = _pick(N, (1536, 1024, 512, 128))
    return pl.pallas_call(
        _mod_kernel,
        grid=(N // tn,),
        in_specs=[pl.BlockSpec((8, D), lambda j: (0, 0)),
                  pl.BlockSpec((D, tn), lambda j: (0, j)),
                  pl.BlockSpec((1, tn), lambda j: (0, j))],
        out_specs=pl.BlockSpec((8, tn), lambda j: (0, j)),
        out_shape=jax.ShapeDtypeStruct((8, N), F32),
        compiler_params=_params("parallel"),
        name="mod_vectors",
    )(cvec, w, b.reshape(1, N))


def _rope_store(acc, cos, sin, o_ref):
    lane = lax.broadcasted_iota(jnp.int32, (1, LANES), 1)
    first = (lane % 32) < 16
    for h in range(acc.shape[1] // LANES):
        a = acc[:, h * LANES:(h + 1) * LANES]
        partner = jnp.where(first, pltpu.roll(a, LANES - 16, 1), pltpu.roll(a, 16, 1))
        o_ref[:, h * LANES:(h + 1) * LANES] = (a * cos + partner * sin).astype(o_ref.dtype)


def _norm_proj_kernel(x_ref, g_ref, sh_ref, sc_ref, w_ref, *rest, rope_tiles):
    if rope_tiles:
        cos_ref, sin_ref, o_ref, h_ref = rest
    else:
        o_ref, h_ref = rest
    j = pl.program_id(1)

    @pl.when(j == 0)
    def _():
        x = x_ref[...]
        y = x * lax.rsqrt(jnp.mean(x * x, axis=-1, keepdims=True) + EPS) * g_ref[...]
        h_ref[...] = (y * (1.0 + sc_ref[...]) + sh_ref[...]).astype(BF16)

    acc = jnp.dot(h_ref[...], w_ref[...], preferred_element_type=F32)
    if rope_tiles:
        is_rope = functools.reduce(jnp.logical_or, [j == t for t in rope_tiles])

        @pl.when(is_rope)
        def _():
            _rope_store(acc, cos_ref[...], sin_ref[...], o_ref)

        @pl.when(jnp.logical_not(is_rope))
        def _():
            o_ref[...] = acc.astype(o_ref.dtype)
    else:
        o_ref[...] = acc.astype(o_ref.dtype)


def _norm_proj(x2, g, shift, scale, w, out_dtype, rope=None, rope_tiles=()):
    M, D = x2.shape
    N = w.shape[1]
    G = shift.shape[0]
    rows_per_group = M // G
    tm = _pick(rows_per_group, (1024, 512, 256))
    tn = 512
    assert N % tn == 0
    in_specs = [pl.BlockSpec((tm, D), lambda i, j: (i, 0)),
                pl.BlockSpec((1, D), lambda i, j: (0, 0)),
                pl.BlockSpec((None, 1, D), lambda i, j: (i * tm // rows_per_group, 0, 0)),
                pl.BlockSpec((None, 1, D), lambda i, j: (i * tm // rows_per_group, 0, 0)),
                pl.BlockSpec((D, tn), lambda i, j: (0, j))]
    args = [x2, g.reshape(1, D), shift, scale, w]
    if rope_tiles:
        cos, sin = rope
        nt = cos.shape[0] // tm
        in_specs += [pl.BlockSpec((tm, LANES), lambda i, j: (i % nt, 0)),
                     pl.BlockSpec((tm, LANES), lambda i, j: (i % nt, 0))]
        args += [cos, sin]
    return pl.pallas_call(
        functools.partial(_norm_proj_kernel, rope_tiles=tuple(rope_tiles)),
        grid=(M // tm, N // tn),
        in_specs=in_specs,
        out_specs=pl.BlockSpec((tm, tn), lambda i, j: (i, j)),
        out_shape=jax.ShapeDtypeStruct((M, N), out_dtype),
        scratch_shapes=[pltpu.VMEM((tm, D), BF16)],
        compiler_params=_params("parallel", "arbitrary"),
        name="norm_proj",
    )(*args)


def _rope_tables(T):
    t = jnp.arange(T)
    row = (t // GRID_W).astype(F32)
    col = (t % GRID_W).astype(F32)
    nf = HALF // 4
    inv = ROPE_BASE ** (-jnp.arange(nf, dtype=F32) / nf)
    ar = row[:, None] * inv
    ac = col[:, None] * inv
    cos = jnp.concatenate([jnp.cos(ar), jnp.cos(ar), jnp.cos(ac), jnp.cos(ac)], axis=-1)
    sin = jnp.concatenate([-jnp.sin(ar), jnp.sin(ar), -jnp.sin(ac), jnp.sin(ac)], axis=-1)
    return jnp.tile(cos, (1, 2)), jnp.tile(sin, (1, 2))


def _flash_kernel(q_ref, k_ref, v_ref, lam_ref, g_ref, o_ref, m_ref, acc_ref, sa_ref, sb_ref, *,
                  mode, tk, lam_init):
    tq = q_ref.shape[1]
    nk = k_ref.shape[0] // tk
    ones = jnp.ones((ONES_ROWS, tk), BF16)
    sub = lax.broadcasted_iota(jnp.int32, (LANES, 1), 0)
    lo = sub < HALF
    if mode == "diff":
        lp = lam_ref[...]
        d1 = jnp.sum(lp[0:1, :] * lp[1:2, :], keepdims=True)
        d2 = jnp.sum(lp[2:3, :] * lp[3:4, :], keepdims=True)
        lam = jnp.exp(d1) - jnp.exp(d2) + lam_init
    for h in range(q_ref.shape[0] // LANES):
        cs = slice(h * LANES, (h + 1) * LANES)
        qh = q_ref[cs, :]
        zero = jnp.zeros_like(qh)
        q2 = jnp.concatenate([jnp.where(lo, qh, zero), jnp.where(lo, zero, qh)], axis=1)
        q2 = q2 * jnp.asarray(HALF ** -0.5, BF16)
        m_ref[...] = jnp.full(m_ref.shape, -jnp.inf, F32)
        acc_ref[...] = jnp.zeros(acc_ref.shape, F32)

        def scores(c, s_ref, q2=q2, cs=cs):
            off = pl.multiple_of(c * tk, tk)
            s_ref[...] = jnp.dot(k_ref[pl.ds(off, tk), cs], q2, preferred_element_type=F32)

        def update(c, s_ref, cs=cs):
            off = pl.multiple_of(c * tk, tk)
            vc = jnp.concatenate([v_ref[cs, pl.ds(off, tk)], ones], axis=0)
            s = s_ref[...]
            m_old = m_ref[...]
            m_new = jnp.maximum(m_old, jnp.max(s, axis=0, keepdims=True))
            alpha = jnp.exp(m_old - m_new)
            p = jnp.exp(s - m_new).astype(BF16)
            acc_ref[...] = alpha * acc_ref[...] + jnp.dot(vc, p, preferred_element_type=F32)
            m_ref[...] = m_new

        scores(0, sa_ref)

        def body(j, carry):
            scores(2 * j + 1, sb_ref)
            update(2 * j, sa_ref)
            scores(2 * j + 2, sa_ref)
            update(2 * j + 1, sb_ref)
            return carry

        lax.fori_loop(0, (nk - 1) // 2, body, 0)
        update(nk - 1, sa_ref)
        a = acc_ref[0:LANES, :] / acc_ref[LANES:LANES + 1, :]
        a0 = a[:, :tq]
        a1 = a[:, tq:]
        if mode == "diff":
            o = a0 - lam * a1
            o = o * lax.rsqrt(jnp.mean(o * o, axis=0, keepdims=True) + EPS)
            o = o * g_ref[...] * (1.0 - lam_init)
        else:
            o = jnp.where(lo, a0, a1)
        o_ref[cs, :] = o.astype(o_ref.dtype)


def _flash(q, k, v, lam_p, g, *, mode, lam_init=0.0):
    B, Tq, W = q.shape
    Tk = k.shape[1]
    tq = _pick(Tq, (256, 128))
    tk = next(c for c in (768, 512, 384, 256, 128) if Tk % c == 0 and (Tk // c) % 2 == 1)
    out_t = pl.pallas_call(
        functools.partial(_flash_kernel, mode=mode, tk=tk, lam_init=lam_init),
        grid=(B, Tq // tq),
        in_specs=[pl.BlockSpec((None, W, tq), lambda b, i: (b, 0, i)),
                  pl.BlockSpec((None, Tk, W), lambda b, i: (b, 0, 0)),
                  pl.BlockSpec((None, W, Tk), lambda b, i: (b, 0, 0)),
                  pl.BlockSpec(lam_p.shape, lambda b, i: (0, 0)),
                  pl.BlockSpec((LANES, 1), lambda b, i: (0, 0))],
        out_specs=pl.BlockSpec((None, W, tq), lambda b, i: (b, 0, i)),
        out_shape=jax.ShapeDtypeStruct((B, W, Tq), BF16),
        scratch_shapes=[pltpu.VMEM((1, 2 * tq), F32), pltpu.VMEM((LANES + ONES_ROWS, 2 * tq), F32),
                        pltpu.VMEM((tk, 2 * tq), F32), pltpu.VMEM((tk, 2 * tq), F32)],
        compiler_params=_params("parallel", "parallel"),
        name="flash_" + mode,
    )(jnp.swapaxes(q, 1, 2), k, jnp.swapaxes(v, 1, 2), lam_p, g.reshape(LANES, 1))
    return jnp.swapaxes(out_t, 1, 2)


def _na_kernel(q_ref, k_ref, v_ref, kc_ref, vc_ref, b_ref, o_ref, *, rows):
    rb = pl.program_id(2)
    nq = NA_QROWS * GRID_W
    nkk = NA_KROWS * GRID_W
    k0 = jnp.clip(rb * NA_QROWS - NA_WIN_R // 2, 0, rows - NA_KROWS) * GRID_W
    k0 = pl.multiple_of(k0, 4 * GRID_W)
    kl = k_ref[pl.ds(k0, nkk), :]
    vl = v_ref[pl.ds(k0, nkk), :]
    kc = kc_ref[...]
    vc = vc_ref[...]
    q = q_ref[...]
    zero = jnp.zeros_like(q)
    lane = lax.broadcasted_iota(jnp.int32, (1, LANES), 1)
    lo = lane < HALF
    scale = jnp.asarray(HALF ** -0.5, BF16)
    outs = []
    for hh in range(2):
        qm = (jnp.where(lo, q, zero) if hh == 0 else jnp.where(lo, zero, q)) * scale
        s_loc = lax.dot_general(qm, kl, _NT, preferred_element_type=F32) + b_ref[hh]
        s_ctx = lax.dot_general(qm, kc, _NT, preferred_element_type=F32)
        m = jnp.maximum(jnp.max(s_loc, axis=-1, keepdims=True), jnp.max(s_ctx, axis=-1, keepdims=True))
        p_loc = jnp.exp(s_loc - m)
        p_ctx = jnp.exp(s_ctx - m)
        l = jnp.sum(p_loc, axis=-1, keepdims=True) + jnp.sum(p_ctx, axis=-1, keepdims=True)
        o = jnp.dot(p_loc.astype(BF16), vl, preferred_element_type=F32) \
            + jnp.dot(p_ctx.astype(BF16), vc, preferred_element_type=F32)
        outs.append(o / l)
    del nq
    o_ref[...] = jnp.where(lo, outs[0], outs[1]).astype(o_ref.dtype)


def _na_tables(rpb, rows):
    H = rpb.shape[0]
    c = np.arange(GRID_W)
    cstart = np.clip(c - NA_WIN_C // 2, 0, GRID_W - NA_WIN_C)
    kcol = np.arange(GRID_W)
    col_ok = (kcol[None, :] >= cstart[:, None]) & (kcol[None, :] < cstart[:, None] + NA_WIN_C)
    dc = np.clip(kcol[None, :] - c[:, None] + NA_WIN_C - 1, 0, 2 * NA_WIN_C - 2)
    cm = (np.arange(2 * NA_WIN_C - 1)[None, None, :] == dc[:, :, None]) & col_ok[:, :, None]
    tabs = []
    for q0, k0, clamp in ((0, 0, True), (NA_QROWS, NA_QROWS - NA_WIN_R // 2, False),
                          (rows - NA_QROWS, rows - NA_KROWS, True)):
        r = q0 + np.arange(NA_QROWS)
        kr = k0 + np.arange(NA_KROWS)
        r0 = r - NA_WIN_R // 2
        if clamp:
            r0 = np.clip(r0, 0, rows - NA_WIN_R)
        row_ok = (kr[None, :] >= r0[:, None]) & (kr[None, :] < r0[:, None] + NA_WIN_R)
        dr = np.clip(kr[None, :] - r[:, None] + NA_WIN_R - 1, 0, 2 * NA_WIN_R - 2)
        rm = (np.arange(2 * NA_WIN_R - 1)[None, None, :] == dr[:, :, None]) & row_ok[:, :, None]
        t = jnp.einsum('rki,hij,cqj->hrckq', jnp.asarray(rm, F32), rpb, jnp.asarray(cm, F32),
                       precision=lax.Precision.HIGHEST)
        ok = row_ok[:, None, :, None] & col_ok[None, :, None, :]
        t = jnp.where(jnp.asarray(ok)[None], t, NEG)
        tabs.append(t.reshape(H, NA_QROWS * GRID_W, NA_KROWS * GRID_W))
    return jnp.stack(tabs)


def _na_attn(z, zc, tables, *, qcol, kcol, vcol):
    B, T, _ = z.shape
    Lc = zc.shape[1]
    rows = T // GRID_W
    nrb = rows // NA_QROWS
    nq = NA_QROWS * GRID_W
    nkk = NA_KROWS * GRID_W
    nhp = BRANCH_W // LANES

    def var(rb):
        return jnp.where(rb == 0, 0, jnp.where(rb == nrb - 1, 2, 1))

    return pl.pallas_call(
        functools.partial(_na_kernel, rows=rows),
        grid=(nhp, B, nrb),
        in_specs=[pl.BlockSpec((None, nq, LANES), lambda h, b, r: (b, r, qcol + h)),
                  pl.BlockSpec((None, T, LANES), lambda h, b, r: (b, 0, kcol + h)),
                  pl.BlockSpec((None, T, LANES), lambda h, b, r: (b, 0, vcol + h)),
                  pl.BlockSpec((None, Lc, LANES), lambda h, b, r: (b, 0, kcol + h)),
                  pl.BlockSpec((None, Lc, LANES), lambda h, b, r: (b, 0, vcol + h)),
                  pl.BlockSpec((None, 2, nq, nkk), lambda h, b, r: (var(r), h, 0, 0))],
        out_specs=pl.BlockSpec((None, nq, LANES), lambda h, b, r: (b, r, h)),
        out_shape=jax.ShapeDtypeStruct((B, T, BRANCH_W), BF16),
        compiler_params=_params("parallel", "parallel", "parallel"),
        name="na_attn",
    )(z, z, z, zc, zc, tables)


def _pool_kernel(prev_ref, cur_ref, next_ref, w_ref, sc_ref, o_ref, *, T):
    i = pl.program_id(1)
    tt = cur_ref.shape[0]
    E = tt + 2 * POOL_HALO
    ext = jnp.concatenate([prev_ref[...], cur_ref[...], next_ref[...]], axis=0).astype(F32)
    gi = i * tt - POOL_HALO + lax.broadcasted_iota(jnp.int32, (E, 1), 0)
    ext = jnp.where((gi >= 0) & (gi < T), ext, 0.0)
    t = i * tt + lax.broadcasted_iota(jnp.int32, (tt, 1), 0)
    for g, w in enumerate(POOL_WINDOWS):
        cs = slice(g * LANES, (g + 1) * LANES)
        zg = ext[:, cs]
        s = zg + pltpu.roll(zg, 1, 0)
        half = 1
        while 2 * half < w:
            s = pltpu.roll(s, half, 0) + pltpu.roll(s, E - half, 0)
            half *= 2
        cnt = (jnp.minimum(t + w // 2, T) - jnp.maximum(t - w // 2, 0)).astype(F32)
        c0 = slice(POOL_HALO, POOL_HALO + tt)
        d = s[c0] / cnt - zg[c0]
        y = jnp.dot(d.astype(BF16), w_ref[g].astype(BF16), preferred_element_type=F32)
        o_ref[:, cs] = (y * sc_ref[:, cs]).astype(o_ref.dtype)


def _pool(z, w_pool, scale, *, col):
    B, T, _ = z.shape
    tt = _pick(T, (512, 256))
    hb = tt // POOL_HALO
    nh = T // POOL_HALO
    return pl.pallas_call(
        functools.partial(_pool_kernel, T=T),
        grid=(B, T // tt),
        in_specs=[pl.BlockSpec((None, POOL_HALO, BRANCH_W), lambda b, i: (b, jnp.maximum(i * hb - 1, 0), col)),
                  pl.BlockSpec((None, tt, BRANCH_W), lambda b, i: (b, i, col)),
                  pl.BlockSpec((None, POOL_HALO, BRANCH_W), lambda b, i: (b, jnp.minimum((i + 1) * hb, nh - 1), col)),
                  pl.BlockSpec(w_pool.shape, lambda b, i: (0, 0, 0)),
                  pl.BlockSpec((1, BRANCH_W), lambda b, i: (0, 0))],
        out_specs=pl.BlockSpec((None, tt, BRANCH_W), lambda b, i: (b, i, 0)),
        out_shape=jax.ShapeDtypeStruct((B, T, BRANCH_W), BF16),
        compiler_params=_params("parallel", "parallel"),
        name="pool_mixer",
    )(z, z, z, w_pool, scale.reshape(1, BRANCH_W))


def _sgu_kernel(u_ref, v_ref, g_ref, w_ref, bt_ref, o_ref):
    ts = u_ref.shape[0]
    u = jax.nn.gelu(u_ref[...].astype(F32))
    v = jax.nn.gelu(v_ref[...].astype(F32))
    vc = v - jnp.mean(v, axis=-1, keepdims=True)
    v = vc * lax.rsqrt(jnp.mean(vc * vc, axis=-1, keepdims=True) + EPS) * g_ref[...]
    vb = v.astype(BF16)
    for g in range(BRANCH_W // LANES):
        cs = slice(g * LANES, (g + 1) * LANES)
        wg = w_ref[g].astype(BF16)
        bg = bt_ref[:, g:g + 1]
        for ck in range(ts // SGU_CHUNK):
            rs = slice(ck * SGU_CHUNK, (ck + 1) * SGU_CHUNK)
            f = jnp.dot(wg, vb[rs, cs], preferred_element_type=F32) + bg
            o_ref[rs, cs] = (u[rs, cs] * f).astype(o_ref.dtype)


def _sgu(z, g, w_s, b, *, ucol, vcol):
    B, T, _ = z.shape
    ts = _pick(T, (512, 256, 128))
    return pl.pallas_call(
        _sgu_kernel,
        grid=(B, T // ts),
        in_specs=[pl.BlockSpec((None, ts, BRANCH_W), lambda b_, i: (b_, i, ucol)),
                  pl.BlockSpec((None, ts, BRANCH_W), lambda b_, i: (b_, i, vcol)),
                  pl.BlockSpec((1, BRANCH_W), lambda b_, i: (0, 0)),
                  pl.BlockSpec(w_s.shape, lambda b_, i: (0, 0, 0)),
                  pl.BlockSpec((SGU_CHUNK, BRANCH_W // LANES), lambda b_, i: (0, 0))],
        out_specs=pl.BlockSpec((None, ts, BRANCH_W), lambda b_, i: (b_, i, 0)),
        out_shape=jax.ShapeDtypeStruct((B, T, BRANCH_W), BF16),
        compiler_params=_params("parallel", "parallel"),
        name="sgu_mixer",
    )(z, z, g.reshape(1, BRANCH_W), w_s, b.T)


def _merge_kernel(p_ref, d_ref, s_ref, n_ref, gz_ref, wbr_ref, wout_ref, x_ref, g1_ref,
                  n2_ref, sh_ref, sc_ref, rw_ref, xo_ref, h_ref, *, n_experts):
    D = x_ref.shape[1]
    acc = None
    for i, br in enumerate((p_ref, d_ref, s_ref, n_ref)):
        proj = jnp.dot(br[...], wbr_ref[i], preferred_element_type=F32)
        t = jax.nn.sigmoid(gz_ref[:, i * D:(i + 1) * D].astype(F32)) * proj
        acc = t if acc is None else acc + t
    mix = jnp.dot(acc.astype(BF16), wout_ref[...], preferred_element_type=F32)
    x = x_ref[...] + g1_ref[...] * mix
    xo_ref[...] = x
    y = x * lax.rsqrt(jnp.mean(x * x, axis=-1, keepdims=True) + EPS) * n2_ref[...]
    h = y * (1.0 + sc_ref[...]) + sh_ref[...]
    h_ref[:, 0:D] = h
    lg = jnp.dot(h, rw_ref[...], preferred_element_type=F32, precision=lax.Precision.HIGHEST)
    lane = lax.broadcasted_iota(jnp.int32, (1, LANES), 1)
    lg = jnp.where(lane < n_experts, lg, -jnp.inf)
    ex = jnp.exp(lg - jnp.max(lg, axis=-1, keepdims=True))
    h_ref[:, D:D + LANES] = ex / jnp.sum(ex, axis=-1, keepdims=True)


def _merge(branches, zr, w_br, w_out, x2, g1, n2g, sh2, sc2, router_w):
    M, D = x2.shape
    G = g1.shape[0]
    rpg = M // G
    tm = _pick(rpg, (256,))
    E = router_w.shape[1]
    router_w = jnp.pad(router_w, ((0, 0), (0, LANES - E)))
    gb = 0
    grp = lambda i: (i * tm // rpg, 0, 0)
    row = lambda i: (i, 0)
    return pl.pallas_call(
        functools.partial(_merge_kernel, n_experts=E),
        grid=(M // tm,),
        in_specs=[pl.BlockSpec((tm, BRANCH_W), row)] * 4 + [
            pl.BlockSpec((tm, N_BRANCH * D), lambda i: (i, gb)),
            pl.BlockSpec(w_br.shape, lambda i: (0, 0, 0)),
            pl.BlockSpec(w_out.shape, lambda i: (0, 0)),
            pl.BlockSpec((tm, D), row),
            pl.BlockSpec((None, 1, D), grp),
            pl.BlockSpec((1, D), lambda i: (0, 0)),
            pl.BlockSpec((None, 1, D), grp),
            pl.BlockSpec((None, 1, D), grp),
            pl.BlockSpec(router_w.shape, lambda i: (0, 0))],
        out_specs=[pl.BlockSpec((tm, D), row), pl.BlockSpec((tm, D + LANES), row)],
        out_shape=[jax.ShapeDtypeStruct((M, D), F32), jax.ShapeDtypeStruct((M, D + LANES), F32)],
        compiler_params=_params("parallel"),
        name="merge_out",
    )(*branches, zr, w_br, w_out, x2, g1, n2g.reshape(1, D), sh2, sc2, router_w)


def _ffn_kernel(idx_ref, h_hbm, w1_ref, w3_ref, w2_ref, y_ref, w1b, w3b, w2b, xbuf, sem, *, T):
    e = pl.program_id(0)
    b = pl.program_id(1)
    C = xbuf.shape[0]
    D = w1_ref.shape[0]

    def row_copy(j):
        row = idx_ref[b, e, j] + b * T
        return pltpu.make_async_copy(h_hbm.at[pl.ds(row, 1)], xbuf.at[pl.ds(j, 1)], sem)

    def issue(j, carry):
        row_copy(j).start()
        return carry

    lax.fori_loop(0, C, issue, 0)

    @pl.when(b == 0)
    def _():
        w1b[...] = w1_ref[...].astype(BF16)
        w3b[...] = w3_ref[...].astype(BF16)
        w2b[...] = w2_ref[...].astype(BF16)

    pltpu.make_async_copy(h_hbm.at[pl.ds(0, C)], xbuf, sem).wait()
    xs = xbuf[:, 0:D].astype(BF16)
    lane = lax.broadcasted_iota(jnp.int32, (1, LANES), 1)
    gate = jnp.sum(jnp.where(lane == e, xbuf[:, D:D + LANES], 0.0), axis=-1, keepdims=True)
    a = jnp.dot(xs, w1b[...], preferred_element_type=F32)
    g = jnp.dot(xs, w3b[...], preferred_element_type=F32)
    hid = (a * jax.nn.sigmoid(a) * g).astype(BF16)
    y = jnp.dot(hid, w2b[...], preferred_element_type=F32) * gate
    y_ref[...] = y.astype(y_ref.dtype)


def _ffn(hext, idx, w1, w3, w2, T):
    B, E, C = idx.shape
    D, Fd = w1.shape[1:]
    grid_spec = pltpu.PrefetchScalarGridSpec(
        num_scalar_prefetch=1,
        grid=(E, B),
        in_specs=[pl.BlockSpec(memory_space=pl.ANY),
                  pl.BlockSpec((None, D, Fd), lambda e, b, ix: (e, 0, 0)),
                  pl.BlockSpec((None, D, Fd), lambda e, b, ix: (e, 0, 0)),
                  pl.BlockSpec((None, Fd, D), lambda e, b, ix: (e, 0, 0))],
        out_specs=pl.BlockSpec((None, None, C, D), lambda e, b, ix: (b, e, 0, 0)),
        scratch_shapes=[pltpu.VMEM((D, Fd), BF16), pltpu.VMEM((D, Fd), BF16), pltpu.VMEM((Fd, D), BF16),
                        pltpu.VMEM((C, D + LANES), F32), pltpu.SemaphoreType.DMA(())],
    )
    return pl.pallas_call(
        functools.partial(_ffn_kernel, T=T),
        grid_spec=grid_spec,
        out_shape=jax.ShapeDtypeStruct((B, E, C, D), BF16),
        compiler_params=_params("arbitrary", "arbitrary"),
        name="expert_ffn",
    )(idx, hext, w1, w3, w2)


def _combine_kernel(starts_ref, x_ref, g2_ref, fg_ref, idx_ref, *rest, wb, nwin, final, spt):
    ys = rest[:-2]
    o_ref, acc_ref = rest[-2:]
    b = pl.program_id(0)
    t = pl.program_id(1)
    tt = x_ref.shape[0]
    E = idx_ref.shape[0]
    C = idx_ref.shape[1]
    tok = t * tt + lax.broadcasted_iota(jnp.int32, (tt, 1), 0)

    def window_dot(e, k, wk):
        if wb == C:
            ids = idx_ref[e:e + 1, :]
        else:
            ids = idx_ref[e:e + 1, pl.ds(pl.multiple_of(wk * wb, wb), wb)]
        onehot = jnp.where(ids == tok, 1.0, 0.0).astype(BF16)
        return jnp.dot(onehot, ys[e * nwin + k][...], preferred_element_type=F32)

    acc = None
    for e in range(E):
        w0 = jnp.minimum(starts_ref[b, t * spt, e] // wb, C // wb - 1)
        d = window_dot(e, 0, w0)
        acc = d if acc is None else acc + d
    acc_ref[...] = acc
    if nwin == 2:
        for e in range(E):
            s0 = starts_ref[b, t * spt, e]
            s1 = starts_ref[b, (t + 1) * spt, e]
            w0 = s0 // wb

            @pl.when((s1 > s0) & ((s1 - 1) // wb > w0))
            def _(e=e, w0=w0):
                acc_ref[...] += window_dot(e, 1, w0 + 1)

    x = x_ref[...] + g2_ref[...] * acc_ref[...]
    if final:
        x = x * lax.rsqrt(jnp.mean(x * x, axis=-1, keepdims=True) + EPS) * fg_ref[...]
    o_ref[...] = x


def _combine(x3, y, idx, starts, g2, final_g, *, final):
    B, T, D = x3.shape
    E, C = idx.shape[1:]
    if C >= LANES:
        wb, nwin, tt = LANES, 2, LANES
    else:
        wb, nwin, tt = C, 1, T
    nt = T // tt
    spt = tt // ROUTE_BLK
    G = g2.shape[0]

    def ymap(e, k):
        return lambda b, t, st: (b, e, jnp.minimum(st[b, t * spt, e] // wb + k, C // wb - 1), 0)

    y_specs = [pl.BlockSpec((None, None, wb, D), ymap(e, k)) for e in range(E) for k in range(nwin)]
    grid_spec = pltpu.PrefetchScalarGridSpec(
        num_scalar_prefetch=1,
        grid=(B, nt),
        in_specs=[pl.BlockSpec((None, tt, D), lambda b, t, st: (b, t, 0)),
                  pl.BlockSpec((None, 1, D), lambda b, t, st: (b * G // B, 0, 0)),
                  pl.BlockSpec((1, D), lambda b, t, st: (0, 0)),
                  pl.BlockSpec((None, E, C), lambda b, t, st: (b, 0, 0))] + y_specs,
        out_specs=pl.BlockSpec((None, tt, D), lambda b, t, st: (b, t, 0)),
        scratch_shapes=[pltpu.VMEM((tt, D), F32)],
    )
    return pl.pallas_call(
        functools.partial(_combine_kernel, wb=wb, nwin=nwin, final=final, spt=spt),
        grid_spec=grid_spec,
        out_shape=jax.ShapeDtypeStruct((B, T, D), F32),
        compiler_params=_params("parallel", "arbitrary"),
        name="moe_combine",
    )(starts, x3, g2, final_g.reshape(1, D), idx, *([y] * (E * nwin)))


def _route_kernel(aff_ref, idx_ref, starts_ref, cs_ref, *, C, E):
    T = aff_ref.shape[0]
    nb = T // ROUTE_BLK

    def value_of(pattern):
        ex = jnp.right_shift(pattern, F32_MANT_BITS)
        frac = (pattern & ((1 << F32_MANT_BITS) - 1)).astype(F32) * (2.0 ** -F32_MANT_BITS)
        n = F32_EXP_BIAS - ex
        p = jnp.ones(pattern.shape, F32)
        for i in range(7):
            p = jnp.where((jnp.right_shift(n, i) & 1) == 1, p * (2.0 ** -(2 ** i)), p)
        return jnp.where(ex == 0, 0.0, p * (1.0 + frac))

    def bit_body(i, prefix):
        cand = prefix | jnp.left_shift(jnp.int32(1), 29 - i)
        cnt = jnp.sum(jnp.where(aff_ref[...] >= value_of(cand), 1.0, 0.0), axis=0, keepdims=True)
        return jnp.where(cnt >= C, cand, prefix)

    vstar = value_of(lax.fori_loop(0, 30, bit_body, jnp.zeros((1, LANES), jnp.int32)))
    n_gt = jnp.sum(jnp.where(aff_ref[...] > vstar, 1.0, 0.0), axis=0, keepdims=True)
    n_tie = C - n_gt

    r = lax.broadcasted_iota(jnp.int32, (ROUTE_BLK, ROUTE_BLK), 0)
    c = lax.broadcasted_iota(jnp.int32, (ROUTE_BLK, ROUTE_BLK), 1)
    lower_strict = jnp.where(c < r, 1.0, 0.0).astype(BF16)
    lower_incl = jnp.where(c <= r, 1.0, 0.0).astype(BF16)

    def blk(k, carry):
        ties_before, chosen_before = carry
        off = pl.multiple_of(k * ROUTE_BLK, ROUTE_BLK)
        a = aff_ref[pl.ds(off, ROUTE_BLK), :]
        gt = a > vstar
        eq = jnp.where(a == vstar, 1.0, 0.0)
        rank = jnp.dot(lower_strict, eq.astype(BF16), preferred_element_type=F32) + ties_before
        sel = jnp.where(gt | ((eq > 0.0) & (rank < n_tie)), 1.0, 0.0)
        csum = jnp.dot(lower_incl, sel.astype(BF16), preferred_element_type=F32) + chosen_before
        cs_ref[pl.ds(off, ROUTE_BLK), :] = csum
        starts_ref[k] = jnp.broadcast_to(chosen_before.astype(jnp.int32), (SUBLANES, LANES))
        return (ties_before + jnp.sum(eq, axis=0, keepdims=True),
                chosen_before + jnp.sum(sel, axis=0, keepdims=True))

    zero = jnp.zeros((1, LANES), F32)
    _, total = lax.fori_loop(0, nb, blk, (zero, zero))
    starts_ref[nb] = jnp.broadcast_to(total.astype(jnp.int32), (SUBLANES, LANES))

    lane_f = lax.broadcasted_iota(jnp.int32, (1, LANES), 1).astype(F32)

    def expert(e, carry):
        pick = jnp.where(r == e, 1.0, 0.0)
        col = jnp.dot(cs_ref[...], pick, preferred_element_type=F32,
                      precision=lax.Precision.HIGHEST)
        for jt in range(-(-C // LANES)):
            cnt = jnp.sum(jnp.where(col <= lane_f + jt * LANES, 1.0, 0.0), axis=0, keepdims=True)
            w = min(LANES, C - jt * LANES)
            idx_ref[e, :, jt * LANES:jt * LANES + w] = jnp.broadcast_to(
                cnt[:, 0:w].astype(jnp.int32), (SUBLANES, w))
        return carry

    lax.fori_loop(0, E, expert, 0)


def _route(hext3, E, C):
    B, T, W = hext3.shape
    nb = T // ROUTE_BLK
    idx8, starts8 = pl.pallas_call(
        functools.partial(_route_kernel, C=C, E=E),
        grid=(B,),
        in_specs=[pl.BlockSpec((None, T, LANES), lambda b: (b, 0, W // LANES - 1))],
        out_specs=[pl.BlockSpec((None, E, SUBLANES, C), lambda b: (b, 0, 0, 0)),
                   pl.BlockSpec((None, nb + 1, SUBLANES, LANES), lambda b: (b, 0, 0, 0))],
        out_shape=[jax.ShapeDtypeStruct((B, E, SUBLANES, C), jnp.int32),
                   jax.ShapeDtypeStruct((B, nb + 1, SUBLANES, LANES), jnp.int32)],
        scratch_shapes=[pltpu.VMEM((T, LANES), F32)],
        compiler_params=_params("parallel"),
        name="ec_route",
    )(hext3)
    return idx8[:, :, 0, :], starts8[:, :, 0, :E]


def _moe(x3, hext, w1, w3, w2, g2, final_g, *, final):
    B, T, D = x3.shape
    E = w1.shape[0]
    C = EC_CAPACITY * T // E
    idx, starts = _route(hext.reshape(B, T, D + LANES), E, C)
    y = _ffn(hext, idx, w1, w3, w2, T)
    return _combine(x3, y, idx, starts, g2, final_g, final=final)


_C_DK, _C_DV, _C_NK, _C_NV, _C_DQ, _C_NQ = 0, 4, 8, 12, 16, 20
ATT_COLS = 6 * BRANCH_W
KV_COLS = 4 * BRANCH_W
GATE_START = 9 * BRANCH_W
_R_POOL, _R_SU, _R_SV = 8, 9, 10


def _mixers(za, zr, zc_kv, T, p, tables, lam_init, *, latent):
    B = za.shape[0]
    sl = lambda a, c: a[:, :, c * LANES:c * LANES + BRANCH_W]
    if latent:
        k_all = jnp.concatenate([sl(zc_kv, _C_DK), sl(za, _C_DK)], axis=1)
        v_all = jnp.concatenate([sl(zc_kv, _C_DV), sl(za, _C_DV)], axis=1)
        diff = _flash(sl(za, _C_DQ), k_all, v_all, p['lam'], p['subln'], mode="diff", lam_init=lam_init)
        na = _na_attn(za, zc_kv, tables, qcol=_C_NQ, kcol=_C_NK, vcol=_C_NV)
    else:
        diff = _flash(sl(za, _C_DQ), sl(za, _C_DK), sl(za, _C_DV), p['lam'], p['subln'],
                      mode="diff", lam_init=lam_init)
        na = _flash(sl(za, _C_NQ), sl(za, _C_NK), sl(za, _C_NV), p['lam'], p['subln'], mode="dense")
    pool = _pool(zr, p['pool_w'], p['pool_scale'], col=_R_POOL)
    sgu = _sgu(zr, p['sgu_g'], p['sgu_w'], p['sgu_b'], ucol=_R_SU, vcol=_R_SV)
    M = B * T
    f2 = lambda a: a.reshape(M, a.shape[-1])
    return [f2(pool), f2(diff), f2(sgu), f2(na)]


def kernel(x, c, ctx, c_ctx, w_mod, b_mod, norm1_g, norm2_g, w_in, pool_w, pool_scale, diff_lambda,
           diff_subln_g, sgu_norm_g, sgu_w, sgu_b, na_rpb, w_br, w_out, router_w, moe_w1, moe_w3,
           moe_w2, final_g):
    B, T, D = x.shape
    Lc = ctx.shape[1]
    depth = w_in.shape[0]
    rows = T // GRID_W
    rope = _rope_tables(T)
    cvec = jnp.concatenate([c, c_ctx[None], jnp.zeros((8 - B - 1, D), F32)], axis=0)
    x_lat = x.reshape(B * T, D)
    x_ctx = ctx.reshape(B * Lc, D)
    for l in range(depth):
        last = l == depth - 1
        lam_init = 0.8 - 0.6 * math.exp(-0.3 * l)
        mod = _mod_vectors(cvec, w_mod[l], b_mod[l])
        mx = mod[:B].reshape(B, 6, 1, D)
        mc = mod[B:B + 1].reshape(1, 6, 1, D)
        sh1, sc1, g1, sh2, sc2, g2 = [mx[:, i] for i in range(6)]
        csh1, csc1, cg1, csh2, csc2, cg2 = [mc[:, i] for i in range(6)]
        w_in_b = w_in[l].astype(BF16)
        w_att = w_in_b[:, :ATT_COLS]
        w_rest = jnp.concatenate([w_in_b[:, GATE_START:], w_in_b[:, ATT_COLS:GATE_START]], axis=1)
        p = {'lam': diff_lambda[l], 'subln': diff_subln_g[l], 'pool_w': pool_w[l],
             'pool_scale': pool_scale[l], 'sgu_g': sgu_norm_g[l], 'sgu_w': sgu_w[l], 'sgu_b': sgu_b[l]}
        w_br_b = w_br[l].astype(BF16)
        w_out_b = w_out[l].astype(BF16)
        tables = _na_tables(na_rpb[l], rows)

        if last:
            zc_a = _norm_proj(x_ctx, norm1_g[l], csh1, csc1, w_att[:, :KV_COLS], BF16)
        else:
            zc_a = _norm_proj(x_ctx, norm1_g[l], csh1, csc1, w_att, BF16)
        zc_a = zc_a.reshape(B, Lc, -1)

        za = _norm_proj(x_lat, norm1_g[l], sh1, sc1, w_att, BF16, rope=rope,
                        rope_tiles=(_C_DK // 4, _C_DQ // 4)).reshape(B, T, ATT_COLS)
        zr = _norm_proj(x_lat, norm1_g[l], sh1, sc1, w_rest, F32).reshape(B, T, -1)
        br = _mixers(za, zr, zc_a, T, p, tables, lam_init, latent=True)
        x_lat, h2 = _merge(br, zr.reshape(B * T, -1), w_br_b, w_out_b, x_lat, g1,
                                   norm2_g[l], sh2, sc2, router_w[l])
        x_lat = _moe(x_lat.reshape(B, T, D), h2, moe_w1[l], moe_w3[l], moe_w2[l], g2,
                     final_g, final=last).reshape(B * T, D)

        if not last:
            zc_r = _norm_proj(x_ctx, norm1_g[l], csh1, csc1, w_rest, F32).reshape(B, Lc, -1)
            brc = _mixers(zc_a, zc_r, None, Lc, p, None, lam_init, latent=False)
            x_ctx, hc2 = _merge(brc, zc_r.reshape(B * Lc, -1), w_br_b, w_out_b, x_ctx, cg1,
                                     norm2_g[l], csh2, csc2, router_w[l])
            x_ctx = _moe(x_ctx.reshape(B, Lc, D), hc2, moe_w1[l], moe_w3[l], moe_w2[l], cg2,
                         final_g, final=False).reshape(B * Lc, D)
    return x_lat.reshape(B, T, D)
```

```python
import functools
import math

import numpy as np
import jax
import jax.numpy as jnp
from jax import lax
from jax.experimental import pallas as pl
from jax.experimental.pallas import tpu as pltpu

F32 = jnp.float32
BF16 = jnp.bfloat16

EPS = 1e-6
GRID_W = 64
LANES = 128
SUBLANES = 8
HALF = 64
ONES_ROWS = 16
N_BRANCH = 4
BRANCH_W = 512
POOL_WINDOWS = (2, 4, 8, 16)
POOL_HALO = 16
SGU_CHUNK = 128
NA_WIN_R = 8
NA_WIN_C = 16
NA_QROWS = 8
NA_KROWS = 16
N_EXPERTS = 16
EC_CAPACITY = 2
ROUTE_BLK = 128
FFN_ROWS = 256
F32_MANT_BITS = 23
F32_EXP_BIAS = 127
ROPE_BASE = 10000.0
NEG = -1e30
VMEM_LIMIT = 56 * 1024 * 1024

_NT = (((1,), (1,)), ((), ()))


def _params(*sem):
    return pltpu.CompilerParams(dimension_semantics=sem, vmem_limit_bytes=VMEM_LIMIT)


def _pick(n, cands):
    for c in cands:
        if n % c == 0:
            return c
    raise ValueError(f"no tile for {n}")


def _mod_kernel(c_ref, w_ref, b_ref, o_ref):
    c = c_ref[...]
    s = c * jax.nn.sigmoid(c)
    o_ref[...] = jnp.dot(s, w_ref[...], preferred_element_type=F32,
                         precision=lax.Precision.HIGHEST) + b_ref[...]


def _mod_vectors(cvec, w, b, layer):
    L, D, N = w.shape
    tn = _pick(N, (1536, 1024, 512, 128))
    return pl.pallas_call(
        _mod_kernel,
        grid=(N // tn,),
        in_specs=[pl.BlockSpec((8, D), lambda j: (0, 0)),
                  pl.BlockSpec((None, D, tn), lambda j: (layer, 0, j)),
                  pl.BlockSpec((None, 1, tn), lambda j: (layer, 0, j))],
        out_specs=pl.BlockSpec((8, tn), lambda j: (0, j)),
        out_shape=jax.ShapeDtypeStruct((8, N), F32),
        compiler_params=_params("parallel"),
        name="mod_vectors",
    )(cvec, w, b.reshape(L, 1, N))


def _rope_store(acc, cos, sin, o_ref):
    lane = lax.broadcasted_iota(jnp.int32, (1, LANES), 1)
    first = (lane % 32) < 16
    for h in range(acc.shape[1] // LANES):
        a = acc[:, h * LANES:(h + 1) * LANES]
        partner = jnp.where(first, pltpu.roll(a, LANES - 16, 1), pltpu.roll(a, 16, 1))
        o_ref[:, h * LANES:(h + 1) * LANES] = (a * cos + partner * sin).astype(o_ref.dtype)


def _norm_proj_kernel(x_ref, g_ref, sh_ref, sc_ref, w_ref, *rest, rope_tiles):
    if rope_tiles:
        cos_ref, sin_ref, o_ref, h_ref = rest
    else:
        o_ref, h_ref = rest
    j = pl.program_id(1)

    @pl.when(j == 0)
    def _():
        x = x_ref[...]
        y = x * lax.rsqrt(jnp.mean(x * x, axis=-1, keepdims=True) + EPS) * g_ref[...]
        h_ref[...] = (y * (1.0 + sc_ref[...]) + sh_ref[...]).astype(BF16)

    acc = jnp.dot(h_ref[...], w_ref[...], preferred_element_type=F32)
    if rope_tiles:
        is_rope = functools.reduce(jnp.logical_or, [j == t for t in rope_tiles])

        @pl.when(is_rope)
        def _():
            _rope_store(acc, cos_ref[...], sin_ref[...], o_ref)

        @pl.when(jnp.logical_not(is_rope))
        def _():
            o_ref[...] = acc.astype(o_ref.dtype)
    else:
        o_ref[...] = acc.astype(o_ref.dtype)


def _norm_proj(x2, g, shift, scale, w, out_dtype, rope=None, rope_tiles=()):
    M, D = x2.shape
    N = w.shape[1]
    G = shift.shape[0]
    rows_per_group = M // G
    tm = _pick(rows_per_group, (1024, 512, 256))
    tn = 512
    assert N % tn == 0
    in_specs = [pl.BlockSpec((tm, D), lambda i, j: (i, 0)),
                pl.BlockSpec((1, D), lambda i, j: (0, 0)),
                pl.BlockSpec((None, 1, D), lambda i, j: (i * tm // rows_per_group, 0, 0)),
                pl.BlockSpec((None, 1, D), lambda i, j: (i * tm // rows_per_group, 0, 0)),
                pl.BlockSpec((D, tn), lambda i, j: (0, j))]
    args = [x2, g.reshape(1, D), shift, scale, w]
    if rope_tiles:
        cos, sin = rope
        nt = cos.shape[0] // tm
        in_specs += [pl.BlockSpec((tm, LANES), lambda i, j: (i % nt, 0)),
                     pl.BlockSpec((tm, LANES), lambda i, j: (i % nt, 0))]
        args += [cos, sin]
    return pl.pallas_call(
        functools.partial(_norm_proj_kernel, rope_tiles=tuple(rope_tiles)),
        grid=(M // tm, N // tn),
        in_specs=in_specs,
        out_specs=pl.BlockSpec((tm, tn), lambda i, j: (i, j)),
        out_shape=jax.ShapeDtypeStruct((M, N), out_dtype),
        scratch_shapes=[pltpu.VMEM((tm, D), BF16)],
        compiler_params=_params("parallel", "arbitrary"),
        name="norm_proj",
    )(*args)


def _rope_tables(T):
    t = jnp.arange(T)
    row = (t // GRID_W).astype(F32)
    col = (t % GRID_W).astype(F32)
    nf = HALF // 4
    inv = ROPE_BASE ** (-jnp.arange(nf, dtype=F32) / nf)
    ar = row[:, None] * inv
    ac = col[:, None] * inv
    cos = jnp.concatenate([jnp.cos(ar), jnp.cos(ar), jnp.cos(ac), jnp.cos(ac)], axis=-1)
    sin = jnp.concatenate([-jnp.sin(ar), jnp.sin(ar), -jnp.sin(ac), jnp.sin(ac)], axis=-1)
    return jnp.tile(cos, (1, 2)), jnp.tile(sin, (1, 2))


def _flash_kernel(q_ref, k_ref, v_ref, lam_ref, g_ref, o_ref, m_ref, acc_ref, sa_ref, sb_ref, *,
                  mode, tk, lam_init):
    tq = q_ref.shape[0]
    nk = k_ref.shape[0] // tk
    ones = jnp.ones((ONES_ROWS, tk), BF16)
    sub = lax.broadcasted_iota(jnp.int32, (LANES, 1), 0)
    lo = sub < HALF
    if mode == "diff":
        lp = lam_ref[...]
        d1 = jnp.sum(lp[0:1, :] * lp[1:2, :], keepdims=True)
        d2 = jnp.sum(lp[2:3, :] * lp[3:4, :], keepdims=True)
        lam = jnp.exp(d1) - jnp.exp(d2) + lam_init
    for h in range(q_ref.shape[1] // LANES):
        cs = slice(h * LANES, (h + 1) * LANES)
        qh = q_ref[:, cs].astype(F32).T.astype(BF16)
        zero = jnp.zeros_like(qh)
        q2 = jnp.concatenate([jnp.where(lo, qh, zero), jnp.where(lo, zero, qh)], axis=1)
        q2 = q2 * jnp.asarray(HALF ** -0.5, BF16)
        m_ref[...] = jnp.full(m_ref.shape, -jnp.inf, F32)
        acc_ref[...] = jnp.zeros(acc_ref.shape, F32)

        def scores(c, s_ref, q2=q2, cs=cs):
            off = pl.multiple_of(c * tk, tk)
            s_ref[...] = jnp.dot(k_ref[pl.ds(off, tk), cs], q2, preferred_element_type=F32)

        def update(c, s_ref, cs=cs):
            off = pl.multiple_of(c * tk, tk)
            vc = jnp.concatenate([v_ref[cs, pl.ds(off, tk)], ones], axis=0)
            s = s_ref[...]
            m_old = m_ref[...]
            m_new = jnp.maximum(m_old, jnp.max(s, axis=0, keepdims=True))
            alpha = jnp.exp(m_old - m_new)
            p = jnp.exp(s - m_new).astype(BF16)
            acc_ref[...] = alpha * acc_ref[...] + jnp.dot(vc, p, preferred_element_type=F32)
            m_ref[...] = m_new

        scores(0, sa_ref)

        def body(j, carry):
            scores(2 * j + 1, sb_ref)
            update(2 * j, sa_ref)
            scores(2 * j + 2, sa_ref)
            update(2 * j + 1, sb_ref)
            return carry

        lax.fori_loop(0, (nk - 1) // 2, body, 0)
        update(nk - 1, sa_ref)
        a = acc_ref[0:LANES, :] / acc_ref[LANES:LANES + 1, :]
        a0 = a[:, :tq]
        a1 = a[:, tq:]
        if mode == "diff":
            o = a0 - lam * a1
            o = o * lax.rsqrt(jnp.mean(o * o, axis=0, keepdims=True) + EPS)
            o = o * g_ref[...] * (1.0 - lam_init)
        else:
            o = jnp.where(lo, a0, a1)
        o_ref[:, cs] = o.T.astype(o_ref.dtype)


def _flash(z, qblk, k, v, lam_p, g, *, mode, lam_init=0.0):
    B, Tq, _ = z.shape
    Tk, W = k.shape[1:]
    tq = _pick(Tq, (512, 256, 128))
    tk = next(c for c in (768, 512, 384, 256, 128) if Tk % c == 0 and (Tk // c) % 2 == 1)
    return pl.pallas_call(
        functools.partial(_flash_kernel, mode=mode, tk=tk, lam_init=lam_init),
        grid=(B, Tq // tq),
        in_specs=[pl.BlockSpec((None, tq, W), lambda b, i: (b, i, qblk)),
                  pl.BlockSpec((None, Tk, W), lambda b, i: (b, 0, 0)),
                  pl.BlockSpec((None, W, Tk), lambda b, i: (b, 0, 0)),
                  pl.BlockSpec(lam_p.shape, lambda b, i: (0, 0)),
                  pl.BlockSpec((LANES, 1), lambda b, i: (0, 0))],
        out_specs=pl.BlockSpec((None, tq, W), lambda b, i: (b, i, 0)),
        out_shape=jax.ShapeDtypeStruct((B, Tq, W), BF16),
        scratch_shapes=[pltpu.VMEM((1, 2 * tq), F32), pltpu.VMEM((LANES + ONES_ROWS, 2 * tq), F32),
                        pltpu.VMEM((tk, 2 * tq), F32), pltpu.VMEM((tk, 2 * tq), F32)],
        compiler_params=_params("parallel", "parallel"),
        name="flash_" + mode,
    )(z, k, jnp.swapaxes(v, 1, 2), lam_p, g.reshape(LANES, 1))


def _na_kernel(q_ref, k_ref, v_ref, kc_ref, vc_ref, b_ref, o_ref, *, rows):
    rb = pl.program_id(2)
    nq = NA_QROWS * GRID_W
    nkk = NA_KROWS * GRID_W
    k0 = jnp.clip(rb * NA_QROWS - NA_WIN_R // 2, 0, rows - NA_KROWS) * GRID_W
    k0 = pl.multiple_of(k0, 4 * GRID_W)
    kl = k_ref[pl.ds(k0, nkk), :]
    vl = v_ref[pl.ds(k0, nkk), :]
    kc = kc_ref[...]
    vc = vc_ref[...]
    q = q_ref[...]
    zero = jnp.zeros_like(q)
    lane = lax.broadcasted_iota(jnp.int32, (1, LANES), 1)
    lo = lane < HALF
    scale = jnp.asarray(HALF ** -0.5, BF16)
    outs = []
    for hh in range(2):
        qm = (jnp.where(lo, q, zero) if hh == 0 else jnp.where(lo, zero, q)) * scale
        s_loc = lax.dot_general(qm, kl, _NT, preferred_element_type=F32) + b_ref[hh]
        s_ctx = lax.dot_general(qm, kc, _NT, preferred_element_type=F32)
        m = jnp.maximum(jnp.max(s_loc, axis=-1, keepdims=True), jnp.max(s_ctx, axis=-1, keepdims=True))
        p_loc = jnp.exp(s_loc - m)
        p_ctx = jnp.exp(s_ctx - m)
        l = jnp.sum(p_loc, axis=-1, keepdims=True) + jnp.sum(p_ctx, axis=-1, keepdims=True)
        o = jnp.dot(p_loc.astype(BF16), vl, preferred_element_type=F32) \
            + jnp.dot(p_ctx.astype(BF16), vc, preferred_element_type=F32)
        outs.append(o / l)
    del nq
    o_ref[...] = jnp.where(lo, outs[0], outs[1]).astype(o_ref.dtype)


def _na_tables(rpb, rows):
    H = rpb.shape[0]
    c = np.arange(GRID_W)
    cstart = np.clip(c - NA_WIN_C // 2, 0, GRID_W - NA_WIN_C)
    kcol = np.arange(GRID_W)
    col_ok = (kcol[None, :] >= cstart[:, None]) & (kcol[None, :] < cstart[:, None] + NA_WIN_C)
    dc = np.clip(kcol[None, :] - c[:, None] + NA_WIN_C - 1, 0, 2 * NA_WIN_C - 2)
    cm = (np.arange(2 * NA_WIN_C - 1)[None, None, :] == dc[:, :, None]) & col_ok[:, :, None]
    tabs = []
    for q0, k0, clamp in ((0, 0, True), (NA_QROWS, NA_QROWS - NA_WIN_R // 2, False),
                          (rows - NA_QROWS, rows - NA_KROWS, True)):
        r = q0 + np.arange(NA_QROWS)
        kr = k0 + np.arange(NA_KROWS)
        r0 = r - NA_WIN_R // 2
        if clamp:
            r0 = np.clip(r0, 0, rows - NA_WIN_R)
        row_ok = (kr[None, :] >= r0[:, None]) & (kr[None, :] < r0[:, None] + NA_WIN_R)
        dr = np.clip(kr[None, :] - r[:, None] + NA_WIN_R - 1, 0, 2 * NA_WIN_R - 2)
        rm = (np.arange(2 * NA_WIN_R - 1)[None, None, :] == dr[:, :, None]) & row_ok[:, :, None]
        t = jnp.einsum('rki,hij,cqj->hrckq', jnp.asarray(rm, F32), rpb, jnp.asarray(cm, F32),
                       precision=lax.Precision.HIGHEST)
        ok = row_ok[:, None, :, None] & col_ok[None, :, None, :]
        t = jnp.where(jnp.asarray(ok)[None], t, NEG)
        tabs.append(t.reshape(H, NA_QROWS * GRID_W, NA_KROWS * GRID_W))
    return jnp.stack(tabs)


def _na_attn(z, zc, tables, *, qcol, kcol, vcol):
    B, T, _ = z.shape
    Lc = zc.shape[1]
    rows = T // GRID_W
    nrb = rows // NA_QROWS
    nq = NA_QROWS * GRID_W
    nkk = NA_KROWS * GRID_W
    nhp = BRANCH_W // LANES

    def var(rb):
        return jnp.where(rb == 0, 0, jnp.where(rb == nrb - 1, 2, 1))

    return pl.pallas_call(
        functools.partial(_na_kernel, rows=rows),
        grid=(nhp, B, nrb),
        in_specs=[pl.BlockSpec((None, nq, LANES), lambda h, b, r: (b, r, qcol + h)),
                  pl.BlockSpec((None, T, LANES), lambda h, b, r: (b, 0, kcol + h)),
                  pl.BlockSpec((None, T, LANES), lambda h, b, r: (b, 0, vcol + h)),
                  pl.BlockSpec((None, Lc, LANES), lambda h, b, r: (b, 0, kcol + h)),
                  pl.BlockSpec((None, Lc, LANES), lambda h, b, r: (b, 0, vcol + h)),
                  pl.BlockSpec((None, 2, nq, nkk), lambda h, b, r: (var(r), h, 0, 0))],
        out_specs=pl.BlockSpec((None, nq, LANES), lambda h, b, r: (b, r, h)),
        out_shape=jax.ShapeDtypeStruct((B, T, BRANCH_W), BF16),
        compiler_params=_params("parallel", "parallel", "parallel"),
        name="na_attn",
    )(z, z, z, zc, zc, tables)


def _pool_kernel(prev_ref, cur_ref, next_ref, w_ref, sc_ref, o_ref, *, T):
    i = pl.program_id(1)
    tt = cur_ref.shape[0]
    E = tt + 2 * POOL_HALO
    ext = jnp.concatenate([prev_ref[...], cur_ref[...], next_ref[...]], axis=0).astype(F32)
    gi = i * tt - POOL_HALO + lax.broadcasted_iota(jnp.int32, (E, 1), 0)
    ext = jnp.where((gi >= 0) & (gi < T), ext, 0.0)
    t = i * tt + lax.broadcasted_iota(jnp.int32, (tt, 1), 0)
    for g, w in enumerate(POOL_WINDOWS):
        cs = slice(g * LANES, (g + 1) * LANES)
        zg = ext[:, cs]
        s = zg + pltpu.roll(zg, 1, 0)
        half = 1
        while 2 * half < w:
            s = pltpu.roll(s, half, 0) + pltpu.roll(s, E - half, 0)
            half *= 2
        cnt = (jnp.minimum(t + w // 2, T) - jnp.maximum(t - w // 2, 0)).astype(F32)
        c0 = slice(POOL_HALO, POOL_HALO + tt)
        d = s[c0] / cnt - zg[c0]
        y = jnp.dot(d.astype(BF16), w_ref[g].astype(BF16), preferred_element_type=F32)
        o_ref[:, cs] = (y * sc_ref[:, cs]).astype(o_ref.dtype)


def _pool(z, w_pool, scale, *, col):
    B, T, _ = z.shape
    tt = _pick(T, (512, 256))
    hb = tt // POOL_HALO
    nh = T // POOL_HALO
    return pl.pallas_call(
        functools.partial(_pool_kernel, T=T),
        grid=(B, T // tt),
        in_specs=[pl.BlockSpec((None, POOL_HALO, BRANCH_W), lambda b, i: (b, jnp.maximum(i * hb - 1, 0), col)),
                  pl.BlockSpec((None, tt, BRANCH_W), lambda b, i: (b, i, col)),
                  pl.BlockSpec((None, POOL_HALO, BRANCH_W), lambda b, i: (b, jnp.minimum((i + 1) * hb, nh - 1), col)),
                  pl.BlockSpec(w_pool.shape, lambda b, i: (0, 0, 0)),
                  pl.BlockSpec((1, BRANCH_W), lambda b, i: (0, 0))],
        out_specs=pl.BlockSpec((None, tt, BRANCH_W), lambda b, i: (b, i, 0)),
        out_shape=jax.ShapeDtypeStruct((B, T, BRANCH_W), BF16),
        compiler_params=_params("parallel", "parallel"),
        name="pool_mixer",
    )(z, z, z, w_pool, scale.reshape(1, BRANCH_W))


def _sgu_kernel(u_ref, v_ref, g_ref, w_ref, bt_ref, o_ref):
    ts = u_ref.shape[0]
    u = jax.nn.gelu(u_ref[...].astype(F32))
    v = jax.nn.gelu(v_ref[...].astype(F32))
    vc = v - jnp.mean(v, axis=-1, keepdims=True)
    v = vc * lax.rsqrt(jnp.mean(vc * vc, axis=-1, keepdims=True) + EPS) * g_ref[...]
    vb = v.astype(BF16)
    for g in range(BRANCH_W // LANES):
        cs = slice(g * LANES, (g + 1) * LANES)
        wg = w_ref[g].astype(BF16)
        bg = bt_ref[:, g:g + 1]
        for ck in range(ts // SGU_CHUNK):
            rs = slice(ck * SGU_CHUNK, (ck + 1) * SGU_CHUNK)
            f = jnp.dot(wg, vb[rs, cs], preferred_element_type=F32) + bg
            o_ref[rs, cs] = (u[rs, cs] * f).astype(o_ref.dtype)


def _sgu(z, g, w_s, b, *, ucol, vcol):
    B, T, _ = z.shape
    ts = _pick(T, (512, 256, 128))
    return pl.pallas_call(
        _sgu_kernel,
        grid=(B, T // ts),
        in_specs=[pl.BlockSpec((None, ts, BRANCH_W), lambda b_, i: (b_, i, ucol)),
                  pl.BlockSpec((None, ts, BRANCH_W), lambda b_, i: (b_, i, vcol)),
                  pl.BlockSpec((1, BRANCH_W), lambda b_, i: (0, 0)),
                  pl.BlockSpec(w_s.shape, lambda b_, i: (0, 0, 0)),
                  pl.BlockSpec((SGU_CHUNK, BRANCH_W // LANES), lambda b_, i: (0, 0))],
        out_specs=pl.BlockSpec((None, ts, BRANCH_W), lambda b_, i: (b_, i, 0)),
        out_shape=jax.ShapeDtypeStruct((B, T, BRANCH_W), BF16),
        compiler_params=_params("parallel", "parallel"),
        name="sgu_mixer",
    )(z, z, g.reshape(1, BRANCH_W), w_s, b.T)


def _merge_kernel(p_ref, d_ref, s_ref, n_ref, gz_ref, wbr_ref, wout_ref, x_ref, g1_ref,
                  n2_ref, sh_ref, sc_ref, rw_ref, xo_ref, h_ref, *, n_experts):
    D = x_ref.shape[1]
    acc = None
    for i, br in enumerate((p_ref, d_ref, s_ref, n_ref)):
        proj = jnp.dot(br[...], wbr_ref[i], preferred_element_type=F32)
        t = jax.nn.sigmoid(gz_ref[:, i * D:(i + 1) * D].astype(F32)) * proj
        acc = t if acc is None else acc + t
    mix = jnp.dot(acc.astype(BF16), wout_ref[...], preferred_element_type=F32)
    x = x_ref[...] + g1_ref[...] * mix
    xo_ref[...] = x
    y = x * lax.rsqrt(jnp.mean(x * x, axis=-1, keepdims=True) + EPS) * n2_ref[...]
    h = y * (1.0 + sc_ref[...]) + sh_ref[...]
    h_ref[:, 0:D] = h
    lg = jnp.dot(h, rw_ref[...], preferred_element_type=F32, precision=lax.Precision.HIGHEST)
    lane = lax.broadcasted_iota(jnp.int32, (1, LANES), 1)
    lg = jnp.where(lane < n_experts, lg, -jnp.inf)
    ex = jnp.exp(lg - jnp.max(lg, axis=-1, keepdims=True))
    h_ref[:, D:D + LANES] = ex / jnp.sum(ex, axis=-1, keepdims=True)


def _merge(branches, zr, w_br, w_out, x2, g1, n2g, sh2, sc2, router_w):
    M, D = x2.shape
    G = g1.shape[0]
    rpg = M // G
    tm = _pick(rpg, (256,))
    E = router_w.shape[1]
    router_w = jnp.pad(router_w, ((0, 0), (0, LANES - E)))
    gb = 0
    grp = lambda i: (i * tm // rpg, 0, 0)
    row = lambda i: (i, 0)
    return pl.pallas_call(
        functools.partial(_merge_kernel, n_experts=E),
        grid=(M // tm,),
        in_specs=[pl.BlockSpec((tm, BRANCH_W), row)] * 4 + [
            pl.BlockSpec((tm, N_BRANCH * D), lambda i: (i, gb)),
            pl.BlockSpec(w_br.shape, lambda i: (0, 0, 0)),
            pl.BlockSpec(w_out.shape, lambda i: (0, 0)),
            pl.BlockSpec((tm, D), row),
            pl.BlockSpec((None, 1, D), grp),
            pl.BlockSpec((1, D), lambda i: (0, 0)),
            pl.BlockSpec((None, 1, D), grp),
            pl.BlockSpec((None, 1, D), grp),
            pl.BlockSpec(router_w.shape, lambda i: (0, 0))],
        out_specs=[pl.BlockSpec((tm, D), row), pl.BlockSpec((tm, D + LANES), row)],
        out_shape=[jax.ShapeDtypeStruct((M, D), F32), jax.ShapeDtypeStruct((M, D + LANES), F32)],
        compiler_params=_params("parallel"),
        name="merge_out",
    )(*branches, zr, w_br, w_out, x2, g1, n2g.reshape(1, D), sh2, sc2, router_w)


def _ffn_kernel(idx_ref, h_hbm, w1_ref, w3_ref, w2_ref, y_ref, w1b, w3b, w2b, xbuf, sem, *, T, B):
    e = pl.program_id(0)
    b = pl.program_id(1)
    C = xbuf.shape[1]
    D = w1_ref.shape[0]
    step = e * B + b
    nsteps = pl.num_programs(0) * B
    slot = step % 2

    def gather(e_, b_, slot_):
        def issue(j, carry):
            row = idx_ref[b_, e_, j] + b_ * T
            pltpu.make_async_copy(h_hbm.at[pl.ds(row, 1)], xbuf.at[slot_, pl.ds(j, 1)],
                                  sem.at[slot_]).start()
            return carry

        lax.fori_loop(0, C, issue, 0, unroll=8)

    @pl.when(step == 0)
    def _():
        gather(e, b, slot)

    @pl.when(step + 1 < nsteps)
    def _():
        gather(e + (b + 1) // B, (b + 1) % B, 1 - slot)

    @pl.when(b == 0)
    def _():
        w1b[...] = w1_ref[...].astype(BF16)
        w3b[...] = w3_ref[...].astype(BF16)
        w2b[...] = w2_ref[...].astype(BF16)

    pltpu.make_async_copy(h_hbm.at[pl.ds(0, C)], xbuf.at[slot], sem.at[slot]).wait()
    lane = lax.broadcasted_iota(jnp.int32, (1, LANES), 1)
    rc = min(C, FFN_ROWS)
    for r in range(C // rc):
        rs = pl.ds(r * rc, rc)
        xs = xbuf[slot, rs, 0:D].astype(BF16)
        gate = jnp.sum(jnp.where(lane == e, xbuf[slot, rs, D:D + LANES], 0.0), axis=-1, keepdims=True)
        a = jnp.dot(xs, w1b[...], preferred_element_type=F32)
        g = jnp.dot(xs, w3b[...], preferred_element_type=F32)
        hid = (a * jax.nn.sigmoid(a) * g).astype(BF16)
        y = jnp.dot(hid, w2b[...], preferred_element_type=F32) * gate
        y_ref[rs, :] = y.astype(y_ref.dtype)


def _ffn(hext, idx, w1, w3, w2, layer, T):
    B, E, C = idx.shape
    D, Fd = w1.shape[2:]
    grid_spec = pltpu.PrefetchScalarGridSpec(
        num_scalar_prefetch=1,
        grid=(E, B),
        in_specs=[pl.BlockSpec(memory_space=pl.ANY),
                  pl.BlockSpec((None, None, D, Fd), lambda e, b, ix: (layer, e, 0, 0)),
                  pl.BlockSpec((None, None, D, Fd), lambda e, b, ix: (layer, e, 0, 0)),
                  pl.BlockSpec((None, None, Fd, D), lambda e, b, ix: (layer, e, 0, 0))],
        out_specs=pl.BlockSpec((None, None, C, D), lambda e, b, ix: (b, e, 0, 0)),
        scratch_shapes=[pltpu.VMEM((D, Fd), BF16), pltpu.VMEM((D, Fd), BF16), pltpu.VMEM((Fd, D), BF16),
                        pltpu.VMEM((2, C, D + LANES), F32), pltpu.SemaphoreType.DMA((2,))],
    )
    return pl.pallas_call(
        functools.partial(_ffn_kernel, T=T, B=B),
        grid_spec=grid_spec,
        out_shape=jax.ShapeDtypeStruct((B, E, C, D), BF16),
        compiler_params=_params("arbitrary", "arbitrary"),
        name="expert_ffn",
    )(idx, hext, w1, w3, w2)


def _combine_kernel(starts_ref, x_ref, g2_ref, fg_ref, idx_ref, *rest, wb, nwin, final, spt):
    ys = rest[:-2]
    o_ref, acc_ref = rest[-2:]
    b = pl.program_id(0)
    t = pl.program_id(1)
    tt = x_ref.shape[0]
    E = idx_ref.shape[0]
    C = idx_ref.shape[1]
    tok = t * tt + lax.broadcasted_iota(jnp.int32, (tt, 1), 0)

    def window_dot(e, k, wk):
        if wb == C:
            ids = idx_ref[e:e + 1, :]
        else:
            ids = idx_ref[e:e + 1, pl.ds(pl.multiple_of(wk * wb, wb), wb)]
        onehot = jnp.where(ids == tok, 1.0, 0.0).astype(BF16)
        return jnp.dot(onehot, ys[e * nwin + k][...], preferred_element_type=F32)

    acc = None
    for e in range(E):
        w0 = jnp.minimum(starts_ref[b, t * spt, e] // wb, C // wb - 1)
        d = window_dot(e, 0, w0)
        acc = d if acc is None else acc + d
    acc_ref[...] = acc
    if nwin == 2:
        for e in range(E):
            s0 = starts_ref[b, t * spt, e]
            s1 = starts_ref[b, (t + 1) * spt, e]
            w0 = s0 // wb

            @pl.when((s1 > s0) & ((s1 - 1) // wb > w0))
            def _(e=e, w0=w0):
                acc_ref[...] += window_dot(e, 1, w0 + 1)

    x = x_ref[...] + g2_ref[...] * acc_ref[...]
    if final:
        x = x * lax.rsqrt(jnp.mean(x * x, axis=-1, keepdims=True) + EPS) * fg_ref[...]
    o_ref[...] = x


def _combine(x3, y, idx, starts, g2, final_g, *, final):
    B, T, D = x3.shape
    E, C = idx.shape[1:]
    if C >= LANES:
        wb, nwin, tt = LANES, 2, LANES
    else:
        wb, nwin, tt = C, 1, T
    nt = T // tt
    spt = tt // ROUTE_BLK
    G = g2.shape[0]

    def ymap(e, k):
        return lambda b, t, st: (b, e, jnp.minimum(st[b, t * spt, e] // wb + k, C // wb - 1), 0)

    y_specs = [pl.BlockSpec((None, None, wb, D), ymap(e, k)) for e in range(E) for k in range(nwin)]
    grid_spec = pltpu.PrefetchScalarGridSpec(
        num_scalar_prefetch=1,
        grid=(B, nt),
        in_specs=[pl.BlockSpec((None, tt, D), lambda b, t, st: (b, t, 0)),
                  pl.BlockSpec((None, 1, D), lambda b, t, st: (b * G // B, 0, 0)),
                  pl.BlockSpec((1, D), lambda b, t, st: (0, 0)),
                  pl.BlockSpec((None, E, C), lambda b, t, st: (b, 0, 0))] + y_specs,
        out_specs=pl.BlockSpec((None, tt, D), lambda b, t, st: (b, t, 0)),
        scratch_shapes=[pltpu.VMEM((tt, D), F32)],
    )
    return pl.pallas_call(
        functools.partial(_combine_kernel, wb=wb, nwin=nwin, final=final, spt=spt),
        grid_spec=grid_spec,
        out_shape=jax.ShapeDtypeStruct((B, T, D), F32),
        compiler_params=_params("parallel", "arbitrary"),
        name="moe_combine",
    )(starts, x3, g2, final_g.reshape(1, D), idx, *([y] * (E * nwin)))


def _route_kernel(aff_ref, idx_ref, starts_ref, cs_ref, *, C, E):
    T = aff_ref.shape[0]
    nb = T // ROUTE_BLK

    def value_of(pattern):
        ex = jnp.right_shift(pattern, F32_MANT_BITS)
        frac = (pattern & ((1 << F32_MANT_BITS) - 1)).astype(F32) * (2.0 ** -F32_MANT_BITS)
        n = F32_EXP_BIAS - ex
        p = jnp.ones(pattern.shape, F32)
        for i in range(7):
            p = jnp.where((jnp.right_shift(n, i) & 1) == 1, p * (2.0 ** -(2 ** i)), p)
        return jnp.where(ex == 0, 0.0, p * (1.0 + frac))

    def bit_body(i, prefix):
        cand = prefix | jnp.left_shift(jnp.int32(1), 29 - i)
        cnt = jnp.sum(jnp.where(aff_ref[...] >= value_of(cand), 1.0, 0.0), axis=0, keepdims=True)
        return jnp.where(cnt >= C, cand, prefix)

    vstar = value_of(lax.fori_loop(0, 30, bit_body, jnp.zeros((1, LANES), jnp.int32)))
    n_gt = jnp.sum(jnp.where(aff_ref[...] > vstar, 1.0, 0.0), axis=0, keepdims=True)
    n_tie = C - n_gt

    r = lax.broadcasted_iota(jnp.int32, (ROUTE_BLK, ROUTE_BLK), 0)
    c = lax.broadcasted_iota(jnp.int32, (ROUTE_BLK, ROUTE_BLK), 1)
    lower_strict = jnp.where(c < r, 1.0, 0.0).astype(BF16)
    lower_incl = jnp.where(c <= r, 1.0, 0.0).astype(BF16)

    def blk(k, carry):
        ties_before, chosen_before = carry
        off = pl.multiple_of(k * ROUTE_BLK, ROUTE_BLK)
        a = aff_ref[pl.ds(off, ROUTE_BLK), :]
        gt = a > vstar
        eq = jnp.where(a == vstar, 1.0, 0.0)
        rank = jnp.dot(lower_strict, eq.astype(BF16), preferred_element_type=F32) + ties_before
        sel = jnp.where(gt | ((eq > 0.0) & (rank < n_tie)), 1.0, 0.0)
        csum = jnp.dot(lower_incl, sel.astype(BF16), preferred_element_type=F32) + chosen_before
        cs_ref[pl.ds(off, ROUTE_BLK), :] = csum
        starts_ref[k] = jnp.broadcast_to(chosen_before.astype(jnp.int32), (SUBLANES, LANES))
        return (ties_before + jnp.sum(eq, axis=0, keepdims=True),
                chosen_before + jnp.sum(sel, axis=0, keepdims=True))

    zero = jnp.zeros((1, LANES), F32)
    _, total = lax.fori_loop(0, nb, blk, (zero, zero))
    starts_ref[nb] = jnp.broadcast_to(total.astype(jnp.int32), (SUBLANES, LANES))

    lane_f = lax.broadcasted_iota(jnp.int32, (1, LANES), 1).astype(F32)

    def expert(e, carry):
        pick = jnp.where(r == e, 1.0, 0.0)
        col = jnp.dot(cs_ref[...], pick, preferred_element_type=F32,
                      precision=lax.Precision.HIGHEST)
        for jt in range(-(-C // LANES)):
            cnt = jnp.sum(jnp.where(col <= lane_f + jt * LANES, 1.0, 0.0), axis=0, keepdims=True)
            w = min(LANES, C - jt * LANES)
            idx_ref[e, :, jt * LANES:jt * LANES + w] = jnp.broadcast_to(
                cnt[:, 0:w].astype(jnp.int32), (SUBLANES, w))
        return carry

    lax.fori_loop(0, E, expert, 0)


def _route(hext3, E, C):
    B, T, W = hext3.shape
    nb = T // ROUTE_BLK
    idx8, starts8 = pl.pallas_call(
        functools.partial(_route_kernel, C=C, E=E),
        grid=(B,),
        in_specs=[pl.BlockSpec((None, T, LANES), lambda b: (b, 0, W // LANES - 1))],
        out_specs=[pl.BlockSpec((None, E, SUBLANES, C), lambda b: (b, 0, 0, 0)),
                   pl.BlockSpec((None, nb + 1, SUBLANES, LANES), lambda b: (b, 0, 0, 0))],
        out_shape=[jax.ShapeDtypeStruct((B, E, SUBLANES, C), jnp.int32),
                   jax.ShapeDtypeStruct((B, nb + 1, SUBLANES, LANES), jnp.int32)],
        scratch_shapes=[pltpu.VMEM((T, LANES), F32)],
        compiler_params=_params("parallel"),
        name="ec_route",
    )(hext3)
    return idx8[:, :, 0, :], starts8[:, :, 0, :E]


def _moe(x3, hext, w1, w3, w2, layer, g2, final_g, *, final):
    B, T, D = x3.shape
    E = w1.shape[1]
    C = EC_CAPACITY * T // E
    idx, starts = _route(hext.reshape(B, T, D + LANES), E, C)
    y = _ffn(hext, idx, w1, w3, w2, layer, T)
    return _combine(x3, y, idx, starts, g2, final_g, final=final)


_C_DK, _C_DV, _C_NK, _C_NV, _C_DQ, _C_NQ = 0, 4, 8, 12, 16, 20
ATT_COLS = 6 * BRANCH_W
KV_COLS = 4 * BRANCH_W
GATE_START = 9 * BRANCH_W
_R_POOL, _R_SU, _R_SV = 8, 9, 10


def _mixers(za, zr, zc_kv, T, p, tables, lam_init, *, latent):
    B = za.shape[0]
    sl = lambda a, c: a[:, :, c * LANES:c * LANES + BRANCH_W]
    if latent:
        k_all = jnp.concatenate([sl(zc_kv, _C_DK), sl(za, _C_DK)], axis=1)
        v_all = jnp.concatenate([sl(zc_kv, _C_DV), sl(za, _C_DV)], axis=1)
        diff = _flash(za, _C_DQ // 4, k_all, v_all, p['lam'], p['subln'], mode="diff", lam_init=lam_init)
        na = _na_attn(za, zc_kv, tables, qcol=_C_NQ, kcol=_C_NK, vcol=_C_NV)
    else:
        diff = _flash(za, _C_DQ // 4, sl(za, _C_DK), sl(za, _C_DV), p['lam'], p['subln'],
                      mode="diff", lam_init=lam_init)
        na = _flash(za, _C_NQ // 4, sl(za, _C_NK), sl(za, _C_NV), p['lam'], p['subln'], mode="dense")
    pool = _pool(zr, p['pool_w'], p['pool_scale'], col=_R_POOL)
    sgu = _sgu(zr, p['sgu_g'], p['sgu_w'], p['sgu_b'], ucol=_R_SU, vcol=_R_SV)
    M = B * T
    f2 = lambda a: a.reshape(M, a.shape[-1])
    return [f2(pool), f2(diff), f2(sgu), f2(na)]


def kernel(x, c, ctx, c_ctx, w_mod, b_mod, norm1_g, norm2_g, w_in, pool_w, pool_scale, diff_lambda,
           diff_subln_g, sgu_norm_g, sgu_w, sgu_b, na_rpb, w_br, w_out, router_w, moe_w1, moe_w3,
           moe_w2, final_g):
    B, T, D = x.shape
    Lc = ctx.shape[1]
    depth = w_in.shape[0]
    rows = T // GRID_W
    rope = _rope_tables(T)
    cvec = jnp.concatenate([c, c_ctx[None], jnp.zeros((8 - B - 1, D), F32)], axis=0)
    x_lat = x.reshape(B * T, D)
    x_ctx = ctx.reshape(B * Lc, D)
    for l in range(depth):
        last = l == depth - 1
        lam_init = 0.8 - 0.6 * math.exp(-0.3 * l)
        mod = _mod_vectors(cvec, w_mod, b_mod, l)
        mx = mod[:B].reshape(B, 6, 1, D)
        mc = mod[B:B + 1].reshape(1, 6, 1, D)
        sh1, sc1, g1, sh2, sc2, g2 = [mx[:, i] for i in range(6)]
        csh1, csc1, cg1, csh2, csc2, cg2 = [mc[:, i] for i in range(6)]
        w_in_b = w_in[l].astype(BF16)
        w_att = w_in_b[:, :ATT_COLS]
        w_rest = jnp.concatenate([w_in_b[:, GATE_START:], w_in_b[:, ATT_COLS:GATE_START]], axis=1)
        p = {'lam': diff_lambda[l], 'subln': diff_subln_g[l], 'pool_w': pool_w[l],
             'pool_scale': pool_scale[l], 'sgu_g': sgu_norm_g[l], 'sgu_w': sgu_w[l], 'sgu_b': sgu_b[l]}
        w_br_b = w_br[l].astype(BF16)
        w_out_b = w_out[l].astype(BF16)
        tables = _na_tables(na_rpb[l], rows)

        if last:
            zc_a = _norm_proj(x_ctx, norm1_g[l], csh1, csc1, w_att[:, :KV_COLS], BF16)
        else:
            zc_a = _norm_proj(x_ctx, norm1_g[l], csh1, csc1, w_att, BF16)
        zc_a = zc_a.reshape(B, Lc, -1)

        za = _norm_proj(x_lat, norm1_g[l], sh1, sc1, w_att, BF16, rope=rope,
                        rope_tiles=(_C_DK // 4, _C_DQ // 4)).reshape(B, T, ATT_COLS)
        zr = _norm_proj(x_lat, norm1_g[l], sh1, sc1, w_rest, F32).reshape(B, T, -1)
        br = _mixers(za, zr, zc_a, T, p, tables, lam_init, latent=True)
        x_lat, h2 = _merge(br, zr.reshape(B * T, -1), w_br_b, w_out_b, x_lat, g1,
                                   norm2_g[l], sh2, sc2, router_w[l])
        x_lat = _moe(x_lat.reshape(B, T, D), h2, moe_w1, moe_w3, moe_w2, l, g2,
                     final_g, final=last).reshape(B * T, D)

        if not last:
            zc_r = _norm_proj(x_ctx, norm1_g[l], csh1, csc1, w_rest, F32).reshape(B, Lc, -1)
            brc = _mixers(zc_a, zc_r, None, Lc, p, None, lam_init, latent=False)
            x_ctx, hc2 = _merge(brc, zc_r.reshape(B * Lc, -1), w_br_b, w_out_b, x_ctx, cg1,
                                     norm2_g[l], csh2, csc2, router_w[l])
            x_ctx = _moe(x_ctx.reshape(B, Lc, D), hc2, moe_w1, moe_w3, moe_w2, l, cg2,
                         final_g, final=False).reshape(B * Lc, D)
    return x_lat.reshape(B, T, D)
```

```python
import functools
import math

import numpy as np
import jax
import jax.numpy as jnp
from jax import lax
from jax.experimental import pallas as pl
from jax.experimental.pallas import tpu as pltpu

F32 = jnp.float32
BF16 = jnp.bfloat16

EPS = 1e-6
GRID_W = 64
LANES = 128
SUBLANES = 8
HALF = 64
ONES_ROWS = 16
N_BRANCH = 4
BRANCH_W = 512
POOL_WINDOWS = (2, 4, 8, 16)
POOL_HALO = 16
SGU_CHUNK = 128
NA_WIN_R = 8
NA_WIN_C = 16
NA_QROWS = 8
NA_KROWS = 16
N_EXPERTS = 16
EC_CAPACITY = 2
ROUTE_BLK = 128
FFN_ROWS = 256
F32_MANT_BITS = 23
F32_EXP_BIAS = 127
ROPE_BASE = 10000.0
NEG = -1e30
VMEM_LIMIT = 56 * 1024 * 1024

_NT = (((1,), (1,)), ((), ()))


def _params(*sem):
    return pltpu.CompilerParams(dimension_semantics=sem, vmem_limit_bytes=VMEM_LIMIT)


def _pick(n, cands):
    for c in cands:
        if n % c == 0:
            return c
    raise ValueError(f"no tile for {n}")


def _mod_kernel(c_ref, w_ref, b_ref, o_ref):
    c = c_ref[...]
    s = c * jax.nn.sigmoid(c)
    o_ref[...] = jnp.dot(s, w_ref[...], preferred_element_type=F32,
                         precision=lax.Precision.HIGHEST) + b_ref[...]


def _mod_vectors(cvec, w, b, layer):
    L, D, N = w.shape
    tn = _pick(N, (1536, 1024, 512, 128))
    return pl.pallas_call(
        _mod_kernel,
        grid=(N // tn,),
        in_specs=[pl.BlockSpec((8, D), lambda j: (0, 0)),
                  pl.BlockSpec((None, D, tn), lambda j: (layer, 0, j)),
                  pl.BlockSpec((None, 1, tn), lambda j: (layer, 0, j))],
        out_specs=pl.BlockSpec((8, tn), lambda j: (0, j)),
        out_shape=jax.ShapeDtypeStruct((8, N), F32),
        compiler_params=_params("parallel"),
        name="mod_vectors",
    )(cvec, w, b.reshape(L, 1, N))


def _rope_store(acc, cos, sin, o_ref):
    lane = lax.broadcasted_iota(jnp.int32, (1, LANES), 1)
    first = (lane % 32) < 16
    for h in range(acc.shape[1] // LANES):
        a = acc[:, h * LANES:(h + 1) * LANES]
        partner = jnp.where(first, pltpu.roll(a, LANES - 16, 1), pltpu.roll(a, 16, 1))
        o_ref[:, h * LANES:(h + 1) * LANES] = (a * cos + partner * sin).astype(o_ref.dtype)


def _norm_proj_kernel(x_ref, g_ref, sh_ref, sc_ref, w_ref, *rest, rope_tiles):
    if rope_tiles:
        cos_ref, sin_ref, o_ref, h_ref = rest
    else:
        o_ref, h_ref = rest
    j = pl.program_id(1)

    @pl.when(j == 0)
    def _():
        x = x_ref[...]
        y = x * lax.rsqrt(jnp.mean(x * x, axis=-1, keepdims=True) + EPS) * g_ref[...]
        h_ref[...] = (y * (1.0 + sc_ref[...]) + sh_ref[...]).astype(BF16)

    acc = jnp.dot(h_ref[...], w_ref[...], preferred_element_type=F32)
    if rope_tiles:
        is_rope = functools.reduce(jnp.logical_or, [j == t for t in rope_tiles])

        @pl.when(is_rope)
        def _():
            _rope_store(acc, cos_ref[...], sin_ref[...], o_ref)

        @pl.when(jnp.logical_not(is_rope))
        def _():
            o_ref[...] = acc.astype(o_ref.dtype)
    else:
        o_ref[...] = acc.astype(o_ref.dtype)


def _norm_proj(x2, g, shift, scale, w, out_dtype, rope=None, rope_tiles=()):
    M, D = x2.shape
    N = w.shape[1]
    G = shift.shape[0]
    rows_per_group = M // G
    tm = _pick(rows_per_group, (1024, 512, 256))
    tn = 512
    assert N % tn == 0
    in_specs = [pl.BlockSpec((tm, D), lambda i, j: (i, 0)),
                pl.BlockSpec((1, D), lambda i, j: (0, 0)),
                pl.BlockSpec((None, 1, D), lambda i, j: (i * tm // rows_per_group, 0, 0)),
                pl.BlockSpec((None, 1, D), lambda i, j: (i * tm // rows_per_group, 0, 0)),
                pl.BlockSpec((D, tn), lambda i, j: (0, j))]
    args = [x2, g.reshape(1, D), shift, scale, w]
    if rope_tiles:
        cos, sin = rope
        nt = cos.shape[0] // tm
        in_specs += [pl.BlockSpec((tm, LANES), lambda i, j: (i % nt, 0)),
                     pl.BlockSpec((tm, LANES), lambda i, j: (i % nt, 0))]
        args += [cos, sin]
    return pl.pallas_call(
        functools.partial(_norm_proj_kernel, rope_tiles=tuple(rope_tiles)),
        grid=(M // tm, N // tn),
        in_specs=in_specs,
        out_specs=pl.BlockSpec((tm, tn), lambda i, j: (i, j)),
        out_shape=jax.ShapeDtypeStruct((M, N), out_dtype),
        scratch_shapes=[pltpu.VMEM((tm, D), BF16)],
        compiler_params=_params("parallel", "arbitrary"),
        name="norm_proj",
    )(*args)


def _rope_tables(T):
    t = jnp.arange(T)
    row = (t // GRID_W).astype(F32)
    col = (t % GRID_W).astype(F32)
    nf = HALF // 4
    inv = ROPE_BASE ** (-jnp.arange(nf, dtype=F32) / nf)
    ar = row[:, None] * inv
    ac = col[:, None] * inv
    cos = jnp.concatenate([jnp.cos(ar), jnp.cos(ar), jnp.cos(ac), jnp.cos(ac)], axis=-1)
    sin = jnp.concatenate([-jnp.sin(ar), jnp.sin(ar), -jnp.sin(ac), jnp.sin(ac)], axis=-1)
    return jnp.tile(cos, (1, 2)), jnp.tile(sin, (1, 2))


def _flash_kernel(q_ref, k_ref, v_ref, lam_ref, g_ref, o_ref, m_ref, acc_ref, sa_ref, sb_ref, *,
                  mode, tk, lam_init):
    tq = q_ref.shape[0]
    nk = k_ref.shape[0] // tk
    ones = jnp.ones((ONES_ROWS, tk), BF16)
    sub = lax.broadcasted_iota(jnp.int32, (LANES, 1), 0)
    lo = sub < HALF
    if mode == "diff":
        lp = lam_ref[...]
        d1 = jnp.sum(lp[0:1, :] * lp[1:2, :], keepdims=True)
        d2 = jnp.sum(lp[2:3, :] * lp[3:4, :], keepdims=True)
        lam = jnp.exp(d1) - jnp.exp(d2) + lam_init
    for h in range(q_ref.shape[1] // LANES):
        cs = slice(h * LANES, (h + 1) * LANES)
        qh = q_ref[:, cs].astype(F32).T.astype(BF16)
        zero = jnp.zeros_like(qh)
        q2 = jnp.concatenate([jnp.where(lo, qh, zero), jnp.where(lo, zero, qh)], axis=1)
        q2 = q2 * jnp.asarray(HALF ** -0.5, BF16)
        m_ref[...] = jnp.full(m_ref.shape, -jnp.inf, F32)
        acc_ref[...] = jnp.zeros(acc_ref.shape, F32)

        def scores(c, s_ref, q2=q2, cs=cs):
            off = pl.multiple_of(c * tk, tk)
            s_ref[...] = jnp.dot(k_ref[pl.ds(off, tk), cs], q2, preferred_element_type=F32)

        def update(c, s_ref, cs=cs):
            off = pl.multiple_of(c * tk, tk)
            vc = jnp.concatenate([v_ref[cs, pl.ds(off, tk)], ones], axis=0)
            s = s_ref[...]
            m_old = m_ref[...]
            m_new = jnp.maximum(m_old, jnp.max(s, axis=0, keepdims=True))
            alpha = jnp.exp(m_old - m_new)
            p = jnp.exp(s - m_new).astype(BF16)
            acc_ref[...] = alpha * acc_ref[...] + jnp.dot(vc, p, preferred_element_type=F32)
            m_ref[...] = m_new

        scores(0, sa_ref)

        def body(j, carry):
            scores(2 * j + 1, sb_ref)
            update(2 * j, sa_ref)
            scores(2 * j + 2, sa_ref)
            update(2 * j + 1, sb_ref)
            return carry

        lax.fori_loop(0, (nk - 1) // 2, body, 0)
        update(nk - 1, sa_ref)
        a = acc_ref[0:LANES, :] / acc_ref[LANES:LANES + 1, :]
        a0 = a[:, :tq]
        a1 = a[:, tq:]
        if mode == "diff":
            o = a0 - lam * a1
            o = o * lax.rsqrt(jnp.mean(o * o, axis=0, keepdims=True) + EPS)
            o = o * g_ref[...] * (1.0 - lam_init)
        else:
            o = jnp.where(lo, a0, a1)
        o_ref[:, cs] = o.T.astype(o_ref.dtype)


def _flash(z, qblk, k, v, lam_p, g, *, mode, lam_init=0.0):
    B, Tq, _ = z.shape
    Tk, W = k.shape[1:]
    tq = _pick(Tq, (512, 256, 128))
    tk = next(c for c in (768, 512, 384, 256, 128) if Tk % c == 0 and (Tk // c) % 2 == 1)
    return pl.pallas_call(
        functools.partial(_flash_kernel, mode=mode, tk=tk, lam_init=lam_init),
        grid=(B, Tq // tq),
        in_specs=[pl.BlockSpec((None, tq, W), lambda b, i: (b, i, qblk)),
                  pl.BlockSpec((None, Tk, W), lambda b, i: (b, 0, 0)),
                  pl.BlockSpec((None, W, Tk), lambda b, i: (b, 0, 0)),
                  pl.BlockSpec(lam_p.shape, lambda b, i: (0, 0)),
                  pl.BlockSpec((LANES, 1), lambda b, i: (0, 0))],
        out_specs=pl.BlockSpec((None, tq, W), lambda b, i: (b, i, 0)),
        out_shape=jax.ShapeDtypeStruct((B, Tq, W), BF16),
        scratch_shapes=[pltpu.VMEM((1, 2 * tq), F32), pltpu.VMEM((LANES + ONES_ROWS, 2 * tq), F32),
                        pltpu.VMEM((tk, 2 * tq), F32), pltpu.VMEM((tk, 2 * tq), F32)],
        compiler_params=_params("parallel", "parallel"),
        name="flash_" + mode,
    )(z, k, jnp.swapaxes(v, 1, 2), lam_p, g.reshape(LANES, 1))


def _na_kernel(q_ref, k_ref, v_ref, kc_ref, vc_ref, b_ref, o_ref, *, rows):
    rb = pl.program_id(2)
    nq = NA_QROWS * GRID_W
    nkk = NA_KROWS * GRID_W
    k0 = jnp.clip(rb * NA_QROWS - NA_WIN_R // 2, 0, rows - NA_KROWS) * GRID_W
    k0 = pl.multiple_of(k0, 4 * GRID_W)
    kl = k_ref[pl.ds(k0, nkk), :]
    vl = v_ref[pl.ds(k0, nkk), :]
    kc = kc_ref[...]
    vc = vc_ref[...]
    q = q_ref[...]
    zero = jnp.zeros_like(q)
    lane = lax.broadcasted_iota(jnp.int32, (1, LANES), 1)
    lo = lane < HALF
    scale = jnp.asarray(HALF ** -0.5, BF16)
    outs = []
    for hh in range(2):
        qm = (jnp.where(lo, q, zero) if hh == 0 else jnp.where(lo, zero, q)) * scale
        s_loc = lax.dot_general(qm, kl, _NT, preferred_element_type=F32) + b_ref[hh]
        s_ctx = lax.dot_general(qm, kc, _NT, preferred_element_type=F32)
        m = jnp.maximum(jnp.max(s_loc, axis=-1, keepdims=True), jnp.max(s_ctx, axis=-1, keepdims=True))
        p_loc = jnp.exp(s_loc - m)
        p_ctx = jnp.exp(s_ctx - m)
        l = jnp.sum(p_loc, axis=-1, keepdims=True) + jnp.sum(p_ctx, axis=-1, keepdims=True)
        o = jnp.dot(p_loc.astype(BF16), vl, preferred_element_type=F32) \
            + jnp.dot(p_ctx.astype(BF16), vc, preferred_element_type=F32)
        outs.append(o / l)
    del nq
    o_ref[...] = jnp.where(lo, outs[0], outs[1]).astype(o_ref.dtype)


def _na_tables(rpb, rows):
    H = rpb.shape[0]
    c = np.arange(GRID_W)
    cstart = np.clip(c - NA_WIN_C // 2, 0, GRID_W - NA_WIN_C)
    kcol = np.arange(GRID_W)
    col_ok = (kcol[None, :] >= cstart[:, None]) & (kcol[None, :] < cstart[:, None] + NA_WIN_C)
    nr, nc = 2 * NA_WIN_R - 1, 2 * NA_WIN_C - 1
    dc = np.where(col_ok, kcol[None, :] - c[:, None] + NA_WIN_C - 1, nc)
    cm = np.arange(nc + 1)[None, None, :] == dc[:, :, None]
    rms = []
    for q0, k0, clamp in ((0, 0, True), (NA_QROWS, NA_QROWS - NA_WIN_R // 2, False),
                          (rows - NA_QROWS, rows - NA_KROWS, True)):
        r = q0 + np.arange(NA_QROWS)
        kr = k0 + np.arange(NA_KROWS)
        r0 = r - NA_WIN_R // 2
        if clamp:
            r0 = np.clip(r0, 0, rows - NA_WIN_R)
        row_ok = (kr[None, :] >= r0[:, None]) & (kr[None, :] < r0[:, None] + NA_WIN_R)
        dr = np.where(row_ok, kr[None, :] - r[:, None] + NA_WIN_R - 1, nr)
        rms.append(np.arange(nr + 1)[None, None, :] == dr[:, :, None])
    rpb_ext = jnp.pad(rpb, ((0, 0), (0, 1), (0, 1)), constant_values=NEG)
    t = jnp.einsum('vrki,hij,cqj->vhrckq', jnp.asarray(np.stack(rms), F32), rpb_ext,
                   jnp.asarray(cm, F32), precision=lax.Precision.HIGHEST)
    return t.reshape(3, H, NA_QROWS * GRID_W, NA_KROWS * GRID_W)


def _na_attn(z, zc, tables, *, qcol, kcol, vcol):
    B, T, _ = z.shape
    Lc = zc.shape[1]
    rows = T // GRID_W
    nrb = rows // NA_QROWS
    nq = NA_QROWS * GRID_W
    nkk = NA_KROWS * GRID_W
    nhp = BRANCH_W // LANES

    def var(rb):
        return jnp.where(rb == 0, 0, jnp.where(rb == nrb - 1, 2, 1))

    return pl.pallas_call(
        functools.partial(_na_kernel, rows=rows),
        grid=(nhp, B, nrb),
        in_specs=[pl.BlockSpec((None, nq, LANES), lambda h, b, r: (b, r, qcol + h)),
                  pl.BlockSpec((None, T, LANES), lambda h, b, r: (b, 0, kcol + h)),
                  pl.BlockSpec((None, T, LANES), lambda h, b, r: (b, 0, vcol + h)),
                  pl.BlockSpec((None, Lc, LANES), lambda h, b, r: (b, 0, kcol + h)),
                  pl.BlockSpec((None, Lc, LANES), lambda h, b, r: (b, 0, vcol + h)),
                  pl.BlockSpec((None, 2, nq, nkk), lambda h, b, r: (var(r), h, 0, 0))],
        out_specs=pl.BlockSpec((None, nq, LANES), lambda h, b, r: (b, r, h)),
        out_shape=jax.ShapeDtypeStruct((B, T, BRANCH_W), BF16),
        compiler_params=_params("parallel", "parallel", "parallel"),
        name="na_attn",
    )(z, z, z, zc, zc, tables)


def _pool_kernel(prev_ref, cur_ref, next_ref, w_ref, sc_ref, o_ref, *, T):
    i = pl.program_id(1)
    tt = cur_ref.shape[0]
    E = tt + 2 * POOL_HALO
    ext = jnp.concatenate([prev_ref[...], cur_ref[...], next_ref[...]], axis=0).astype(F32)
    gi = i * tt - POOL_HALO + lax.broadcasted_iota(jnp.int32, (E, 1), 0)
    ext = jnp.where((gi >= 0) & (gi < T), ext, 0.0)
    t = i * tt + lax.broadcasted_iota(jnp.int32, (tt, 1), 0)
    for g, w in enumerate(POOL_WINDOWS):
        cs = slice(g * LANES, (g + 1) * LANES)
        zg = ext[:, cs]
        s = zg + pltpu.roll(zg, 1, 0)
        half = 1
        while 2 * half < w:
            s = pltpu.roll(s, half, 0) + pltpu.roll(s, E - half, 0)
            half *= 2
        cnt = (jnp.minimum(t + w // 2, T) - jnp.maximum(t - w // 2, 0)).astype(F32)
        c0 = slice(POOL_HALO, POOL_HALO + tt)
        d = s[c0] / cnt - zg[c0]
        y = jnp.dot(d.astype(BF16), w_ref[g].astype(BF16), preferred_element_type=F32)
        o_ref[:, cs] = (y * sc_ref[:, cs]).astype(o_ref.dtype)


def _pool(z, w_pool, scale, *, col):
    B, T, _ = z.shape
    tt = _pick(T, (512, 256))
    hb = tt // POOL_HALO
    nh = T // POOL_HALO
    return pl.pallas_call(
        functools.partial(_pool_kernel, T=T),
        grid=(B, T // tt),
        in_specs=[pl.BlockSpec((None, POOL_HALO, BRANCH_W), lambda b, i: (b, jnp.maximum(i * hb - 1, 0), col)),
                  pl.BlockSpec((None, tt, BRANCH_W), lambda b, i: (b, i, col)),
                  pl.BlockSpec((None, POOL_HALO, BRANCH_W), lambda b, i: (b, jnp.minimum((i + 1) * hb, nh - 1), col)),
                  pl.BlockSpec(w_pool.shape, lambda b, i: (0, 0, 0)),
                  pl.BlockSpec((1, BRANCH_W), lambda b, i: (0, 0))],
        out_specs=pl.BlockSpec((None, tt, BRANCH_W), lambda b, i: (b, i, 0)),
        out_shape=jax.ShapeDtypeStruct((B, T, BRANCH_W), BF16),
        compiler_params=_params("parallel", "parallel"),
        name="pool_mixer",
    )(z, z, z, w_pool, scale.reshape(1, BRANCH_W))


def _sgu_kernel(u_ref, v_ref, g_ref, w_ref, bt_ref, o_ref):
    ts = u_ref.shape[0]
    u = jax.nn.gelu(u_ref[...].astype(F32))
    v = jax.nn.gelu(v_ref[...].astype(F32))
    vc = v - jnp.mean(v, axis=-1, keepdims=True)
    v = vc * lax.rsqrt(jnp.mean(vc * vc, axis=-1, keepdims=True) + EPS) * g_ref[...]
    vb = v.astype(BF16)
    for g in range(BRANCH_W // LANES):
        cs = slice(g * LANES, (g + 1) * LANES)
        wg = w_ref[g].astype(BF16)
        bg = bt_ref[:, g:g + 1]
        for ck in range(ts // SGU_CHUNK):
            rs = slice(ck * SGU_CHUNK, (ck + 1) * SGU_CHUNK)
            f = jnp.dot(wg, vb[rs, cs], preferred_element_type=F32) + bg
            o_ref[rs, cs] = (u[rs, cs] * f).astype(o_ref.dtype)


def _sgu(z, g, w_s, b, *, ucol, vcol):
    B, T, _ = z.shape
    ts = _pick(T, (512, 256, 128))
    return pl.pallas_call(
        _sgu_kernel,
        grid=(B, T // ts),
        in_specs=[pl.BlockSpec((None, ts, BRANCH_W), lambda b_, i: (b_, i, ucol)),
                  pl.BlockSpec((None, ts, BRANCH_W), lambda b_, i: (b_, i, vcol)),
                  pl.BlockSpec((1, BRANCH_W), lambda b_, i: (0, 0)),
                  pl.BlockSpec(w_s.shape, lambda b_, i: (0, 0, 0)),
                  pl.BlockSpec((SGU_CHUNK, BRANCH_W // LANES), lambda b_, i: (0, 0))],
        out_specs=pl.BlockSpec((None, ts, BRANCH_W), lambda b_, i: (b_, i, 0)),
        out_shape=jax.ShapeDtypeStruct((B, T, BRANCH_W), BF16),
        compiler_params=_params("parallel", "parallel"),
        name="sgu_mixer",
    )(z, z, g.reshape(1, BRANCH_W), w_s, b.T)


def _merge_kernel(p_ref, d_ref, s_ref, n_ref, gz_ref, wbr_ref, wout_ref, x_ref, g1_ref,
                  n2_ref, sh_ref, sc_ref, rw_ref, xo_ref, h_ref, *, n_experts):
    D = x_ref.shape[1]
    acc = None
    for i, br in enumerate((p_ref, d_ref, s_ref, n_ref)):
        proj = jnp.dot(br[...], wbr_ref[i], preferred_element_type=F32)
        t = jax.nn.sigmoid(gz_ref[:, i * D:(i + 1) * D].astype(F32)) * proj
        acc = t if acc is None else acc + t
    mix = jnp.dot(acc.astype(BF16), wout_ref[...], preferred_element_type=F32)
    x = x_ref[...] + g1_ref[...] * mix
    xo_ref[...] = x
    y = x * lax.rsqrt(jnp.mean(x * x, axis=-1, keepdims=True) + EPS) * n2_ref[...]
    h = y * (1.0 + sc_ref[...]) + sh_ref[...]
    h_ref[:, 0:D] = h
    rw = rw_ref[...]
    h_hi = h.astype(BF16)
    h_lo = (h - h_hi.astype(F32)).astype(BF16)
    w_hi = rw.astype(BF16)
    w_lo = (rw - w_hi.astype(F32)).astype(BF16)
    lg = (jnp.dot(h_hi, w_hi, preferred_element_type=F32)
          + jnp.dot(h_lo, w_hi, preferred_element_type=F32)
          + jnp.dot(h_hi, w_lo, preferred_element_type=F32))
    lane = lax.broadcasted_iota(jnp.int32, (1, LANES), 1)
    lg = jnp.where(lane < n_experts, lg, -jnp.inf)
    ex = jnp.exp(lg - jnp.max(lg, axis=-1, keepdims=True))
    h_ref[:, D:D + LANES] = ex / jnp.sum(ex, axis=-1, keepdims=True)


def _merge(branches, zr, w_br, w_out, x2, g1, n2g, sh2, sc2, router_w):
    M, D = x2.shape
    G = g1.shape[0]
    rpg = M // G
    tm = _pick(rpg, (256,))
    E = router_w.shape[1]
    router_w = jnp.pad(router_w, ((0, 0), (0, LANES - E)))
    gb = 0
    grp = lambda i: (i * tm // rpg, 0, 0)
    row = lambda i: (i, 0)
    return pl.pallas_call(
        functools.partial(_merge_kernel, n_experts=E),
        grid=(M // tm,),
        in_specs=[pl.BlockSpec((tm, BRANCH_W), row)] * 4 + [
            pl.BlockSpec((tm, N_BRANCH * D), lambda i: (i, gb)),
            pl.BlockSpec(w_br.shape, lambda i: (0, 0, 0)),
            pl.BlockSpec(w_out.shape, lambda i: (0, 0)),
            pl.BlockSpec((tm, D), row),
            pl.BlockSpec((None, 1, D), grp),
            pl.BlockSpec((1, D), lambda i: (0, 0)),
            pl.BlockSpec((None, 1, D), grp),
            pl.BlockSpec((None, 1, D), grp),
            pl.BlockSpec(router_w.shape, lambda i: (0, 0))],
        out_specs=[pl.BlockSpec((tm, D), row), pl.BlockSpec((tm, D + LANES), row)],
        out_shape=[jax.ShapeDtypeStruct((M, D), F32), jax.ShapeDtypeStruct((M, D + LANES), F32)],
        compiler_params=_params("parallel"),
        name="merge_out",
    )(*branches, zr, w_br, w_out, x2, g1, n2g.reshape(1, D), sh2, sc2, router_w)


def _ffn_kernel(rows_ref, h_hbm, w1_ref, w3_ref, w2_ref, y_ref, w1b, w3b, w2b, xbuf, sem, *, B):
    e = pl.program_id(0)
    b = pl.program_id(1)
    C = xbuf.shape[1]
    D = w1_ref.shape[0]
    step = e * B + b
    n_exp = pl.num_programs(0)
    nsteps = n_exp * B
    slot = step % 2

    def start_row(base, slot_, j):
        pltpu.make_async_copy(h_hbm.at[pl.ds(rows_ref[base + j], 1)], xbuf.at[slot_, pl.ds(j, 1)],
                              sem.at[slot_]).start()

    def wait_slot(slot_):
        pltpu.make_async_copy(h_hbm.at[pl.ds(0, C)], xbuf.at[slot_], sem.at[slot_]).wait()

    @pl.when(step == 0)
    def _():
        def issue(j, carry):
            start_row((b * n_exp + e) * C, slot, j)
            return carry

        lax.fori_loop(0, C, issue, 0, unroll=8)

    @pl.when(b == 0)
    def _():
        w1b[...] = w1_ref[...].astype(BF16)
        w3b[...] = w3_ref[...].astype(BF16)
        w2b[...] = w2_ref[...].astype(BF16)

    last = step + 1 >= nsteps
    nxt_b = jnp.where(last, b, (b + 1) % B)
    nxt_e = jnp.where(last, e, e + (b + 1) // B)
    nxt_base = (nxt_b * n_exp + nxt_e) * C
    wait_slot(slot)
    lane = lax.broadcasted_iota(jnp.int32, (1, LANES), 1)
    rc = min(C, FFN_ROWS)
    for r in range(C // rc):
        for j in range(r * rc, (r + 1) * rc):
            start_row(nxt_base, 1 - slot, j)
        rs = pl.ds(r * rc, rc)
        xs = xbuf[slot, rs, 0:D].astype(BF16)
        gate = jnp.sum(jnp.where(lane == e, xbuf[slot, rs, D:D + LANES], 0.0), axis=-1, keepdims=True)
        a = jnp.dot(xs, w1b[...], preferred_element_type=F32)
        g = jnp.dot(xs, w3b[...], preferred_element_type=F32)
        hid = (a * jax.nn.sigmoid(a) * g).astype(BF16)
        y = jnp.dot(hid, w2b[...], preferred_element_type=F32) * gate
        y_ref[rs, :] = y.astype(y_ref.dtype)

    @pl.when(last)
    def _():
        wait_slot(1 - slot)


def _ffn(hext, idx, w1, w3, w2, layer, T):
    B, E, C = idx.shape
    D, Fd = w1.shape[2:]
    grid_spec = pltpu.PrefetchScalarGridSpec(
        num_scalar_prefetch=1,
        grid=(E, B),
        in_specs=[pl.BlockSpec(memory_space=pl.ANY),
                  pl.BlockSpec((None, None, D, Fd), lambda e, b, ix: (layer, e, 0, 0)),
                  pl.BlockSpec((None, None, D, Fd), lambda e, b, ix: (layer, e, 0, 0)),
                  pl.BlockSpec((None, None, Fd, D), lambda e, b, ix: (layer, e, 0, 0))],
        out_specs=pl.BlockSpec((None, None, C, D), lambda e, b, ix: (b, e, 0, 0)),
        scratch_shapes=[pltpu.VMEM((D, Fd), BF16), pltpu.VMEM((D, Fd), BF16), pltpu.VMEM((Fd, D), BF16),
                        pltpu.VMEM((2, C, D + LANES), F32), pltpu.SemaphoreType.DMA((2,))],
    )
    return pl.pallas_call(
        functools.partial(_ffn_kernel, B=B),
        grid_spec=grid_spec,
        out_shape=jax.ShapeDtypeStruct((B, E, C, D), BF16),
        compiler_params=_params("arbitrary", "arbitrary"),
        name="expert_ffn",
    )((idx + jnp.arange(B, dtype=jnp.int32)[:, None, None] * T).reshape(-1), hext, w1, w3, w2)


def _combine_kernel(starts_ref, x_ref, g2_ref, fg_ref, idx_ref, *rest, wb, nwin, final, spt):
    ys = rest[:-2]
    o_ref, acc_ref = rest[-2:]
    b = pl.program_id(0)
    t = pl.program_id(1)
    tt = x_ref.shape[0]
    E = idx_ref.shape[0]
    C = idx_ref.shape[1]
    tok = t * tt + lax.broadcasted_iota(jnp.int32, (tt, 1), 0)

    def window_onehot(e, wk):
        if wb == C:
            ids = idx_ref[e:e + 1, :]
        else:
            ids = idx_ref[e:e + 1, pl.ds(pl.multiple_of(wk * wb, wb), wb)]
        return jnp.where(ids == tok, 1.0, 0.0).astype(BF16)

    def window_dot(e, k, wk):
        return jnp.dot(window_onehot(e, wk), ys[e * nwin + k][...], preferred_element_type=F32)

    onehots = []
    for e in range(E):
        w0 = jnp.minimum(starts_ref[b, t * spt, e] // wb, C // wb - 1)
        onehots.append(window_onehot(e, w0))
    if wb % LANES == 0:
        acc_ref[...] = jnp.dot(jnp.concatenate(onehots, axis=1),
                               jnp.concatenate([ys[e * nwin][...] for e in range(E)], axis=0),
                               preferred_element_type=F32)
    else:
        acc = None
        for e in range(E):
            d = jnp.dot(onehots[e], ys[e * nwin][...], preferred_element_type=F32)
            acc = d if acc is None else acc + d
        acc_ref[...] = acc
    if nwin == 2:
        for e in range(E):
            s0 = starts_ref[b, t * spt, e]
            s1 = starts_ref[b, (t + 1) * spt, e]
            w0 = s0 // wb

            @pl.when((s1 > s0) & ((s1 - 1) // wb > w0))
            def _(e=e, w0=w0):
                acc_ref[...] += window_dot(e, 1, w0 + 1)

    x = x_ref[...] + g2_ref[...] * acc_ref[...]
    if final:
        x = x * lax.rsqrt(jnp.mean(x * x, axis=-1, keepdims=True) + EPS) * fg_ref[...]
    o_ref[...] = x


def _combine(x3, y, idx, starts, g2, final_g, *, final):
    B, T, D = x3.shape
    E, C = idx.shape[1:]
    if C >= LANES:
        wb, nwin, tt = LANES, 2, LANES
    else:
        wb, nwin, tt = C, 1, T
    nt = T // tt
    spt = tt // ROUTE_BLK
    G = g2.shape[0]

    def ymap(e, k):
        return lambda b, t, st: (b, e, jnp.minimum(st[b, t * spt, e] // wb + k, C // wb - 1), 0)

    y_specs = [pl.BlockSpec((None, None, wb, D), ymap(e, k)) for e in range(E) for k in range(nwin)]
    grid_spec = pltpu.PrefetchScalarGridSpec(
        num_scalar_prefetch=1,
        grid=(B, nt),
        in_specs=[pl.BlockSpec((None, tt, D), lambda b, t, st: (b, t, 0)),
                  pl.BlockSpec((None, 1, D), lambda b, t, st: (b * G // B, 0, 0)),
                  pl.BlockSpec((1, D), lambda b, t, st: (0, 0)),
                  pl.BlockSpec((None, E, C), lambda b, t, st: (b, 0, 0))] + y_specs,
        out_specs=pl.BlockSpec((None, tt, D), lambda b, t, st: (b, t, 0)),
        scratch_shapes=[pltpu.VMEM((tt, D), F32)],
    )
    return pl.pallas_call(
        functools.partial(_combine_kernel, wb=wb, nwin=nwin, final=final, spt=spt),
        grid_spec=grid_spec,
        out_shape=jax.ShapeDtypeStruct((B, T, D), F32),
        compiler_params=_params("parallel", "arbitrary"),
        name="moe_combine",
    )(starts, x3, g2, final_g.reshape(1, D), idx, *([y] * (E * nwin)))


def _route_kernel(aff_ref, idx_ref, starts_ref, cs_ref, *, C, E):
    T = aff_ref.shape[0]
    nb = T // ROUTE_BLK

    def value_of(pattern):
        ex = jnp.right_shift(pattern, F32_MANT_BITS)
        frac = (pattern & ((1 << F32_MANT_BITS) - 1)).astype(F32) * (2.0 ** -F32_MANT_BITS)
        n = F32_EXP_BIAS - ex
        p = jnp.ones(pattern.shape, F32)
        for i in range(7):
            p = jnp.where((jnp.right_shift(n, i) & 1) == 1, p * (2.0 ** -(2 ** i)), p)
        return jnp.where(ex == 0, 0.0, p * (1.0 + frac))

    def bit_body(i, prefix):
        cand = prefix | jnp.left_shift(jnp.int32(1), 29 - i)
        cnt = jnp.sum(jnp.where(aff_ref[...] >= value_of(cand), 1.0, 0.0), axis=0, keepdims=True)
        return jnp.where(cnt >= C, cand, prefix)

    vstar = value_of(lax.fori_loop(0, 30, bit_body, jnp.zeros((1, LANES), jnp.int32)))
    n_gt = jnp.sum(jnp.where(aff_ref[...] > vstar, 1.0, 0.0), axis=0, keepdims=True)
    n_tie = C - n_gt

    r = lax.broadcasted_iota(jnp.int32, (ROUTE_BLK, ROUTE_BLK), 0)
    c = lax.broadcasted_iota(jnp.int32, (ROUTE_BLK, ROUTE_BLK), 1)
    lower_strict = jnp.where(c < r, 1.0, 0.0).astype(BF16)
    lower_incl = jnp.where(c <= r, 1.0, 0.0).astype(BF16)

    def blk(k, carry):
        ties_before, chosen_before = carry
        off = pl.multiple_of(k * ROUTE_BLK, ROUTE_BLK)
        a = aff_ref[pl.ds(off, ROUTE_BLK), :]
        gt = a > vstar
        eq = jnp.where(a == vstar, 1.0, 0.0)
        rank = jnp.dot(lower_strict, eq.astype(BF16), preferred_element_type=F32) + ties_before
        sel = jnp.where(gt | ((eq > 0.0) & (rank < n_tie)), 1.0, 0.0)
        csum = jnp.dot(lower_incl, sel.astype(BF16), preferred_element_type=F32) + chosen_before
        cs_ref[pl.ds(off, ROUTE_BLK), :] = csum
        starts_ref[k] = jnp.broadcast_to(chosen_before.astype(jnp.int32), (SUBLANES, LANES))
        return (ties_before + jnp.sum(eq, axis=0, keepdims=True),
                chosen_before + jnp.sum(sel, axis=0, keepdims=True))

    zero = jnp.zeros((1, LANES), F32)
    _, total = lax.fori_loop(0, nb, blk, (zero, zero))
    starts_ref[nb] = jnp.broadcast_to(total.astype(jnp.int32), (SUBLANES, LANES))

    lane_f = lax.broadcasted_iota(jnp.int32, (1, LANES), 1).astype(F32)

    def expert(e, carry):
        pick = jnp.where(r == e, 1.0, 0.0)
        col = jnp.dot(cs_ref[...], pick, preferred_element_type=F32,
                      precision=lax.Precision.HIGHEST)
        for jt in range(-(-C // LANES)):
            cnt = jnp.sum(jnp.where(col <= lane_f + jt * LANES, 1.0, 0.0), axis=0, keepdims=True)
            w = min(LANES, C - jt * LANES)
            idx_ref[e, :, jt * LANES:jt * LANES + w] = jnp.broadcast_to(
                cnt[:, 0:w].astype(jnp.int32), (SUBLANES, w))
        return carry

    lax.fori_loop(0, E, expert, 0)


def _route(hext3, E, C):
    B, T, W = hext3.shape
    nb = T // ROUTE_BLK
    idx8, starts8 = pl.pallas_call(
        functools.partial(_route_kernel, C=C, E=E),
        grid=(B,),
        in_specs=[pl.BlockSpec((None, T, LANES), lambda b: (b, 0, W // LANES - 1))],
        out_specs=[pl.BlockSpec((None, E, SUBLANES, C), lambda b: (b, 0, 0, 0)),
                   pl.BlockSpec((None, nb + 1, SUBLANES, LANES), lambda b: (b, 0, 0, 0))],
        out_shape=[jax.ShapeDtypeStruct((B, E, SUBLANES, C), jnp.int32),
                   jax.ShapeDtypeStruct((B, nb + 1, SUBLANES, LANES), jnp.int32)],
        scratch_shapes=[pltpu.VMEM((T, LANES), F32)],
        compiler_params=_params("parallel"),
        name="ec_route",
    )(hext3)
    return idx8[:, :, 0, :], starts8[:, :, 0, :E]


def _moe(x3, hext, w1, w3, w2, layer, g2, final_g, *, final):
    B, T, D = x3.shape
    E = w1.shape[1]
    C = EC_CAPACITY * T // E
    idx, starts = _route(hext.reshape(B, T, D + LANES), E, C)
    y = _ffn(hext, idx, w1, w3, w2, layer, T)
    return _combine(x3, y, idx, starts, g2, final_g, final=final)


_C_DK, _C_DV, _C_NK, _C_NV, _C_DQ, _C_NQ = 0, 4, 8, 12, 16, 20
ATT_COLS = 6 * BRANCH_W
KV_COLS = 4 * BRANCH_W
GATE_START = 9 * BRANCH_W
_R_POOL, _R_SU, _R_SV = 8, 9, 10


def _mixers(za, zr, zc_kv, T, p, tables, lam_init, *, latent):
    B = za.shape[0]
    sl = lambda a, c: a[:, :, c * LANES:c * LANES + BRANCH_W]
    if latent:
        k_all = jnp.concatenate([sl(zc_kv, _C_DK), sl(za, _C_DK)], axis=1)
        v_all = jnp.concatenate([sl(zc_kv, _C_DV), sl(za, _C_DV)], axis=1)
        diff = _flash(za, _C_DQ // 4, k_all, v_all, p['lam'], p['subln'], mode="diff", lam_init=lam_init)
        na = _na_attn(za, zc_kv, tables, qcol=_C_NQ, kcol=_C_NK, vcol=_C_NV)
    else:
        diff = _flash(za, _C_DQ // 4, sl(za, _C_DK), sl(za, _C_DV), p['lam'], p['subln'],
                      mode="diff", lam_init=lam_init)
        na = _flash(za, _C_NQ // 4, sl(za, _C_NK), sl(za, _C_NV), p['lam'], p['subln'], mode="dense")
    pool = _pool(zr, p['pool_w'], p['pool_scale'], col=_R_POOL)
    sgu = _sgu(zr, p['sgu_g'], p['sgu_w'], p['sgu_b'], ucol=_R_SU, vcol=_R_SV)
    M = B * T
    f2 = lambda a: a.reshape(M, a.shape[-1])
    return [f2(pool), f2(diff), f2(sgu), f2(na)]


def kernel(x, c, ctx, c_ctx, w_mod, b_mod, norm1_g, norm2_g, w_in, pool_w, pool_scale, diff_lambda,
           diff_subln_g, sgu_norm_g, sgu_w, sgu_b, na_rpb, w_br, w_out, router_w, moe_w1, moe_w3,
           moe_w2, final_g):
    B, T, D = x.shape
    Lc = ctx.shape[1]
    depth = w_in.shape[0]
    rows = T // GRID_W
    rope = _rope_tables(T)
    cvec = jnp.concatenate([c, c_ctx[None], jnp.zeros((8 - B - 1, D), F32)], axis=0)
    x_lat = x.reshape(B * T, D)
    x_ctx = ctx.reshape(B * Lc, D)
    for l in range(depth):
        last = l == depth - 1
        lam_init = 0.8 - 0.6 * math.exp(-0.3 * l)
        mod = _mod_vectors(cvec, w_mod, b_mod, l)
        mx = mod[:B].reshape(B, 6, 1, D)
        mc = mod[B:B + 1].reshape(1, 6, 1, D)
        sh1, sc1, g1, sh2, sc2, g2 = [mx[:, i] for i in range(6)]
        csh1, csc1, cg1, csh2, csc2, cg2 = [mc[:, i] for i in range(6)]
        w_in_b = w_in[l].astype(BF16)
        w_att = w_in_b[:, :ATT_COLS]
        w_rest = jnp.concatenate([w_in_b[:, GATE_START:], w_in_b[:, ATT_COLS:GATE_START]], axis=1)
        p = {'lam': diff_lambda[l], 'subln': diff_subln_g[l], 'pool_w': pool_w[l],
             'pool_scale': pool_scale[l], 'sgu_g': sgu_norm_g[l], 'sgu_w': sgu_w[l], 'sgu_b': sgu_b[l]}
        w_br_b = w_br[l].astype(BF16)
        w_out_b = w_out[l].astype(BF16)
        tables = _na_tables(na_rpb[l], rows)

        if last:
            zc_a = _norm_proj(x_ctx, norm1_g[l], csh1, csc1, w_att[:, :KV_COLS], BF16)
        else:
            zc_a = _norm_proj(x_ctx, norm1_g[l], csh1, csc1, w_att, BF16)
        zc_a = zc_a.reshape(B, Lc, -1)

        za = _norm_proj(x_lat, norm1_g[l], sh1, sc1, w_att, BF16, rope=rope,
                        rope_tiles=(_C_DK // 4, _C_DQ // 4)).reshape(B, T, ATT_COLS)
        zr = _norm_proj(x_lat, norm1_g[l], sh1, sc1, w_rest, F32).reshape(B, T, -1)
        br = _mixers(za, zr, zc_a, T, p, tables, lam_init, latent=True)
        x_lat, h2 = _merge(br, zr.reshape(B * T, -1), w_br_b, w_out_b, x_lat, g1,
                                   norm2_g[l], sh2, sc2, router_w[l])
        x_lat = _moe(x_lat.reshape(B, T, D), h2, moe_w1, moe_w3, moe_w2, l, g2,
                     final_g, final=last).reshape(B * T, D)

        if not last:
            zc_r = _norm_proj(x_ctx, norm1_g[l], csh1, csc1, w_rest, F32).reshape(B, Lc, -1)
            brc = _mixers(zc_a, zc_r, None, Lc, p, None, lam_init, latent=False)
            x_ctx, hc2 = _merge(brc, zc_r.reshape(B * Lc, -1), w_br_b, w_out_b, x_ctx, cg1,
                                     norm2_g[l], csh2, csc2, router_w[l])
            x_ctx = _moe(x_ctx.reshape(B, Lc, D), hc2, moe_w1, moe_w3, moe_w2, l, cg2,
                         final_g, final=False).reshape(B * Lc, D)
    return x_lat.reshape(B, T, D)
```

```python
import functools
import math

import numpy as np
import jax
import jax.numpy as jnp
from jax import lax
from jax.experimental import pallas as pl
from jax.experimental.pallas import tpu as pltpu

F32 = jnp.float32
BF16 = jnp.bfloat16

EPS = 1e-6
GRID_W = 64
LANES = 128
SUBLANES = 8
HALF = 64
ONES_ROWS = 16
N_BRANCH = 4
BRANCH_W = 512
POOL_WINDOWS = (2, 4, 8, 16)
POOL_HALO = 16
SGU_CHUNK = 128
NA_WIN_R = 8
NA_WIN_C = 16
NA_QROWS = 8
NA_KROWS = 16
N_EXPERTS = 16
EC_CAPACITY = 2
ROUTE_BLK = 128
FFN_ROWS = 256
COMBINE_TILE = 256
F32_MANT_BITS = 23
F32_EXP_BIAS = 127
ROPE_BASE = 10000.0
NEG = -1e30
VMEM_LIMIT = 56 * 1024 * 1024

_NT = (((1,), (1,)), ((), ()))


def _params(*sem):
    return pltpu.CompilerParams(dimension_semantics=sem, vmem_limit_bytes=VMEM_LIMIT)


def _pick(n, cands):
    for c in cands:
        if n % c == 0:
            return c
    raise ValueError(f"no tile for {n}")


def _mod_kernel(c_ref, w_ref, b_ref, o_ref):
    c = c_ref[...]
    s = c * jax.nn.sigmoid(c)
    o_ref[...] = jnp.dot(s, w_ref[...], preferred_element_type=F32,
                         precision=lax.Precision.HIGHEST) + b_ref[...]


def _mod_vectors(cvec, w, b, layer):
    L, D, N = w.shape
    tn = _pick(N, (1536, 1024, 512, 128))
    return pl.pallas_call(
        _mod_kernel,
        grid=(N // tn,),
        in_specs=[pl.BlockSpec((8, D), lambda j: (0, 0)),
                  pl.BlockSpec((None, D, tn), lambda j: (layer, 0, j)),
                  pl.BlockSpec((None, 1, tn), lambda j: (layer, 0, j))],
        out_specs=pl.BlockSpec((8, tn), lambda j: (0, j)),
        out_shape=jax.ShapeDtypeStruct((8, N), F32),
        compiler_params=_params("parallel"),
        name="mod_vectors",
    )(cvec, w, b.reshape(L, 1, N))


def _rope_store(acc, cos, sin, o_ref):
    lane = lax.broadcasted_iota(jnp.int32, (1, LANES), 1)
    first = (lane % 32) < 16
    for h in range(acc.shape[1] // LANES):
        a = acc[:, h * LANES:(h + 1) * LANES]
        partner = jnp.where(first, pltpu.roll(a, LANES - 16, 1), pltpu.roll(a, 16, 1))
        o_ref[:, h * LANES:(h + 1) * LANES] = (a * cos + partner * sin).astype(o_ref.dtype)


def _norm_proj_kernel(x_ref, g_ref, sh_ref, sc_ref, w_ref, *rest, rope_tiles):
    if rope_tiles:
        cos_ref, sin_ref, o_ref, h_ref = rest
    else:
        o_ref, h_ref = rest
    j = pl.program_id(1)

    @pl.when(j == 0)
    def _():
        x = x_ref[...]
        y = x * lax.rsqrt(jnp.mean(x * x, axis=-1, keepdims=True) + EPS) * g_ref[...]
        h_ref[...] = (y * (1.0 + sc_ref[...]) + sh_ref[...]).astype(BF16)

    acc = jnp.dot(h_ref[...], w_ref[...], preferred_element_type=F32)
    if rope_tiles:
        is_rope = functools.reduce(jnp.logical_or, [j == t for t in rope_tiles])

        @pl.when(is_rope)
        def _():
            _rope_store(acc, cos_ref[...], sin_ref[...], o_ref)

        @pl.when(jnp.logical_not(is_rope))
        def _():
            o_ref[...] = acc.astype(o_ref.dtype)
    else:
        o_ref[...] = acc.astype(o_ref.dtype)


def _norm_proj(x2, g, shift, scale, w, out_dtype, rope=None, rope_tiles=()):
    M, D = x2.shape
    N = w.shape[1]
    G = shift.shape[0]
    rows_per_group = M // G
    tm = _pick(rows_per_group, (1024, 512, 256))
    tn = 512
    assert N % tn == 0
    in_specs = [pl.BlockSpec((tm, D), lambda i, j: (i, 0)),
                pl.BlockSpec((1, D), lambda i, j: (0, 0)),
                pl.BlockSpec((None, 1, D), lambda i, j: (i * tm // rows_per_group, 0, 0)),
                pl.BlockSpec((None, 1, D), lambda i, j: (i * tm // rows_per_group, 0, 0)),
                pl.BlockSpec((D, tn), lambda i, j: (0, j))]
    args = [x2, g.reshape(1, D), shift, scale, w]
    if rope_tiles:
        cos, sin = rope
        nt = cos.shape[0] // tm
        in_specs += [pl.BlockSpec((tm, LANES), lambda i, j: (i % nt, 0)),
                     pl.BlockSpec((tm, LANES), lambda i, j: (i % nt, 0))]
        args += [cos, sin]
    return pl.pallas_call(
        functools.partial(_norm_proj_kernel, rope_tiles=tuple(rope_tiles)),
        grid=(M // tm, N // tn),
        in_specs=in_specs,
        out_specs=pl.BlockSpec((tm, tn), lambda i, j: (i, j)),
        out_shape=jax.ShapeDtypeStruct((M, N), out_dtype),
        scratch_shapes=[pltpu.VMEM((tm, D), BF16)],
        compiler_params=_params("parallel", "arbitrary"),
        name="norm_proj",
    )(*args)


def _rope_tables(T):
    t = jnp.arange(T)
    row = (t // GRID_W).astype(F32)
    col = (t % GRID_W).astype(F32)
    nf = HALF // 4
    inv = ROPE_BASE ** (-jnp.arange(nf, dtype=F32) / nf)
    ar = row[:, None] * inv
    ac = col[:, None] * inv
    cos = jnp.concatenate([jnp.cos(ar), jnp.cos(ar), jnp.cos(ac), jnp.cos(ac)], axis=-1)
    sin = jnp.concatenate([-jnp.sin(ar), jnp.sin(ar), -jnp.sin(ac), jnp.sin(ac)], axis=-1)
    return jnp.tile(cos, (1, 2)), jnp.tile(sin, (1, 2))


def _flash_kernel(q_ref, k_ref, v_ref, lam_ref, g_ref, o_ref, m_ref, acc_ref, sa_ref, sb_ref, *,
                  mode, tk, lam_init):
    tq = q_ref.shape[0]
    nk = k_ref.shape[0] // tk
    ones = jnp.ones((ONES_ROWS, tk), BF16)
    sub = lax.broadcasted_iota(jnp.int32, (LANES, 1), 0)
    lo = sub < HALF
    if mode == "diff":
        lp = lam_ref[...]
        d1 = jnp.sum(lp[0:1, :] * lp[1:2, :], keepdims=True)
        d2 = jnp.sum(lp[2:3, :] * lp[3:4, :], keepdims=True)
        lam = jnp.exp(d1) - jnp.exp(d2) + lam_init
    for h in range(q_ref.shape[1] // LANES):
        cs = slice(h * LANES, (h + 1) * LANES)
        qh = q_ref[:, cs].astype(F32).T.astype(BF16)
        zero = jnp.zeros_like(qh)
        q2 = jnp.concatenate([jnp.where(lo, qh, zero), jnp.where(lo, zero, qh)], axis=1)
        q2 = q2 * jnp.asarray(HALF ** -0.5, BF16)
        m_ref[...] = jnp.full(m_ref.shape, -jnp.inf, F32)
        acc_ref[...] = jnp.zeros(acc_ref.shape, F32)

        def scores(c, s_ref, q2=q2, cs=cs):
            off = pl.multiple_of(c * tk, tk)
            s_ref[...] = jnp.dot(k_ref[pl.ds(off, tk), cs], q2, preferred_element_type=F32)

        def update(c, s_ref, cs=cs):
            off = pl.multiple_of(c * tk, tk)
            vc = jnp.concatenate([v_ref[cs, pl.ds(off, tk)], ones], axis=0)
            s = s_ref[...]
            m_old = m_ref[...]
            m_new = jnp.maximum(m_old, jnp.max(s, axis=0, keepdims=True))
            alpha = jnp.exp(m_old - m_new)
            p = jnp.exp(s - m_new).astype(BF16)
            acc_ref[...] = alpha * acc_ref[...] + jnp.dot(vc, p, preferred_element_type=F32)
            m_ref[...] = m_new

        s0_ref, s1_ref = (sa_ref, sb_ref) if h % 2 == 0 else (sb_ref, sa_ref)
        scores(0, s0_ref)

        def body(j, carry, s0_ref=s0_ref, s1_ref=s1_ref):
            scores(2 * j + 1, s1_ref)
            update(2 * j, s0_ref)
            scores(2 * j + 2, s0_ref)
            update(2 * j + 1, s1_ref)
            return carry

        lax.fori_loop(0, (nk - 1) // 2, body, 0)
        update(nk - 1, s0_ref)
        a = acc_ref[0:LANES, :] / acc_ref[LANES:LANES + 1, :]
        a0 = a[:, :tq]
        a1 = a[:, tq:]
        if mode == "diff":
            o = a0 - lam * a1
            o = o * lax.rsqrt(jnp.mean(o * o, axis=0, keepdims=True) + EPS)
            o = o * g_ref[...] * (1.0 - lam_init)
        else:
            o = jnp.where(lo, a0, a1)
        o_ref[:, cs] = o.T.astype(o_ref.dtype)


def _flash(z, qblk, k, v, lam_p, g, *, mode, lam_init=0.0):
    B, Tq, _ = z.shape
    Tk, W = k.shape[1:]
    tq = _pick(Tq, (512, 256, 128))
    tk = next(c for c in (768, 512, 384, 256, 128) if Tk % c == 0 and (Tk // c) % 2 == 1)
    return pl.pallas_call(
        functools.partial(_flash_kernel, mode=mode, tk=tk, lam_init=lam_init),
        grid=(B, Tq // tq),
        in_specs=[pl.BlockSpec((None, tq, W), lambda b, i: (b, i, qblk)),
                  pl.BlockSpec((None, Tk, W), lambda b, i: (b, 0, 0)),
                  pl.BlockSpec((None, W, Tk), lambda b, i: (b, 0, 0)),
                  pl.BlockSpec(lam_p.shape, lambda b, i: (0, 0)),
                  pl.BlockSpec((LANES, 1), lambda b, i: (0, 0))],
        out_specs=pl.BlockSpec((None, tq, W), lambda b, i: (b, i, 0)),
        out_shape=jax.ShapeDtypeStruct((B, Tq, W), BF16),
        scratch_shapes=[pltpu.VMEM((1, 2 * tq), F32), pltpu.VMEM((LANES + ONES_ROWS, 2 * tq), F32),
                        pltpu.VMEM((tk, 2 * tq), F32), pltpu.VMEM((tk, 2 * tq), F32)],
        compiler_params=_params("parallel", "parallel"),
        name="flash_" + mode,
    )(z, k, jnp.swapaxes(v, 1, 2), lam_p, g.reshape(LANES, 1))


def _na_kernel(q_ref, k_ref, v_ref, kc_ref, vc_ref, u_ref, o_ref, b_ref, *, rows):
    rb = pl.program_id(2)
    nrb = pl.num_programs(2)
    nq = NA_QROWS * GRID_W
    nkk = NA_KROWS * GRID_W
    offs = _na_row_offsets(rows)
    lane1 = lax.broadcasted_iota(jnp.int32, (1, LANES), 1)

    def build(variant):
        for hh in range(2):
            for rl in range(NA_QROWS):
                for kp in range(NA_KROWS // 2):
                    a = int(offs[variant, rl, 2 * kp])
                    b2 = int(offs[variant, rl, 2 * kp + 1])
                    tile = u_ref[hh, a] if a == b2 else jnp.where(lane1 < GRID_W, u_ref[hh, a], u_ref[hh, b2])
                    b_ref[hh, rl * GRID_W:(rl + 1) * GRID_W, kp * LANES:(kp + 1) * LANES] = tile

    pl.when(rb == 0)(functools.partial(build, 0))
    if rows // NA_QROWS > 2:
        pl.when(rb == 1)(functools.partial(build, 1))
    pl.when(rb == nrb - 1)(functools.partial(build, 2))
    k0 = jnp.clip(rb * NA_QROWS - NA_WIN_R // 2, 0, rows - NA_KROWS) * GRID_W
    k0 = pl.multiple_of(k0, 4 * GRID_W)
    kl = k_ref[pl.ds(k0, nkk), :]
    vl = v_ref[pl.ds(k0, nkk), :]
    kc = kc_ref[...]
    vc = vc_ref[...]
    q = q_ref[...]
    zero = jnp.zeros_like(q)
    lane = lax.broadcasted_iota(jnp.int32, (1, LANES), 1)
    lo = lane < HALF
    scale = jnp.asarray(HALF ** -0.5, BF16)
    outs = []
    for hh in range(2):
        qm = (jnp.where(lo, q, zero) if hh == 0 else jnp.where(lo, zero, q)) * scale
        s_loc = lax.dot_general(qm, kl, _NT, preferred_element_type=F32) + b_ref[hh]
        s_ctx = lax.dot_general(qm, kc, _NT, preferred_element_type=F32)
        m = jnp.maximum(jnp.max(s_loc, axis=-1, keepdims=True), jnp.max(s_ctx, axis=-1, keepdims=True))
        p_loc = jnp.exp(s_loc - m)
        p_ctx = jnp.exp(s_ctx - m)
        l = jnp.sum(p_loc, axis=-1, keepdims=True) + jnp.sum(p_ctx, axis=-1, keepdims=True)
        o = jnp.dot(p_loc.astype(BF16), vl, preferred_element_type=F32) \
            + jnp.dot(p_ctx.astype(BF16), vc, preferred_element_type=F32)
        outs.append(o / l)
    del nq
    o_ref[...] = jnp.where(lo, outs[0], outs[1]).astype(o_ref.dtype)


def _na_tables(rpb, rows):
    H = rpb.shape[0]
    c = np.arange(GRID_W)
    cstart = np.clip(c - NA_WIN_C // 2, 0, GRID_W - NA_WIN_C)
    kcol = np.arange(GRID_W)
    col_ok = (kcol[None, :] >= cstart[:, None]) & (kcol[None, :] < cstart[:, None] + NA_WIN_C)
    nc = 2 * NA_WIN_C - 1
    dc = np.where(col_ok, kcol[None, :] - c[:, None] + NA_WIN_C - 1, nc)
    cm = np.arange(nc + 1)[None, None, :] == dc[:, :, None]
    rpb_ext = jnp.pad(rpb, ((0, 0), (0, 1), (0, 1)), constant_values=NEG)
    u = jnp.einsum('hij,cqj->hicq', rpb_ext, jnp.asarray(cm, F32), precision=lax.Precision.HIGHEST)
    return jnp.tile(u, (1, 1, 1, 2))


def _na_row_offsets(rows):
    nr = 2 * NA_WIN_R - 1
    out = []
    for q0, k0, clamp in ((0, 0, True), (NA_QROWS, NA_QROWS - NA_WIN_R // 2, False),
                          (rows - NA_QROWS, rows - NA_KROWS, True)):
        r = q0 + np.arange(NA_QROWS)
        kr = k0 + np.arange(NA_KROWS)
        r0 = r - NA_WIN_R // 2
        if clamp:
            r0 = np.clip(r0, 0, rows - NA_WIN_R)
        row_ok = (kr[None, :] >= r0[:, None]) & (kr[None, :] < r0[:, None] + NA_WIN_R)
        out.append(np.where(row_ok, kr[None, :] - r[:, None] + NA_WIN_R - 1, nr))
    return np.stack(out)


def _na_attn(z, zc, tables, *, qcol, kcol, vcol):
    B, T, _ = z.shape
    Lc = zc.shape[1]
    rows = T // GRID_W
    nrb = rows // NA_QROWS
    nq = NA_QROWS * GRID_W
    nkk = NA_KROWS * GRID_W
    nhp = BRANCH_W // LANES
    assert nrb >= 2
    return pl.pallas_call(
        functools.partial(_na_kernel, rows=rows),
        grid=(nhp, B, nrb),
        in_specs=[pl.BlockSpec((None, nq, LANES), lambda h, b, r: (b, r, qcol + h)),
                  pl.BlockSpec((None, T, LANES), lambda h, b, r: (b, 0, kcol + h)),
                  pl.BlockSpec((None, T, LANES), lambda h, b, r: (b, 0, vcol + h)),
                  pl.BlockSpec((None, Lc, LANES), lambda h, b, r: (b, 0, kcol + h)),
                  pl.BlockSpec((None, Lc, LANES), lambda h, b, r: (b, 0, vcol + h)),
                  pl.BlockSpec((2, 2 * NA_WIN_R, GRID_W, LANES), lambda h, b, r: (h, 0, 0, 0))],
        out_specs=pl.BlockSpec((None, nq, LANES), lambda h, b, r: (b, r, h)),
        out_shape=jax.ShapeDtypeStruct((B, T, BRANCH_W), BF16),
        scratch_shapes=[pltpu.VMEM((2, nq, nkk), F32)],
        compiler_params=_params("arbitrary", "arbitrary", "arbitrary"),
        name="na_attn",
    )(z, z, z, zc, zc, tables)


def _pool_kernel(prev_ref, cur_ref, next_ref, w_ref, sc_ref, o_ref, *, T):
    i = pl.program_id(1)
    tt = cur_ref.shape[0]
    E = tt + 2 * POOL_HALO
    ext = jnp.concatenate([prev_ref[...], cur_ref[...], next_ref[...]], axis=0).astype(F32)
    gi = i * tt - POOL_HALO + lax.broadcasted_iota(jnp.int32, (E, 1), 0)
    ext = jnp.where((gi >= 0) & (gi < T), ext, 0.0)
    t = i * tt + lax.broadcasted_iota(jnp.int32, (tt, 1), 0)
    for g, w in enumerate(POOL_WINDOWS):
        cs = slice(g * LANES, (g + 1) * LANES)
        zg = ext[:, cs]
        s = zg + pltpu.roll(zg, 1, 0)
        half = 1
        while 2 * half < w:
            s = pltpu.roll(s, half, 0) + pltpu.roll(s, E - half, 0)
            half *= 2
        cnt = (jnp.minimum(t + w // 2, T) - jnp.maximum(t - w // 2, 0)).astype(F32)
        c0 = slice(POOL_HALO, POOL_HALO + tt)
        d = s[c0] / cnt - zg[c0]
        y = jnp.dot(d.astype(BF16), w_ref[g].astype(BF16), preferred_element_type=F32)
        o_ref[:, cs] = (y * sc_ref[:, cs]).astype(o_ref.dtype)


def _pool(z, w_pool, scale, *, col):
    B, T, _ = z.shape
    tt = _pick(T, (512, 256))
    hb = tt // POOL_HALO
    nh = T // POOL_HALO
    return pl.pallas_call(
        functools.partial(_pool_kernel, T=T),
        grid=(B, T // tt),
        in_specs=[pl.BlockSpec((None, POOL_HALO, BRANCH_W), lambda b, i: (b, jnp.maximum(i * hb - 1, 0), col)),
                  pl.BlockSpec((None, tt, BRANCH_W), lambda b, i: (b, i, col)),
                  pl.BlockSpec((None, POOL_HALO, BRANCH_W), lambda b, i: (b, jnp.minimum((i + 1) * hb, nh - 1), col)),
                  pl.BlockSpec(w_pool.shape, lambda b, i: (0, 0, 0)),
                  pl.BlockSpec((1, BRANCH_W), lambda b, i: (0, 0))],
        out_specs=pl.BlockSpec((None, tt, BRANCH_W), lambda b, i: (b, i, 0)),
        out_shape=jax.ShapeDtypeStruct((B, T, BRANCH_W), BF16),
        compiler_params=_params("parallel", "parallel"),
        name="pool_mixer",
    )(z, z, z, w_pool, scale.reshape(1, BRANCH_W))


def _sgu_kernel(u_ref, v_ref, g_ref, w_ref, bt_ref, o_ref):
    ts = u_ref.shape[0]
    u = jax.nn.gelu(u_ref[...].astype(F32))
    v = jax.nn.gelu(v_ref[...].astype(F32))
    vc = v - jnp.mean(v, axis=-1, keepdims=True)
    v = vc * lax.rsqrt(jnp.mean(vc * vc, axis=-1, keepdims=True) + EPS) * g_ref[...]
    vb = v.astype(BF16)
    for g in range(BRANCH_W // LANES):
        cs = slice(g * LANES, (g + 1) * LANES)
        wg = w_ref[g].astype(BF16)
        bg = bt_ref[:, g:g + 1]
        for ck in range(ts // SGU_CHUNK):
            rs = slice(ck * SGU_CHUNK, (ck + 1) * SGU_CHUNK)
            f = jnp.dot(wg, vb[rs, cs], preferred_element_type=F32) + bg
            o_ref[rs, cs] = (u[rs, cs] * f).astype(o_ref.dtype)


def _sgu(z, g, w_s, b, *, ucol, vcol):
    B, T, _ = z.shape
    ts = _pick(T, (512, 256, 128))
    return pl.pallas_call(
        _sgu_kernel,
        grid=(B, T // ts),
        in_specs=[pl.BlockSpec((None, ts, BRANCH_W), lambda b_, i: (b_, i, ucol)),
                  pl.BlockSpec((None, ts, BRANCH_W), lambda b_, i: (b_, i, vcol)),
                  pl.BlockSpec((1, BRANCH_W), lambda b_, i: (0, 0)),
                  pl.BlockSpec(w_s.shape, lambda b_, i: (0, 0, 0)),
                  pl.BlockSpec((SGU_CHUNK, BRANCH_W // LANES), lambda b_, i: (0, 0))],
        out_specs=pl.BlockSpec((None, ts, BRANCH_W), lambda b_, i: (b_, i, 0)),
        out_shape=jax.ShapeDtypeStruct((B, T, BRANCH_W), BF16),
        compiler_params=_params("parallel", "parallel"),
        name="sgu_mixer",
    )(z, z, g.reshape(1, BRANCH_W), w_s, b.T)


def _merge_kernel(p_ref, d_ref, s_ref, n_ref, gz_ref, wbr_ref, wout_ref, x_ref, g1_ref,
                  n2_ref, sh_ref, sc_ref, rw_ref, xo_ref, h_ref, *, n_experts):
    D = x_ref.shape[1]
    acc = None
    for i, br in enumerate((p_ref, d_ref, s_ref, n_ref)):
        proj = jnp.dot(br[...], wbr_ref[i], preferred_element_type=F32)
        t = jax.nn.sigmoid(gz_ref[:, i * D:(i + 1) * D].astype(F32)) * proj
        acc = t if acc is None else acc + t
    mix = jnp.dot(acc.astype(BF16), wout_ref[...], preferred_element_type=F32)
    x = x_ref[...] + g1_ref[...] * mix
    xo_ref[...] = x
    y = x * lax.rsqrt(jnp.mean(x * x, axis=-1, keepdims=True) + EPS) * n2_ref[...]
    h = y * (1.0 + sc_ref[...]) + sh_ref[...]
    h_ref[:, 0:D] = h
    rw = rw_ref[...]
    h_hi = h.astype(BF16)
    h_lo = (h - h_hi.astype(F32)).astype(BF16)
    w_hi = rw.astype(BF16)
    w_lo = (rw - w_hi.astype(F32)).astype(BF16)
    lg = (jnp.dot(h_hi, w_hi, preferred_element_type=F32)
          + jnp.dot(h_lo, w_hi, preferred_element_type=F32)
          + jnp.dot(h_hi, w_lo, preferred_element_type=F32))
    lane = lax.broadcasted_iota(jnp.int32, (1, LANES), 1)
    lg = jnp.where(lane < n_experts, lg, -jnp.inf)
    ex = jnp.exp(lg - jnp.max(lg, axis=-1, keepdims=True))
    h_ref[:, D:D + LANES] = ex / jnp.sum(ex, axis=-1, keepdims=True)


def _merge(branches, zr, w_br, w_out, x2, g1, n2g, sh2, sc2, router_w):
    M, D = x2.shape
    G = g1.shape[0]
    rpg = M // G
    tm = _pick(rpg, (256,))
    E = router_w.shape[1]
    router_w = jnp.pad(router_w, ((0, 0), (0, LANES - E)))
    gb = 0
    grp = lambda i: (i * tm // rpg, 0, 0)
    row = lambda i: (i, 0)
    return pl.pallas_call(
        functools.partial(_merge_kernel, n_experts=E),
        grid=(M // tm,),
        in_specs=[pl.BlockSpec((tm, BRANCH_W), row)] * 4 + [
            pl.BlockSpec((tm, N_BRANCH * D), lambda i: (i, gb)),
            pl.BlockSpec(w_br.shape, lambda i: (0, 0, 0)),
            pl.BlockSpec(w_out.shape, lambda i: (0, 0)),
            pl.BlockSpec((tm, D), row),
            pl.BlockSpec((None, 1, D), grp),
            pl.BlockSpec((1, D), lambda i: (0, 0)),
            pl.BlockSpec((None, 1, D), grp),
            pl.BlockSpec((None, 1, D), grp),
            pl.BlockSpec(router_w.shape, lambda i: (0, 0))],
        out_specs=[pl.BlockSpec((tm, D), row), pl.BlockSpec((tm, D + LANES), row)],
        out_shape=[jax.ShapeDtypeStruct((M, D), F32), jax.ShapeDtypeStruct((M, D + LANES), F32)],
        compiler_params=_params("parallel"),
        name="merge_out",
    )(*branches, zr, w_br, w_out, x2, g1, n2g.reshape(1, D), sh2, sc2, router_w)


def _ffn_kernel(rows_ref, h_hbm, w1_ref, w3_ref, w2_ref, y_ref, w1b, w3b, w2b, xbuf, sem, *, B):
    e = pl.program_id(0)
    b = pl.program_id(1)
    C = xbuf.shape[1]
    D = w1_ref.shape[0]
    step = e * B + b
    n_exp = pl.num_programs(0)
    nsteps = n_exp * B
    slot = step % 2

    def start_row(base, slot_, j):
        pltpu.make_async_copy(h_hbm.at[pl.ds(rows_ref[base + j], 1)], xbuf.at[slot_, pl.ds(j, 1)],
                              sem.at[slot_]).start()

    def wait_slot(slot_):
        pltpu.make_async_copy(h_hbm.at[pl.ds(0, C)], xbuf.at[slot_], sem.at[slot_]).wait()

    @pl.when(step == 0)
    def _():
        def issue(j, carry):
            start_row((b * n_exp + e) * C, slot, j)
            return carry

        lax.fori_loop(0, C, issue, 0, unroll=8)

    @pl.when(b == 0)
    def _():
        w1b[...] = w1_ref[...].astype(BF16)
        w3b[...] = w3_ref[...].astype(BF16)
        w2b[...] = w2_ref[...].astype(BF16)

    last = step + 1 >= nsteps
    nxt_b = jnp.where(last, b, (b + 1) % B)
    nxt_e = jnp.where(last, e, e + (b + 1) // B)
    nxt_base = (nxt_b * n_exp + nxt_e) * C
    wait_slot(slot)
    lane = lax.broadcasted_iota(jnp.int32, (1, LANES), 1)
    rc = min(C, FFN_ROWS)
    for r in range(C // rc):
        for j in range(r * rc, (r + 1) * rc):
            start_row(nxt_base, 1 - slot, j)
        rs = pl.ds(r * rc, rc)
        xs = xbuf[slot, rs, 0:D].astype(BF16)
        gate = jnp.sum(jnp.where(lane == e, xbuf[slot, rs, D:D + LANES], 0.0), axis=-1, keepdims=True)
        a = jnp.dot(xs, w1b[...], preferred_element_type=F32)
        g = jnp.dot(xs, w3b[...], preferred_element_type=F32)
        hid = (a * jax.nn.sigmoid(a) * g).astype(BF16)
        y = jnp.dot(hid, w2b[...], preferred_element_type=F32) * gate
        y_ref[rs, :] = y.astype(y_ref.dtype)

    @pl.when(last)
    def _():
        wait_slot(1 - slot)


def _ffn(hext, idx, w1, w3, w2, layer, T):
    B, E, C = idx.shape
    D, Fd = w1.shape[2:]
    grid_spec = pltpu.PrefetchScalarGridSpec(
        num_scalar_prefetch=1,
        grid=(E, B),
        in_specs=[pl.BlockSpec(memory_space=pl.ANY),
                  pl.BlockSpec((None, None, D, Fd), lambda e, b, ix: (layer, e, 0, 0)),
                  pl.BlockSpec((None, None, D, Fd), lambda e, b, ix: (layer, e, 0, 0)),
                  pl.BlockSpec((None, None, Fd, D), lambda e, b, ix: (layer, e, 0, 0))],
        out_specs=pl.BlockSpec((None, None, C, D), lambda e, b, ix: (b, e, 0, 0)),
        scratch_shapes=[pltpu.VMEM((D, Fd), BF16), pltpu.VMEM((D, Fd), BF16), pltpu.VMEM((Fd, D), BF16),
                        pltpu.VMEM((2, C, D + LANES), F32), pltpu.SemaphoreType.DMA((2,))],
    )
    return pl.pallas_call(
        functools.partial(_ffn_kernel, B=B),
        grid_spec=grid_spec,
        out_shape=jax.ShapeDtypeStruct((B, E, C, D), BF16),
        compiler_params=_params("arbitrary", "arbitrary"),
        name="expert_ffn",
    )((idx + jnp.arange(B, dtype=jnp.int32)[:, None, None] * T).reshape(-1), hext, w1, w3, w2)


def _combine_kernel(starts_ref, x_ref, g2_ref, fg_ref, idx_ref, *rest, wb, nwin, final, spt):
    ys = rest[:-2]
    o_ref, acc_ref = rest[-2:]
    b = pl.program_id(0)
    t = pl.program_id(1)
    tt = x_ref.shape[0]
    E = idx_ref.shape[0]
    C = idx_ref.shape[1]
    tok = t * tt + lax.broadcasted_iota(jnp.int32, (tt, 1), 0)

    def window_onehot(e, wk):
        if wb == C:
            ids = idx_ref[e:e + 1, :]
        else:
            ids = idx_ref[e:e + 1, pl.ds(pl.multiple_of(wk * wb, wb), wb)]
        return jnp.where(ids == tok, 1.0, 0.0).astype(BF16)

    def window_dot(e, k, wk):
        return jnp.dot(window_onehot(e, wk), ys[e * nwin + k][...], preferred_element_type=F32)

    onehots = []
    for e in range(E):
        w0 = jnp.minimum(starts_ref[b, t * spt, e] // wb, C // wb - 1)
        onehots.append(window_onehot(e, w0))
    if wb % LANES == 0:
        acc_ref[...] = jnp.dot(jnp.concatenate(onehots, axis=1),
                               jnp.concatenate([ys[e * nwin][...] for e in range(E)], axis=0),
                               preferred_element_type=F32)
    else:
        acc = None
        for e in range(E):
            d = jnp.dot(onehots[e], ys[e * nwin][...], preferred_element_type=F32)
            acc = d if acc is None else acc + d
        acc_ref[...] = acc
    if nwin == 2:
        for e in range(E):
            s0 = starts_ref[b, t * spt, e]
            s1 = starts_ref[b, (t + 1) * spt, e]
            w0 = s0 // wb

            @pl.when((s1 > s0) & ((s1 - 1) // wb > w0))
            def _(e=e, w0=w0):
                acc_ref[...] += window_dot(e, 1, w0 + 1)

    x = x_ref[...] + g2_ref[...] * acc_ref[...]
    if final:
        x = x * lax.rsqrt(jnp.mean(x * x, axis=-1, keepdims=True) + EPS) * fg_ref[...]
    o_ref[...] = x


def _combine(x3, y, idx, starts, g2, final_g, *, final):
    B, T, D = x3.shape
    E, C = idx.shape[1:]
    if C >= COMBINE_TILE and T % COMBINE_TILE == 0:
        wb, nwin, tt = COMBINE_TILE, 2, COMBINE_TILE
    elif C >= LANES:
        wb, nwin, tt = LANES, 2, LANES
    else:
        wb, nwin, tt = C, 1, T
    nt = T // tt
    spt = tt // ROUTE_BLK
    G = g2.shape[0]

    def ymap(e, k):
        return lambda b, t, st: (b, e, jnp.minimum(st[b, t * spt, e] // wb + k, C // wb - 1), 0)

    y_specs = [pl.BlockSpec((None, None, wb, D), ymap(e, k)) for e in range(E) for k in range(nwin)]
    grid_spec = pltpu.PrefetchScalarGridSpec(
        num_scalar_prefetch=1,
        grid=(B, nt),
        in_specs=[pl.BlockSpec((None, tt, D), lambda b, t, st: (b, t, 0)),
                  pl.BlockSpec((None, 1, D), lambda b, t, st: (b * G // B, 0, 0)),
                  pl.BlockSpec((1, D), lambda b, t, st: (0, 0)),
                  pl.BlockSpec((None, E, C), lambda b, t, st: (b, 0, 0))] + y_specs,
        out_specs=pl.BlockSpec((None, tt, D), lambda b, t, st: (b, t, 0)),
        scratch_shapes=[pltpu.VMEM((tt, D), F32)],
    )
    return pl.pallas_call(
        functools.partial(_combine_kernel, wb=wb, nwin=nwin, final=final, spt=spt),
        grid_spec=grid_spec,
        out_shape=jax.ShapeDtypeStruct((B, T, D), F32),
        compiler_params=_params("parallel", "arbitrary"),
        name="moe_combine",
    )(starts, x3, g2, final_g.reshape(1, D), idx, *([y] * (E * nwin)))


def _route_kernel(aff_ref, idx_ref, starts_ref, cs_ref, *, C, E):
    T = aff_ref.shape[0]
    nb = T // ROUTE_BLK

    def count(mask):
        ones = jnp.where(mask, 1.0, 0.0)
        ways = 16 if T % (16 * SUBLANES) == 0 else 1
        part = jnp.sum(ones.reshape(ways, T // ways, LANES), axis=1)
        return jnp.sum(part, axis=0, keepdims=True)

    def value_of(pattern):
        ex = jnp.right_shift(pattern, F32_MANT_BITS)
        frac = (pattern & ((1 << F32_MANT_BITS) - 1)).astype(F32) * (2.0 ** -F32_MANT_BITS)
        n = F32_EXP_BIAS - ex
        p = jnp.ones(pattern.shape, F32)
        for i in range(7):
            p = jnp.where((jnp.right_shift(n, i) & 1) == 1, p * (2.0 ** -(2 ** i)), p)
        return jnp.where(ex == 0, 0.0, p * (1.0 + frac))

    def bit_body(i, prefix):
        cand = prefix | jnp.left_shift(jnp.int32(1), 29 - i)
        cnt = count(aff_ref[...] >= value_of(cand))
        return jnp.where(cnt >= C, cand, prefix)

    vstar = value_of(lax.fori_loop(0, 30, bit_body, jnp.zeros((1, LANES), jnp.int32)))
    n_gt = count(aff_ref[...] > vstar)
    n_tie = C - n_gt

    r = lax.broadcasted_iota(jnp.int32, (ROUTE_BLK, ROUTE_BLK), 0)
    c = lax.broadcasted_iota(jnp.int32, (ROUTE_BLK, ROUTE_BLK), 1)
    lower_strict = jnp.where(c < r, 1.0, 0.0).astype(BF16)
    lower_incl = jnp.where(c <= r, 1.0, 0.0).astype(BF16)

    def blk(k, carry):
        ties_before, chosen_before = carry
        off = pl.multiple_of(k * ROUTE_BLK, ROUTE_BLK)
        a = aff_ref[pl.ds(off, ROUTE_BLK), :]
        gt = a > vstar
        eq = jnp.where(a == vstar, 1.0, 0.0)
        rank = jnp.dot(lower_strict, eq.astype(BF16), preferred_element_type=F32) + ties_before
        sel = jnp.where(gt | ((eq > 0.0) & (rank < n_tie)), 1.0, 0.0)
        csum = jnp.dot(lower_incl, sel.astype(BF16), preferred_element_type=F32) + chosen_before
        cs_ref[pl.ds(off, ROUTE_BLK), :] = csum
        starts_ref[k] = jnp.broadcast_to(chosen_before.astype(jnp.int32), (SUBLANES, LANES))
        return (ties_before + jnp.sum(eq, axis=0, keepdims=True),
                chosen_before + jnp.sum(sel, axis=0, keepdims=True))

    zero = jnp.zeros((1, LANES), F32)
    _, total = lax.fori_loop(0, nb, blk, (zero, zero))
    starts_ref[nb] = jnp.broadcast_to(total.astype(jnp.int32), (SUBLANES, LANES))

    lane_f = lax.broadcasted_iota(jnp.int32, (1, LANES), 1).astype(F32)

    def expert(e, carry):
        pick = jnp.where(r == e, 1.0, 0.0)
        col = jnp.dot(cs_ref[...], pick, preferred_element_type=F32,
                      precision=lax.Precision.HIGHEST)
        for jt in range(-(-C // LANES)):
            cnt = count(col <= lane_f + jt * LANES)
            w = min(LANES, C - jt * LANES)
            idx_ref[e, :, jt * LANES:jt * LANES + w] = jnp.broadcast_to(
                cnt[:, 0:w].astype(jnp.int32), (SUBLANES, w))
        return carry

    lax.fori_loop(0, E, expert, 0)


def _route(hext3, E, C):
    B, T, W = hext3.shape
    nb = T // ROUTE_BLK
    idx8, starts8 = pl.pallas_call(
        functools.partial(_route_kernel, C=C, E=E),
        grid=(B,),
        in_specs=[pl.BlockSpec((None, T, LANES), lambda b: (b, 0, W // LANES - 1))],
        out_specs=[pl.BlockSpec((None, E, SUBLANES, C), lambda b: (b, 0, 0, 0)),
                   pl.BlockSpec((None, nb + 1, SUBLANES, LANES), lambda b: (b, 0, 0, 0))],
        out_shape=[jax.ShapeDtypeStruct((B, E, SUBLANES, C), jnp.int32),
                   jax.ShapeDtypeStruct((B, nb + 1, SUBLANES, LANES), jnp.int32)],
        scratch_shapes=[pltpu.VMEM((T, LANES), F32)],
        compiler_params=_params("parallel"),
        name="ec_route",
    )(hext3)
    return idx8[:, :, 0, :], starts8[:, :, 0, :E]


def _moe(x3, hext, w1, w3, w2, layer, g2, final_g, *, final):
    B, T, D = x3.shape
    E = w1.shape[1]
    C = EC_CAPACITY * T // E
    idx, starts = _route(hext.reshape(B, T, D + LANES), E, C)
    y = _ffn(hext, idx, w1, w3, w2, layer, T)
    return _combine(x3, y, idx, starts, g2, final_g, final=final)


_C_DK, _C_DV, _C_NK, _C_NV, _C_DQ, _C_NQ = 0, 4, 8, 12, 16, 20
ATT_COLS = 6 * BRANCH_W
KV_COLS = 4 * BRANCH_W
GATE_START = 9 * BRANCH_W
_R_POOL, _R_SU, _R_SV = 8, 9, 10


def _mixers(za, zr, zc_kv, T, p, tables, lam_init, *, latent):
    B = za.shape[0]
    sl = lambda a, c: a[:, :, c * LANES:c * LANES + BRANCH_W]
    if latent:
        k_all = jnp.concatenate([sl(zc_kv, _C_DK), sl(za, _C_DK)], axis=1)
        v_all = jnp.concatenate([sl(zc_kv, _C_DV), sl(za, _C_DV)], axis=1)
        diff = _flash(za, _C_DQ // 4, k_all, v_all, p['lam'], p['subln'], mode="diff", lam_init=lam_init)
        na = _na_attn(za, zc_kv, tables, qcol=_C_NQ, kcol=_C_NK, vcol=_C_NV)
    else:
        diff = _flash(za, _C_DQ // 4, sl(za, _C_DK), sl(za, _C_DV), p['lam'], p['subln'],
                      mode="diff", lam_init=lam_init)
        na = _flash(za, _C_NQ // 4, sl(za, _C_NK), sl(za, _C_NV), p['lam'], p['subln'], mode="dense")
    pool = _pool(zr, p['pool_w'], p['pool_scale'], col=_R_POOL)
    sgu = _sgu(zr, p['sgu_g'], p['sgu_w'], p['sgu_b'], ucol=_R_SU, vcol=_R_SV)
    M = B * T
    f2 = lambda a: a.reshape(M, a.shape[-1])
    return [f2(pool), f2(diff), f2(sgu), f2(na)]


def kernel(x, c, ctx, c_ctx, w_mod, b_mod, norm1_g, norm2_g, w_in, pool_w, pool_scale, diff_lambda,
           diff_subln_g, sgu_norm_g, sgu_w, sgu_b, na_rpb, w_br, w_out, router_w, moe_w1, moe_w3,
           moe_w2, final_g):
    B, T, D = x.shape
    Lc = ctx.shape[1]
    depth = w_in.shape[0]
    rows = T // GRID_W
    rope = _rope_tables(T)
    cvec = jnp.concatenate([c, c_ctx[None], jnp.zeros((8 - B - 1, D), F32)], axis=0)
    x_lat = x.reshape(B * T, D)
    x_ctx = ctx.reshape(B * Lc, D)
    for l in range(depth):
        last = l == depth - 1
        lam_init = 0.8 - 0.6 * math.exp(-0.3 * l)
        mod = _mod_vectors(cvec, w_mod, b_mod, l)
        mx = mod[:B].reshape(B, 6, 1, D)
        mc = mod[B:B + 1].reshape(1, 6, 1, D)
        sh1, sc1, g1, sh2, sc2, g2 = [mx[:, i] for i in range(6)]
        csh1, csc1, cg1, csh2, csc2, cg2 = [mc[:, i] for i in range(6)]
        w_in_b = w_in[l].astype(BF16)
        w_att = w_in_b[:, :ATT_COLS]
        w_rest = jnp.concatenate([w_in_b[:, GATE_START:], w_in_b[:, ATT_COLS:GATE_START]], axis=1)
        p = {'lam': diff_lambda[l], 'subln': diff_subln_g[l], 'pool_w': pool_w[l],
             'pool_scale': pool_scale[l], 'sgu_g': sgu_norm_g[l], 'sgu_w': sgu_w[l], 'sgu_b': sgu_b[l]}
        w_br_b = w_br[l].astype(BF16)
        w_out_b = w_out[l].astype(BF16)
        tables = _na_tables(na_rpb[l], rows)

        if last:
            zc_a = _norm_proj(x_ctx, norm1_g[l], csh1, csc1, w_att[:, :KV_COLS], BF16)
        else:
            zc_a = _norm_proj(x_ctx, norm1_g[l], csh1, csc1, w_att, BF16)
        zc_a = zc_a.reshape(B, Lc, -1)

        za = _norm_proj(x_lat, norm1_g[l], sh1, sc1, w_att, BF16, rope=rope,
                        rope_tiles=(_C_DK // 4, _C_DQ // 4)).reshape(B, T, ATT_COLS)
        zr = _norm_proj(x_lat, norm1_g[l], sh1, sc1, w_rest, F32).reshape(B, T, -1)
        br = _mixers(za, zr, zc_a, T, p, tables, lam_init, latent=True)
        x_lat, h2 = _merge(br, zr.reshape(B * T, -1), w_br_b, w_out_b, x_lat, g1,
                                   norm2_g[l], sh2, sc2, router_w[l])
        x_lat = _moe(x_lat.reshape(B, T, D), h2, moe_w1, moe_w3, moe_w2, l, g2,
                     final_g, final=last).reshape(B * T, D)

        if not last:
            zc_r = _norm_proj(x_ctx, norm1_g[l], csh1, csc1, w_rest, F32).reshape(B, Lc, -1)
            brc = _mixers(zc_a, zc_r, None, Lc, p, None, lam_init, latent=False)
            x_ctx, hc2 = _merge(brc, zc_r.reshape(B * Lc, -1), w_br_b, w_out_b, x_ctx, cg1,
                                     norm2_g[l], csh2, csc2, router_w[l])
            x_ctx = _moe(x_ctx.reshape(B, Lc, D), hc2, moe_w1, moe_w3, moe_w2, l, cg2,
                         final_g, final=False).reshape(B * Lc, D)
    return x_lat.reshape(B, T, D)
```

```python
import functools
import math

import numpy as np
import jax
import jax.numpy as jnp
from jax import lax
from jax.experimental import pallas as pl
from jax.experimental.pallas import tpu as pltpu

F32 = jnp.float32
BF16 = jnp.bfloat16

EPS = 1e-6
GRID_W = 64
LANES = 128
SUBLANES = 8
HALF = 64
ONES_ROWS = 16
N_BRANCH = 4
BRANCH_W = 512
POOL_WINDOWS = (2, 4, 8, 16)
POOL_HALO = 16
SGU_CHUNK = 128
NA_WIN_R = 8
NA_WIN_C = 16
NA_QROWS = 8
NA_KROWS = 16
N_EXPERTS = 16
EC_CAPACITY = 2
ROUTE_BLK = 128
FFN_ROWS = 256
COMBINE_TILE = 256
F32_MANT_BITS = 23
F32_EXP_BIAS = 127
ROPE_BASE = 10000.0
NEG = -1e30
VMEM_LIMIT = 56 * 1024 * 1024

_NT = (((1,), (1,)), ((), ()))


def _params(*sem):
    return pltpu.CompilerParams(dimension_semantics=sem, vmem_limit_bytes=VMEM_LIMIT)


def _pick(n, cands):
    for c in cands:
        if n % c == 0:
            return c
    raise ValueError(f"no tile for {n}")


def _mod_kernel(c_ref, w_ref, b_ref, o_ref):
    c = c_ref[...]
    s = c * jax.nn.sigmoid(c)
    o_ref[...] = jnp.dot(s, w_ref[...], preferred_element_type=F32,
                         precision=lax.Precision.HIGHEST) + b_ref[...]


def _mod_vectors(cvec, w, b, layer):
    L, D, N = w.shape
    tn = _pick(N, (1536, 1024, 512, 128))
    return pl.pallas_call(
        _mod_kernel,
        grid=(N // tn,),
        in_specs=[pl.BlockSpec((8, D), lambda j: (0, 0)),
                  pl.BlockSpec((None, D, tn), lambda j: (layer, 0, j)),
                  pl.BlockSpec((None, 1, tn), lambda j: (layer, 0, j))],
        out_specs=pl.BlockSpec((8, tn), lambda j: (0, j)),
        out_shape=jax.ShapeDtypeStruct((8, N), F32),
        compiler_params=_params("parallel"),
        name="mod_vectors",
    )(cvec, w, b.reshape(L, 1, N))


def _rope_store(acc, cos, sin, o_ref, col0):
    lane = lax.broadcasted_iota(jnp.int32, (1, LANES), 1)
    first = (lane % 32) < 16
    for h in range(BRANCH_W // LANES):
        cs = slice(col0 + h * LANES, col0 + (h + 1) * LANES)
        a = acc[:, cs]
        partner = jnp.where(first, pltpu.roll(a, LANES - 16, 1), pltpu.roll(a, 16, 1))
        o_ref[:, cs] = (a * cos + partner * sin).astype(o_ref.dtype)


def _norm_proj_kernel(x_ref, g_ref, sh_ref, sc_ref, w_ref, *rest, rope_tiles):
    if rope_tiles:
        cos_ref, sin_ref, o_ref, h_ref = rest
    else:
        o_ref, h_ref = rest
    j = pl.program_id(1)

    @pl.when(j == 0)
    def _():
        x = x_ref[...]
        y = x * lax.rsqrt(jnp.mean(x * x, axis=-1, keepdims=True) + EPS) * g_ref[...]
        h_ref[...] = (y * (1.0 + sc_ref[...]) + sh_ref[...]).astype(BF16)

    acc = jnp.dot(h_ref[...], w_ref[...], preferred_element_type=F32)
    o_ref[...] = acc.astype(o_ref.dtype)
    nsub = o_ref.shape[1] // BRANCH_W
    for g in rope_tiles:
        @pl.when(j == g // nsub)
        def _(g=g):
            _rope_store(acc, cos_ref[...], sin_ref[...], o_ref, (g % nsub) * BRANCH_W)


def _norm_proj(x2, g, shift, scale, w, out_dtype, rope=None, rope_tiles=()):
    M, D = x2.shape
    N = w.shape[1]
    G = shift.shape[0]
    rows_per_group = M // G
    tm = _pick(rows_per_group, (1024, 512, 256))
    tn = _pick(N, (3 * BRANCH_W, 11 * LANES, 2 * BRANCH_W, BRANCH_W))
    assert not rope_tiles or tn % BRANCH_W == 0
    in_specs = [pl.BlockSpec((tm, D), lambda i, j: (i, 0)),
                pl.BlockSpec((1, D), lambda i, j: (0, 0)),
                pl.BlockSpec((None, 1, D), lambda i, j: (i * tm // rows_per_group, 0, 0)),
                pl.BlockSpec((None, 1, D), lambda i, j: (i * tm // rows_per_group, 0, 0)),
                pl.BlockSpec((D, tn), lambda i, j: (0, j))]
    args = [x2, g.reshape(1, D), shift, scale, w]
    if rope_tiles:
        cos, sin = rope
        nt = cos.shape[0] // tm
        in_specs += [pl.BlockSpec((tm, LANES), lambda i, j: (i % nt, 0)),
                     pl.BlockSpec((tm, LANES), lambda i, j: (i % nt, 0))]
        args += [cos, sin]
    return pl.pallas_call(
        functools.partial(_norm_proj_kernel, rope_tiles=tuple(rope_tiles)),
        grid=(M // tm, N // tn),
        in_specs=in_specs,
        out_specs=pl.BlockSpec((tm, tn), lambda i, j: (i, j)),
        out_shape=jax.ShapeDtypeStruct((M, N), out_dtype),
        scratch_shapes=[pltpu.VMEM((tm, D), BF16)],
        compiler_params=_params("parallel", "arbitrary"),
        name="norm_proj",
    )(*args)


def _rope_tables(T):
    t = jnp.arange(T)
    row = (t // GRID_W).astype(F32)
    col = (t % GRID_W).astype(F32)
    nf = HALF // 4
    inv = ROPE_BASE ** (-jnp.arange(nf, dtype=F32) / nf)
    ar = row[:, None] * inv
    ac = col[:, None] * inv
    cos = jnp.concatenate([jnp.cos(ar), jnp.cos(ar), jnp.cos(ac), jnp.cos(ac)], axis=-1)
    sin = jnp.concatenate([-jnp.sin(ar), jnp.sin(ar), -jnp.sin(ac), jnp.sin(ac)], axis=-1)
    return jnp.tile(cos, (1, 2)), jnp.tile(sin, (1, 2))


def _flash_kernel(q_ref, k_ref, v_ref, lam_ref, g_ref, o_ref, m_ref, acc_ref, sa_ref, sb_ref, *,
                  mode, tk, lam_init):
    tq = q_ref.shape[0]
    nk = k_ref.shape[0] // tk
    ones = jnp.ones((ONES_ROWS, tk), BF16)
    sub = lax.broadcasted_iota(jnp.int32, (LANES, 1), 0)
    lo = sub < HALF
    if mode == "diff":
        lp = lam_ref[...]
        d1 = jnp.sum(lp[0:1, :] * lp[1:2, :], keepdims=True)
        d2 = jnp.sum(lp[2:3, :] * lp[3:4, :], keepdims=True)
        lam = jnp.exp(d1) - jnp.exp(d2) + lam_init
    for h in range(q_ref.shape[1] // LANES):
        cs = slice(h * LANES, (h + 1) * LANES)
        qh = q_ref[:, cs].astype(F32).T.astype(BF16)
        zero = jnp.zeros_like(qh)
        q2 = jnp.concatenate([jnp.where(lo, qh, zero), jnp.where(lo, zero, qh)], axis=1)
        q2 = q2 * jnp.asarray(HALF ** -0.5, BF16)
        m_ref[...] = jnp.full(m_ref.shape, -jnp.inf, F32)
        acc_ref[...] = jnp.zeros(acc_ref.shape, F32)

        def scores(c, s_ref, q2=q2, cs=cs):
            off = pl.multiple_of(c * tk, tk)
            s_ref[...] = jnp.dot(k_ref[pl.ds(off, tk), cs], q2, preferred_element_type=F32)

        def update(c, s_ref, cs=cs):
            off = pl.multiple_of(c * tk, tk)
            vc = jnp.concatenate([v_ref[cs, pl.ds(off, tk)], ones], axis=0)
            s = s_ref[...]
            m_old = m_ref[...]
            m_new = jnp.maximum(m_old, jnp.max(s, axis=0, keepdims=True))
            alpha = jnp.exp(m_old - m_new)
            p = jnp.exp(s - m_new).astype(BF16)
            acc_ref[...] = alpha * acc_ref[...] + jnp.dot(vc, p, preferred_element_type=F32)
            m_ref[...] = m_new

        s0_ref, s1_ref = (sa_ref, sb_ref) if h % 2 == 0 else (sb_ref, sa_ref)
        scores(0, s0_ref)

        def body(j, carry, s0_ref=s0_ref, s1_ref=s1_ref):
            scores(2 * j + 1, s1_ref)
            update(2 * j, s0_ref)
            scores(2 * j + 2, s0_ref)
            update(2 * j + 1, s1_ref)
            return carry

        lax.fori_loop(0, (nk - 1) // 2, body, 0)
        update(nk - 1, s0_ref)
        a = acc_ref[0:LANES, :] / acc_ref[LANES:LANES + 1, :]
        a0 = a[:, :tq]
        a1 = a[:, tq:]
        if mode == "diff":
            o = a0 - lam * a1
            o = o * lax.rsqrt(jnp.mean(o * o, axis=0, keepdims=True) + EPS)
            o = o * g_ref[...] * (1.0 - lam_init)
        else:
            o = jnp.where(lo, a0, a1)
        o_ref[:, cs] = o.T.astype(o_ref.dtype)


def _flash(z, qblk, k, v, lam_p, g, *, mode, lam_init=0.0):
    B, Tq, _ = z.shape
    Tk, W = k.shape[1:]
    tq = _pick(Tq, (512, 256, 128))
    tk = next(c for c in (768, 512, 384, 256, 128) if Tk % c == 0 and (Tk // c) % 2 == 1)
    return pl.pallas_call(
        functools.partial(_flash_kernel, mode=mode, tk=tk, lam_init=lam_init),
        grid=(B, Tq // tq),
        in_specs=[pl.BlockSpec((None, tq, W), lambda b, i: (b, i, qblk)),
                  pl.BlockSpec((None, Tk, W), lambda b, i: (b, 0, 0)),
                  pl.BlockSpec((None, W, Tk), lambda b, i: (b, 0, 0)),
                  pl.BlockSpec(lam_p.shape, lambda b, i: (0, 0)),
                  pl.BlockSpec((LANES, 1), lambda b, i: (0, 0))],
        out_specs=pl.BlockSpec((None, tq, W), lambda b, i: (b, i, 0)),
        out_shape=jax.ShapeDtypeStruct((B, Tq, W), BF16),
        scratch_shapes=[pltpu.VMEM((1, 2 * tq), F32), pltpu.VMEM((LANES + ONES_ROWS, 2 * tq), F32),
                        pltpu.VMEM((tk, 2 * tq), F32), pltpu.VMEM((tk, 2 * tq), F32)],
        compiler_params=_params("parallel", "parallel"),
        name="flash_" + mode,
    )(z, k, jnp.swapaxes(v, 1, 2), lam_p, g.reshape(LANES, 1))


def _na_kernel(q_ref, k_ref, v_ref, kc_ref, vc_ref, u_ref, o_ref, b_ref, *, rows):
    rb = pl.program_id(2)
    nrb = pl.num_programs(2)
    nq = NA_QROWS * GRID_W
    nkk = NA_KROWS * GRID_W
    offs = _na_row_offsets(rows)
    lane1 = lax.broadcasted_iota(jnp.int32, (1, LANES), 1)

    def build(variant):
        for hh in range(2):
            for rl in range(NA_QROWS):
                for kp in range(NA_KROWS // 2):
                    a = int(offs[variant, rl, 2 * kp])
                    b2 = int(offs[variant, rl, 2 * kp + 1])
                    tile = u_ref[hh, a] if a == b2 else jnp.where(lane1 < GRID_W, u_ref[hh, a], u_ref[hh, b2])
                    b_ref[hh, rl * GRID_W:(rl + 1) * GRID_W, kp * LANES:(kp + 1) * LANES] = tile

    pl.when(rb == 0)(functools.partial(build, 0))
    if rows // NA_QROWS > 2:
        pl.when(rb == 1)(functools.partial(build, 1))
    pl.when(rb == nrb - 1)(functools.partial(build, 2))
    k0 = jnp.clip(rb * NA_QROWS - NA_WIN_R // 2, 0, rows - NA_KROWS) * GRID_W
    k0 = pl.multiple_of(k0, 4 * GRID_W)
    kl = k_ref[pl.ds(k0, nkk), :]
    vl = v_ref[pl.ds(k0, nkk), :]
    kc = kc_ref[...]
    vc = vc_ref[...]
    q = q_ref[...]
    zero = jnp.zeros_like(q)
    lane = lax.broadcasted_iota(jnp.int32, (1, LANES), 1)
    lo = lane < HALF
    scale = jnp.asarray(HALF ** -0.5, BF16)
    outs = []
    for hh in range(2):
        qm = (jnp.where(lo, q, zero) if hh == 0 else jnp.where(lo, zero, q)) * scale
        s_loc = lax.dot_general(qm, kl, _NT, preferred_element_type=F32) + b_ref[hh]
        s_ctx = lax.dot_general(qm, kc, _NT, preferred_element_type=F32)
        m = jnp.maximum(jnp.max(s_loc, axis=-1, keepdims=True), jnp.max(s_ctx, axis=-1, keepdims=True))
        p_loc = jnp.exp(s_loc - m)
        p_ctx = jnp.exp(s_ctx - m)
        l = jnp.sum(p_loc, axis=-1, keepdims=True) + jnp.sum(p_ctx, axis=-1, keepdims=True)
        o = jnp.dot(p_loc.astype(BF16), vl, preferred_element_type=F32) \
            + jnp.dot(p_ctx.astype(BF16), vc, preferred_element_type=F32)
        outs.append(o / l)
    del nq
    o_ref[...] = jnp.where(lo, outs[0], outs[1]).astype(o_ref.dtype)


def _na_tables(rpb, rows):
    H = rpb.shape[0]
    c = np.arange(GRID_W)
    cstart = np.clip(c - NA_WIN_C // 2, 0, GRID_W - NA_WIN_C)
    kcol = np.arange(GRID_W)
    col_ok = (kcol[None, :] >= cstart[:, None]) & (kcol[None, :] < cstart[:, None] + NA_WIN_C)
    nc = 2 * NA_WIN_C - 1
    dc = np.where(col_ok, kcol[None, :] - c[:, None] + NA_WIN_C - 1, nc)
    cm = np.arange(nc + 1)[None, None, :] == dc[:, :, None]
    rpb_ext = jnp.pad(rpb, ((0, 0), (0, 1), (0, 1)), constant_values=NEG)
    u = jnp.einsum('hij,cqj->hicq', rpb_ext, jnp.asarray(cm, F32), precision=lax.Precision.HIGHEST)
    return jnp.tile(u, (1, 1, 1, 2))


def _na_row_offsets(rows):
    nr = 2 * NA_WIN_R - 1
    out = []
    for q0, k0, clamp in ((0, 0, True), (NA_QROWS, NA_QROWS - NA_WIN_R // 2, False),
                          (rows - NA_QROWS, rows - NA_KROWS, True)):
        r = q0 + np.arange(NA_QROWS)
        kr = k0 + np.arange(NA_KROWS)
        r0 = r - NA_WIN_R // 2
        if clamp:
            r0 = np.clip(r0, 0, rows - NA_WIN_R)
        row_ok = (kr[None, :] >= r0[:, None]) & (kr[None, :] < r0[:, None] + NA_WIN_R)
        out.append(np.where(row_ok, kr[None, :] - r[:, None] + NA_WIN_R - 1, nr))
    return np.stack(out)


def _na_attn(z, zc, tables, *, qcol, kcol, vcol):
    B, T, _ = z.shape
    Lc = zc.shape[1]
    rows = T // GRID_W
    nrb = rows // NA_QROWS
    nq = NA_QROWS * GRID_W
    nkk = NA_KROWS * GRID_W
    nhp = BRANCH_W // LANES
    assert nrb >= 2
    return pl.pallas_call(
        functools.partial(_na_kernel, rows=rows),
        grid=(nhp, B, nrb),
        in_specs=[pl.BlockSpec((None, nq, LANES), lambda h, b, r: (b, r, qcol + h)),
                  pl.BlockSpec((None, T, LANES), lambda h, b, r: (b, 0, kcol + h)),
                  pl.BlockSpec((None, T, LANES), lambda h, b, r: (b, 0, vcol + h)),
                  pl.BlockSpec((None, Lc, LANES), lambda h, b, r: (b, 0, kcol + h)),
                  pl.BlockSpec((None, Lc, LANES), lambda h, b, r: (b, 0, vcol + h)),
                  pl.BlockSpec((2, 2 * NA_WIN_R, GRID_W, LANES), lambda h, b, r: (h, 0, 0, 0))],
        out_specs=pl.BlockSpec((None, nq, LANES), lambda h, b, r: (b, r, h)),
        out_shape=jax.ShapeDtypeStruct((B, T, BRANCH_W), BF16),
        scratch_shapes=[pltpu.VMEM((2, nq, nkk), F32)],
        compiler_params=_params("arbitrary", "arbitrary", "arbitrary"),
        name="na_attn",
    )(z, z, z, zc, zc, tables)


def _pool_kernel(prev_ref, cur_ref, next_ref, w_ref, sc_ref, o_ref, *, T):
    i = pl.program_id(1)
    tt = cur_ref.shape[0]
    E = tt + 2 * POOL_HALO
    ext = jnp.concatenate([prev_ref[...], cur_ref[...], next_ref[...]], axis=0).astype(F32)
    gi = i * tt - POOL_HALO + lax.broadcasted_iota(jnp.int32, (E, 1), 0)
    ext = jnp.where((gi >= 0) & (gi < T), ext, 0.0)
    t = i * tt + lax.broadcasted_iota(jnp.int32, (tt, 1), 0)
    for g, w in enumerate(POOL_WINDOWS):
        cs = slice(g * LANES, (g + 1) * LANES)
        zg = ext[:, cs]
        s = zg + pltpu.roll(zg, 1, 0)
        half = 1
        while 2 * half < w:
            s = pltpu.roll(s, half, 0) + pltpu.roll(s, E - half, 0)
            half *= 2
        cnt = (jnp.minimum(t + w // 2, T) - jnp.maximum(t - w // 2, 0)).astype(F32)
        c0 = slice(POOL_HALO, POOL_HALO + tt)
        d = s[c0] / cnt - zg[c0]
        y = jnp.dot(d.astype(BF16), w_ref[g].astype(BF16), preferred_element_type=F32)
        o_ref[:, cs] = (y * sc_ref[:, cs]).astype(o_ref.dtype)


def _pool(z, w_pool, scale, *, col):
    B, T, _ = z.shape
    tt = _pick(T, (1024, 512, 256))
    hb = tt // POOL_HALO
    nh = T // POOL_HALO
    return pl.pallas_call(
        functools.partial(_pool_kernel, T=T),
        grid=(B, T // tt),
        in_specs=[pl.BlockSpec((None, POOL_HALO, BRANCH_W), lambda b, i: (b, jnp.maximum(i * hb - 1, 0), col)),
                  pl.BlockSpec((None, tt, BRANCH_W), lambda b, i: (b, i, col)),
                  pl.BlockSpec((None, POOL_HALO, BRANCH_W), lambda b, i: (b, jnp.minimum((i + 1) * hb, nh - 1), col)),
                  pl.BlockSpec(w_pool.shape, lambda b, i: (0, 0, 0)),
                  pl.BlockSpec((1, BRANCH_W), lambda b, i: (0, 0))],
        out_specs=pl.BlockSpec((None, tt, BRANCH_W), lambda b, i: (b, i, 0)),
        out_shape=jax.ShapeDtypeStruct((B, T, BRANCH_W), BF16),
        compiler_params=_params("parallel", "parallel"),
        name="pool_mixer",
    )(z, z, z, w_pool, scale.reshape(1, BRANCH_W))


def _sgu_kernel(u_ref, v_ref, g_ref, w_ref, bt_ref, o_ref):
    ts = u_ref.shape[0]
    u = jax.nn.gelu(u_ref[...].astype(F32))
    v = jax.nn.gelu(v_ref[...].astype(F32))
    vc = v - jnp.mean(v, axis=-1, keepdims=True)
    v = vc * lax.rsqrt(jnp.mean(vc * vc, axis=-1, keepdims=True) + EPS) * g_ref[...]
    vb = v.astype(BF16)
    for g in range(BRANCH_W // LANES):
        cs = slice(g * LANES, (g + 1) * LANES)
        wg = w_ref[g].astype(BF16)
        bg = bt_ref[:, g:g + 1]
        for ck in range(ts // SGU_CHUNK):
            rs = slice(ck * SGU_CHUNK, (ck + 1) * SGU_CHUNK)
            f = jnp.dot(wg, vb[rs, cs], preferred_element_type=F32) + bg
            o_ref[rs, cs] = (u[rs, cs] * f).astype(o_ref.dtype)


def _sgu(z, g, w_s, b, *, ucol, vcol):
    B, T, _ = z.shape
    ts = _pick(T, (1024, 512, 256, 128))
    return pl.pallas_call(
        _sgu_kernel,
        grid=(B, T // ts),
        in_specs=[pl.BlockSpec((None, ts, BRANCH_W), lambda b_, i: (b_, i, ucol)),
                  pl.BlockSpec((None, ts, BRANCH_W), lambda b_, i: (b_, i, vcol)),
                  pl.BlockSpec((1, BRANCH_W), lambda b_, i: (0, 0)),
                  pl.BlockSpec(w_s.shape, lambda b_, i: (0, 0, 0)),
                  pl.BlockSpec((SGU_CHUNK, BRANCH_W // LANES), lambda b_, i: (0, 0))],
        out_specs=pl.BlockSpec((None, ts, BRANCH_W), lambda b_, i: (b_, i, 0)),
        out_shape=jax.ShapeDtypeStruct((B, T, BRANCH_W), BF16),
        compiler_params=_params("parallel", "parallel"),
        name="sgu_mixer",
    )(z, z, g.reshape(1, BRANCH_W), w_s, b.T)


def _merge_kernel(p_ref, d_ref, s_ref, n_ref, gz_ref, wbr_ref, wout_ref, x_ref, g1_ref,
                  n2_ref, sh_ref, sc_ref, rw_ref, xo_ref, h_ref, *, n_experts):
    D = x_ref.shape[1]
    acc = None
    for i, br in enumerate((p_ref, d_ref, s_ref, n_ref)):
        proj = jnp.dot(br[...], wbr_ref[i], preferred_element_type=F32)
        t = jax.nn.sigmoid(gz_ref[:, i * D:(i + 1) * D].astype(F32)) * proj
        acc = t if acc is None else acc + t
    mix = jnp.dot(acc.astype(BF16), wout_ref[...], preferred_element_type=F32)
    x = x_ref[...] + g1_ref[...] * mix
    xo_ref[...] = x
    y = x * lax.rsqrt(jnp.mean(x * x, axis=-1, keepdims=True) + EPS) * n2_ref[...]
    h = y * (1.0 + sc_ref[...]) + sh_ref[...]
    h_ref[:, 0:D] = h
    rw = rw_ref[...]
    h_hi = h.astype(BF16)
    h_lo = (h - h_hi.astype(F32)).astype(BF16)
    w_hi = rw.astype(BF16)
    w_lo = (rw - w_hi.astype(F32)).astype(BF16)
    lg = (jnp.dot(h_hi, w_hi, preferred_element_type=F32)
          + jnp.dot(h_lo, w_hi, preferred_element_type=F32)
          + jnp.dot(h_hi, w_lo, preferred_element_type=F32))
    lane = lax.broadcasted_iota(jnp.int32, (1, LANES), 1)
    lg = jnp.where(lane < n_experts, lg, -jnp.inf)
    ex = jnp.exp(lg - jnp.max(lg, axis=-1, keepdims=True))
    h_ref[:, D:D + LANES] = ex / jnp.sum(ex, axis=-1, keepdims=True)


def _merge(branches, zr, w_br, w_out, x2, g1, n2g, sh2, sc2, router_w):
    M, D = x2.shape
    G = g1.shape[0]
    rpg = M // G
    tm = _pick(rpg, (256,))
    E = router_w.shape[1]
    router_w = jnp.pad(router_w, ((0, 0), (0, LANES - E)))
    gb = 0
    grp = lambda i: (i * tm // rpg, 0, 0)
    row = lambda i: (i, 0)
    return pl.pallas_call(
        functools.partial(_merge_kernel, n_experts=E),
        grid=(M // tm,),
        in_specs=[pl.BlockSpec((tm, BRANCH_W), row)] * 4 + [
            pl.BlockSpec((tm, N_BRANCH * D), lambda i: (i, gb)),
            pl.BlockSpec(w_br.shape, lambda i: (0, 0, 0)),
            pl.BlockSpec(w_out.shape, lambda i: (0, 0)),
            pl.BlockSpec((tm, D), row),
            pl.BlockSpec((None, 1, D), grp),
            pl.BlockSpec((1, D), lambda i: (0, 0)),
            pl.BlockSpec((None, 1, D), grp),
            pl.BlockSpec((None, 1, D), grp),
            pl.BlockSpec(router_w.shape, lambda i: (0, 0))],
        out_specs=[pl.BlockSpec((tm, D), row), pl.BlockSpec((tm, D + LANES), row)],
        out_shape=[jax.ShapeDtypeStruct((M, D), F32), jax.ShapeDtypeStruct((M, D + LANES), F32)],
        compiler_params=_params("parallel"),
        name="merge_out",
    )(*branches, zr, w_br, w_out, x2, g1, n2g.reshape(1, D), sh2, sc2, router_w)


def _ffn_kernel(rows_ref, h_hbm, w1_ref, w3_ref, w2_ref, y_ref, w1b, w3b, w2b, xbuf, sem, *, B):
    e = pl.program_id(0)
    b = pl.program_id(1)
    C = xbuf.shape[1]
    D = w1_ref.shape[0]
    step = e * B + b
    n_exp = pl.num_programs(0)
    nsteps = n_exp * B
    slot = step % 2

    def start_row(base, slot_, j):
        pltpu.make_async_copy(h_hbm.at[pl.ds(rows_ref[base + j], 1)], xbuf.at[slot_, pl.ds(j, 1)],
                              sem.at[slot_]).start()

    def wait_slot(slot_):
        pltpu.make_async_copy(h_hbm.at[pl.ds(0, C)], xbuf.at[slot_], sem.at[slot_]).wait()

    @pl.when(step == 0)
    def _():
        def issue(j, carry):
            start_row((b * n_exp + e) * C, slot, j)
            return carry

        lax.fori_loop(0, C, issue, 0, unroll=8)

    @pl.when(b == 0)
    def _():
        w1b[...] = w1_ref[...].astype(BF16)
        w3b[...] = w3_ref[...].astype(BF16)
        w2b[...] = w2_ref[...].astype(BF16)

    last = step + 1 >= nsteps
    nxt_b = jnp.where(last, b, (b + 1) % B)
    nxt_e = jnp.where(last, e, e + (b + 1) // B)
    nxt_base = (nxt_b * n_exp + nxt_e) * C
    wait_slot(slot)
    lane = lax.broadcasted_iota(jnp.int32, (1, LANES), 1)
    rc = min(C, FFN_ROWS)
    for r in range(C // rc):
        for j in range(r * rc, (r + 1) * rc):
            start_row(nxt_base, 1 - slot, j)
        rs = pl.ds(r * rc, rc)
        xs = xbuf[slot, rs, 0:D].astype(BF16)
        gate = jnp.sum(jnp.where(lane == e, xbuf[slot, rs, D:D + LANES], 0.0), axis=-1, keepdims=True)
        a = jnp.dot(xs, w1b[...], preferred_element_type=F32)
        g = jnp.dot(xs, w3b[...], preferred_element_type=F32)
        hid = (a * jax.nn.sigmoid(a) * g).astype(BF16)
        y = jnp.dot(hid, w2b[...], preferred_element_type=F32) * gate
        y_ref[rs, :] = y.astype(y_ref.dtype)

    @pl.when(last)
    def _():
        wait_slot(1 - slot)


def _ffn(hext, idx, w1, w3, w2, layer, T):
    B, E, C = idx.shape
    D, Fd = w1.shape[2:]
    grid_spec = pltpu.PrefetchScalarGridSpec(
        num_scalar_prefetch=1,
        grid=(E, B),
        in_specs=[pl.BlockSpec(memory_space=pl.ANY),
                  pl.BlockSpec((None, None, D, Fd), lambda e, b, ix: (layer, e, 0, 0)),
                  pl.BlockSpec((None, None, D, Fd), lambda e, b, ix: (layer, e, 0, 0)),
                  pl.BlockSpec((None, None, Fd, D), lambda e, b, ix: (layer, e, 0, 0))],
        out_specs=pl.BlockSpec((None, None, C, D), lambda e, b, ix: (b, e, 0, 0)),
        scratch_shapes=[pltpu.VMEM((D, Fd), BF16), pltpu.VMEM((D, Fd), BF16), pltpu.VMEM((Fd, D), BF16),
                        pltpu.VMEM((2, C, D + LANES), F32), pltpu.SemaphoreType.DMA((2,))],
    )
    return pl.pallas_call(
        functools.partial(_ffn_kernel, B=B),
        grid_spec=grid_spec,
        out_shape=jax.ShapeDtypeStruct((B, E, C, D), BF16),
        compiler_params=_params("arbitrary", "arbitrary"),
        name="expert_ffn",
    )((idx + jnp.arange(B, dtype=jnp.int32)[:, None, None] * T).reshape(-1), hext, w1, w3, w2)


def _combine_kernel(starts_ref, x_ref, g2_ref, fg_ref, idx_ref, *rest, wb, nwin, final, spt):
    ys = rest[:-2]
    o_ref, acc_ref = rest[-2:]
    b = pl.program_id(0)
    t = pl.program_id(1)
    tt = x_ref.shape[0]
    E = idx_ref.shape[0]
    C = idx_ref.shape[1]
    tok = t * tt + lax.broadcasted_iota(jnp.int32, (tt, 1), 0)

    def window_onehot(e, wk):
        if wb == C:
            ids = idx_ref[e:e + 1, :]
        else:
            ids = idx_ref[e:e + 1, pl.ds(pl.multiple_of(wk * wb, wb), wb)]
        return jnp.where(ids == tok, 1.0, 0.0).astype(BF16)

    def window_dot(e, k, wk):
        return jnp.dot(window_onehot(e, wk), ys[e * nwin + k][...], preferred_element_type=F32)

    onehots = []
    for e in range(E):
        w0 = jnp.minimum(starts_ref[b, t * spt, e] // wb, C // wb - 1)
        onehots.append(window_onehot(e, w0))
    if wb % LANES == 0:
        acc_ref[...] = jnp.dot(jnp.concatenate(onehots, axis=1),
                               jnp.concatenate([ys[e * nwin][...] for e in range(E)], axis=0),
                               preferred_element_type=F32)
    else:
        acc = None
        for e in range(E):
            d = jnp.dot(onehots[e], ys[e * nwin][...], preferred_element_type=F32)
            acc = d if acc is None else acc + d
        acc_ref[...] = acc
    if nwin == 2:
        for e in range(E):
            s0 = starts_ref[b, t * spt, e]
            s1 = starts_ref[b, (t + 1) * spt, e]
            w0 = s0 // wb

            @pl.when((s1 > s0) & ((s1 - 1) // wb > w0))
            def _(e=e, w0=w0):
                acc_ref[...] += window_dot(e, 1, w0 + 1)

    x = x_ref[...] + g2_ref[...] * acc_ref[...]
    if final:
        x = x * lax.rsqrt(jnp.mean(x * x, axis=-1, keepdims=True) + EPS) * fg_ref[...]
    o_ref[...] = x


def _combine(x3, y, idx, starts, g2, final_g, *, final):
    B, T, D = x3.shape
    E, C = idx.shape[1:]
    if C >= COMBINE_TILE and T % COMBINE_TILE == 0:
        wb, nwin, tt = COMBINE_TILE, 2, COMBINE_TILE
    elif C >= LANES:
        wb, nwin, tt = LANES, 2, LANES
    else:
        wb, nwin, tt = C, 1, T
    nt = T // tt
    spt = tt // ROUTE_BLK
    G = g2.shape[0]

    def ymap(e, k):
        return lambda b, t, st: (b, e, jnp.minimum(st[b, t * spt, e] // wb + k, C // wb - 1), 0)

    y_specs = [pl.BlockSpec((None, None, wb, D), ymap(e, k)) for e in range(E) for k in range(nwin)]
    grid_spec = pltpu.PrefetchScalarGridSpec(
        num_scalar_prefetch=1,
        grid=(B, nt),
        in_specs=[pl.BlockSpec((None, tt, D), lambda b, t, st: (b, t, 0)),
                  pl.BlockSpec((None, 1, D), lambda b, t, st: (b * G // B, 0, 0)),
                  pl.BlockSpec((1, D), lambda b, t, st: (0, 0)),
                  pl.BlockSpec((None, E, C), lambda b, t, st: (b, 0, 0))] + y_specs,
        out_specs=pl.BlockSpec((None, tt, D), lambda b, t, st: (b, t, 0)),
        scratch_shapes=[pltpu.VMEM((tt, D), F32)],
    )
    return pl.pallas_call(
        functools.partial(_combine_kernel, wb=wb, nwin=nwin, final=final, spt=spt),
        grid_spec=grid_spec,
        out_shape=jax.ShapeDtypeStruct((B, T, D), F32),
        compiler_params=_params("parallel", "arbitrary"),
        name="moe_combine",
    )(starts, x3, g2, final_g.reshape(1, D), idx, *([y] * (E * nwin)))


def _route_kernel(aff_ref, idx_ref, starts_ref, cs_ref, *, C, E):
    T = aff_ref.shape[0]
    nb = T // ROUTE_BLK

    def count(mask):
        ones = jnp.where(mask, 1.0, 0.0)
        ways = 16 if T % (16 * SUBLANES) == 0 else 1
        part = jnp.sum(ones.reshape(ways, T // ways, LANES), axis=1)
        return jnp.sum(part, axis=0, keepdims=True)

    def value_of(pattern):
        ex = jnp.right_shift(pattern, F32_MANT_BITS)
        frac = (pattern & ((1 << F32_MANT_BITS) - 1)).astype(F32) * (2.0 ** -F32_MANT_BITS)
        n = F32_EXP_BIAS - ex
        p = jnp.ones(pattern.shape, F32)
        for i in range(7):
            p = jnp.where((jnp.right_shift(n, i) & 1) == 1, p * (2.0 ** -(2 ** i)), p)
        return jnp.where(ex == 0, 0.0, p * (1.0 + frac))

    def bit_body(i, prefix):
        cand = prefix | jnp.left_shift(jnp.int32(1), 29 - i)
        cnt = count(aff_ref[...] >= value_of(cand))
        return jnp.where(cnt >= C, cand, prefix)

    vstar = value_of(lax.fori_loop(0, 30, bit_body, jnp.zeros((1, LANES), jnp.int32)))
    n_gt = count(aff_ref[...] > vstar)
    n_tie = C - n_gt

    r = lax.broadcasted_iota(jnp.int32, (ROUTE_BLK, ROUTE_BLK), 0)
    c = lax.broadcasted_iota(jnp.int32, (ROUTE_BLK, ROUTE_BLK), 1)
    lower_strict = jnp.where(c < r, 1.0, 0.0).astype(BF16)
    lower_incl = jnp.where(c <= r, 1.0, 0.0).astype(BF16)

    def blk(k, carry):
        ties_before, chosen_before = carry
        off = pl.multiple_of(k * ROUTE_BLK, ROUTE_BLK)
        a = aff_ref[pl.ds(off, ROUTE_BLK), :]
        gt = a > vstar
        eq = jnp.where(a == vstar, 1.0, 0.0)
        rank = jnp.dot(lower_strict, eq.astype(BF16), preferred_element_type=F32) + ties_before
        sel = jnp.where(gt | ((eq > 0.0) & (rank < n_tie)), 1.0, 0.0)
        csum = jnp.dot(lower_incl, sel.astype(BF16), preferred_element_type=F32) + chosen_before
        cs_ref[pl.ds(off, ROUTE_BLK), :] = csum
        starts_ref[k] = jnp.broadcast_to(chosen_before.astype(jnp.int32), (SUBLANES, LANES))
        return (ties_before + jnp.sum(eq, axis=0, keepdims=True),
                chosen_before + jnp.sum(sel, axis=0, keepdims=True))

    zero = jnp.zeros((1, LANES), F32)
    _, total = lax.fori_loop(0, nb, blk, (zero, zero))
    starts_ref[nb] = jnp.broadcast_to(total.astype(jnp.int32), (SUBLANES, LANES))

    lane_f = lax.broadcasted_iota(jnp.int32, (1, LANES), 1).astype(F32)

    def expert(e, carry):
        pick = jnp.where(r == e, 1.0, 0.0)
        col = jnp.dot(cs_ref[...], pick, preferred_element_type=F32,
                      precision=lax.Precision.HIGHEST)
        for jt in range(-(-C // LANES)):
            cnt = count(col <= lane_f + jt * LANES)
            w = min(LANES, C - jt * LANES)
            idx_ref[e, :, jt * LANES:jt * LANES + w] = jnp.broadcast_to(
                cnt[:, 0:w].astype(jnp.int32), (SUBLANES, w))
        return carry

    lax.fori_loop(0, E, expert, 0)


def _route(hext3, E, C):
    B, T, W = hext3.shape
    nb = T // ROUTE_BLK
    idx8, starts8 = pl.pallas_call(
        functools.partial(_route_kernel, C=C, E=E),
        grid=(B,),
        in_specs=[pl.BlockSpec((None, T, LANES), lambda b: (b, 0, W // LANES - 1))],
        out_specs=[pl.BlockSpec((None, E, SUBLANES, C), lambda b: (b, 0, 0, 0)),
                   pl.BlockSpec((None, nb + 1, SUBLANES, LANES), lambda b: (b, 0, 0, 0))],
        out_shape=[jax.ShapeDtypeStruct((B, E, SUBLANES, C), jnp.int32),
                   jax.ShapeDtypeStruct((B, nb + 1, SUBLANES, LANES), jnp.int32)],
        scratch_shapes=[pltpu.VMEM((T, LANES), F32)],
        compiler_params=_params("parallel"),
        name="ec_route",
    )(hext3)
    return idx8[:, :, 0, :], starts8[:, :, 0, :E]


def _moe(x3, hext, w1, w3, w2, layer, g2, final_g, *, final):
    B, T, D = x3.shape
    E = w1.shape[1]
    C = EC_CAPACITY * T // E
    idx, starts = _route(hext.reshape(B, T, D + LANES), E, C)
    y = _ffn(hext, idx, w1, w3, w2, layer, T)
    return _combine(x3, y, idx, starts, g2, final_g, final=final)


_C_DK, _C_DV, _C_NK, _C_NV, _C_DQ, _C_NQ = 0, 4, 8, 12, 16, 20
ATT_COLS = 6 * BRANCH_W
KV_COLS = 4 * BRANCH_W
GATE_START = 9 * BRANCH_W
_R_POOL, _R_SU, _R_SV = 8, 9, 10


def _mixers(za, zr, zc_kv, T, p, tables, lam_init, *, latent):
    B = za.shape[0]
    sl = lambda a, c: a[:, :, c * LANES:c * LANES + BRANCH_W]
    if latent:
        k_all = jnp.concatenate([sl(zc_kv, _C_DK), sl(za, _C_DK)], axis=1)
        v_all = jnp.concatenate([sl(zc_kv, _C_DV), sl(za, _C_DV)], axis=1)
        diff = _flash(za, _C_DQ // 4, k_all, v_all, p['lam'], p['subln'], mode="diff", lam_init=lam_init)
        na = _na_attn(za, zc_kv, tables, qcol=_C_NQ, kcol=_C_NK, vcol=_C_NV)
    else:
        diff = _flash(za, _C_DQ // 4, sl(za, _C_DK), sl(za, _C_DV), p['lam'], p['subln'],
                      mode="diff", lam_init=lam_init)
        na = _flash(za, _C_NQ // 4, sl(za, _C_NK), sl(za, _C_NV), p['lam'], p['subln'], mode="dense")
    pool = _pool(zr, p['pool_w'], p['pool_scale'], col=_R_POOL)
    sgu = _sgu(zr, p['sgu_g'], p['sgu_w'], p['sgu_b'], ucol=_R_SU, vcol=_R_SV)
    M = B * T
    f2 = lambda a: a.reshape(M, a.shape[-1])
    return [f2(pool), f2(diff), f2(sgu), f2(na)]


def kernel(x, c, ctx, c_ctx, w_mod, b_mod, norm1_g, norm2_g, w_in, pool_w, pool_scale, diff_lambda,
           diff_subln_g, sgu_norm_g, sgu_w, sgu_b, na_rpb, w_br, w_out, router_w, moe_w1, moe_w3,
           moe_w2, final_g):
    B, T, D = x.shape
    Lc = ctx.shape[1]
    depth = w_in.shape[0]
    rows = T // GRID_W
    rope = _rope_tables(T)
    cvec = jnp.concatenate([c, c_ctx[None], jnp.zeros((8 - B - 1, D), F32)], axis=0)
    x_lat = x.reshape(B * T, D)
    x_ctx = ctx.reshape(B * Lc, D)
    for l in range(depth):
        last = l == depth - 1
        lam_init = 0.8 - 0.6 * math.exp(-0.3 * l)
        mod = _mod_vectors(cvec, w_mod, b_mod, l)
        mx = mod[:B].reshape(B, 6, 1, D)
        mc = mod[B:B + 1].reshape(1, 6, 1, D)
        sh1, sc1, g1, sh2, sc2, g2 = [mx[:, i] for i in range(6)]
        csh1, csc1, cg1, csh2, csc2, cg2 = [mc[:, i] for i in range(6)]
        w_in_b = w_in[l].astype(BF16)
        w_att = w_in_b[:, :ATT_COLS]
        w_rest = jnp.concatenate([w_in_b[:, GATE_START:], w_in_b[:, ATT_COLS:GATE_START]], axis=1)
        p = {'lam': diff_lambda[l], 'subln': diff_subln_g[l], 'pool_w': pool_w[l],
             'pool_scale': pool_scale[l], 'sgu_g': sgu_norm_g[l], 'sgu_w': sgu_w[l], 'sgu_b': sgu_b[l]}
        w_br_b = w_br[l].astype(BF16)
        w_out_b = w_out[l].astype(BF16)
        tables = _na_tables(na_rpb[l], rows)

        if last:
            zc_a = _norm_proj(x_ctx, norm1_g[l], csh1, csc1, w_att[:, :KV_COLS], BF16)
        else:
            zc_a = _norm_proj(x_ctx, norm1_g[l], csh1, csc1, w_att, BF16)
        zc_a = zc_a.reshape(B, Lc, -1)

        za = _norm_proj(x_lat, norm1_g[l], sh1, sc1, w_att, BF16, rope=rope,
                        rope_tiles=(_C_DK // 4, _C_DQ // 4)).reshape(B, T, ATT_COLS)
        zr = _norm_proj(x_lat, norm1_g[l], sh1, sc1, w_rest, F32).reshape(B, T, -1)
        br = _mixers(za, zr, zc_a, T, p, tables, lam_init, latent=True)
        x_lat, h2 = _merge(br, zr.reshape(B * T, -1), w_br_b, w_out_b, x_lat, g1,
                                   norm2_g[l], sh2, sc2, router_w[l])
        x_lat = _moe(x_lat.reshape(B, T, D), h2, moe_w1, moe_w3, moe_w2, l, g2,
                     final_g, final=last).reshape(B * T, D)

        if not last:
            zc_r = _norm_proj(x_ctx, norm1_g[l], csh1, csc1, w_rest, F32).reshape(B, Lc, -1)
            brc = _mixers(zc_a, zc_r, None, Lc, p, None, lam_init, latent=False)
            x_ctx, hc2 = _merge(brc, zc_r.reshape(B * Lc, -1), w_br_b, w_out_b, x_ctx, cg1,
                                     norm2_g[l], csh2, csc2, router_w[l])
            x_ctx = _moe(x_ctx.reshape(B, Lc, D), hc2, moe_w1, moe_w3, moe_w2, l, cg2,
                         final_g, final=False).reshape(B * Lc, D)
    return x_lat.reshape(B, T, D)
```

```python
import functools
import math

import numpy as np
import jax
import jax.numpy as jnp
from jax import lax
from jax.experimental import pallas as pl
from jax.experimental.pallas import tpu as pltpu

F32 = jnp.float32
BF16 = jnp.bfloat16

EPS = 1e-6
GRID_W = 64
LANES = 128
SUBLANES = 8
HALF = 64
ONES_ROWS = 16
N_BRANCH = 4
BRANCH_W = 512
POOL_WINDOWS = (2, 4, 8, 16)
POOL_HALO = 16
SGU_CHUNK = 128
NA_WIN_R = 8
NA_WIN_C = 16
NA_QROWS = 8
NA_KROWS = 16
N_EXPERTS = 16
EC_CAPACITY = 2
ROUTE_BLK = 128
FFN_ROWS = 256
COMBINE_TILE = 256
F32_MANT_BITS = 23
F32_EXP_BIAS = 127
ROPE_BASE = 10000.0
NEG = -1e30
VMEM_LIMIT = 56 * 1024 * 1024

_NT = (((1,), (1,)), ((), ()))


def _params(*sem):
    return pltpu.CompilerParams(dimension_semantics=sem, vmem_limit_bytes=VMEM_LIMIT)


def _pick(n, cands):
    for c in cands:
        if n % c == 0:
            return c
    raise ValueError(f"no tile for {n}")


def _mod_kernel(c_ref, w_ref, b_ref, o_ref):
    c = c_ref[...]
    s = c * jax.nn.sigmoid(c)
    o_ref[...] = jnp.dot(s, w_ref[...], preferred_element_type=F32,
                         precision=lax.Precision.HIGHEST) + b_ref[...]


def _mod_vectors(cvec, w, b, layer):
    L, D, N = w.shape
    tn = _pick(N, (1536, 1024, 512, 128))
    return pl.pallas_call(
        _mod_kernel,
        grid=(N // tn,),
        in_specs=[pl.BlockSpec((8, D), lambda j: (0, 0)),
                  pl.BlockSpec((None, D, tn), lambda j: (layer, 0, j)),
                  pl.BlockSpec((None, 1, tn), lambda j: (layer, 0, j))],
        out_specs=pl.BlockSpec((8, tn), lambda j: (0, j)),
        out_shape=jax.ShapeDtypeStruct((8, N), F32),
        compiler_params=_params("parallel"),
        name="mod_vectors",
    )(cvec, w, b.reshape(L, 1, N))


def _rope_store(acc, cos, sin, o_ref, col0):
    lane = lax.broadcasted_iota(jnp.int32, (1, LANES), 1)
    first = (lane % 32) < 16
    for h in range(BRANCH_W // LANES):
        cs = slice(col0 + h * LANES, col0 + (h + 1) * LANES)
        a = acc[:, cs]
        partner = jnp.where(first, pltpu.roll(a, LANES - 16, 1), pltpu.roll(a, 16, 1))
        o_ref[:, cs] = (a * cos + partner * sin).astype(o_ref.dtype)


def _norm_proj_kernel(x_ref, g_ref, sh_ref, sc_ref, w_ref, *rest, rope_tiles):
    if rope_tiles:
        cos_ref, sin_ref, o_ref, h_ref = rest
    else:
        o_ref, h_ref = rest
    j = pl.program_id(1)

    @pl.when(j == 0)
    def _():
        x = x_ref[...]
        y = x * lax.rsqrt(jnp.mean(x * x, axis=-1, keepdims=True) + EPS) * g_ref[...]
        h_ref[...] = (y * (1.0 + sc_ref[...]) + sh_ref[...]).astype(BF16)

    acc = jnp.dot(h_ref[...], w_ref[...], preferred_element_type=F32)
    o_ref[...] = acc.astype(o_ref.dtype)
    nsub = o_ref.shape[1] // BRANCH_W
    for g in rope_tiles:
        @pl.when(j == g // nsub)
        def _(g=g):
            _rope_store(acc, cos_ref[...], sin_ref[...], o_ref, (g % nsub) * BRANCH_W)


def _norm_proj(x2, g, shift, scale, w, out_dtype, rope=None, rope_tiles=()):
    M, D = x2.shape
    N = w.shape[1]
    G = shift.shape[0]
    rows_per_group = M // G
    tm = _pick(rows_per_group, (1024, 512, 256))
    tn = _pick(N, (3 * BRANCH_W, 11 * LANES, 2 * BRANCH_W, BRANCH_W))
    assert not rope_tiles or tn % BRANCH_W == 0
    in_specs = [pl.BlockSpec((tm, D), lambda i, j: (i, 0)),
                pl.BlockSpec((1, D), lambda i, j: (0, 0)),
                pl.BlockSpec((None, 1, D), lambda i, j: (i * tm // rows_per_group, 0, 0)),
                pl.BlockSpec((None, 1, D), lambda i, j: (i * tm // rows_per_group, 0, 0)),
                pl.BlockSpec((D, tn), lambda i, j: (0, j))]
    args = [x2, g.reshape(1, D), shift, scale, w]
    if rope_tiles:
        cos, sin = rope
        nt = cos.shape[0] // tm
        in_specs += [pl.BlockSpec((tm, LANES), lambda i, j: (i % nt, 0)),
                     pl.BlockSpec((tm, LANES), lambda i, j: (i % nt, 0))]
        args += [cos, sin]
    return pl.pallas_call(
        functools.partial(_norm_proj_kernel, rope_tiles=tuple(rope_tiles)),
        grid=(M // tm, N // tn),
        in_specs=in_specs,
        out_specs=pl.BlockSpec((tm, tn), lambda i, j: (i, j)),
        out_shape=jax.ShapeDtypeStruct((M, N), out_dtype),
        scratch_shapes=[pltpu.VMEM((tm, D), BF16)],
        compiler_params=_params("parallel", "arbitrary"),
        name="norm_proj",
    )(*args)


def _rope_tables(T):
    t = np.arange(T)
    row = (t // GRID_W).astype(np.float64)
    col = (t % GRID_W).astype(np.float64)
    nf = HALF // 4
    inv = ROPE_BASE ** (-np.arange(nf, dtype=np.float64) / nf)
    ar = row[:, None] * inv
    ac = col[:, None] * inv
    cos = np.concatenate([np.cos(ar), np.cos(ar), np.cos(ac), np.cos(ac)], axis=-1)
    sin = np.concatenate([-np.sin(ar), np.sin(ar), -np.sin(ac), np.sin(ac)], axis=-1)
    return (jnp.asarray(np.tile(cos, (1, 2)), F32), jnp.asarray(np.tile(sin, (1, 2)), F32))


def _flash_kernel(q_ref, k_ref, v_ref, lam_ref, g_ref, o_ref, m_ref, acc_ref, sa_ref, sb_ref, *,
                  mode, tk, lam_init):
    tq = q_ref.shape[0]
    nk = k_ref.shape[0] // tk
    ones = jnp.ones((ONES_ROWS, tk), BF16)
    sub = lax.broadcasted_iota(jnp.int32, (LANES, 1), 0)
    lo = sub < HALF
    if mode == "diff":
        lp = lam_ref[...]
        d1 = jnp.sum(lp[0:1, :] * lp[1:2, :], keepdims=True)
        d2 = jnp.sum(lp[2:3, :] * lp[3:4, :], keepdims=True)
        lam = jnp.exp(d1) - jnp.exp(d2) + lam_init
    for h in range(q_ref.shape[1] // LANES):
        cs = slice(h * LANES, (h + 1) * LANES)
        qh = q_ref[:, cs].astype(F32).T.astype(BF16)
        zero = jnp.zeros_like(qh)
        q2 = jnp.concatenate([jnp.where(lo, qh, zero), jnp.where(lo, zero, qh)], axis=1)
        q2 = q2 * jnp.asarray(HALF ** -0.5, BF16)
        m_ref[...] = jnp.full(m_ref.shape, -jnp.inf, F32)
        acc_ref[...] = jnp.zeros(acc_ref.shape, F32)

        def scores(c, s_ref, q2=q2, cs=cs):
            off = pl.multiple_of(c * tk, tk)
            s_ref[...] = jnp.dot(k_ref[pl.ds(off, tk), cs], q2, preferred_element_type=F32)

        def update(c, s_ref, cs=cs):
            off = pl.multiple_of(c * tk, tk)
            vc = jnp.concatenate([v_ref[cs, pl.ds(off, tk)], ones], axis=0)
            s = s_ref[...]
            m_old = m_ref[...]
            m_new = jnp.maximum(m_old, jnp.max(s, axis=0, keepdims=True))
            alpha = jnp.exp(m_old - m_new)
            p = jnp.exp(s - m_new).astype(BF16)
            acc_ref[...] = alpha * acc_ref[...] + jnp.dot(vc, p, preferred_element_type=F32)
            m_ref[...] = m_new

        s0_ref, s1_ref = (sa_ref, sb_ref) if h % 2 == 0 else (sb_ref, sa_ref)
        scores(0, s0_ref)

        def body(j, carry, s0_ref=s0_ref, s1_ref=s1_ref):
            scores(2 * j + 1, s1_ref)
            update(2 * j, s0_ref)
            scores(2 * j + 2, s0_ref)
            update(2 * j + 1, s1_ref)
            return carry

        lax.fori_loop(0, (nk - 1) // 2, body, 0)
        update(nk - 1, s0_ref)
        a = acc_ref[0:LANES, :] / acc_ref[LANES:LANES + 1, :]
        a0 = a[:, :tq]
        a1 = a[:, tq:]
        if mode == "diff":
            o = a0 - lam * a1
            o = o * lax.rsqrt(jnp.mean(o * o, axis=0, keepdims=True) + EPS)
            o = o * g_ref[...] * (1.0 - lam_init)
        else:
            o = jnp.where(lo, a0, a1)
        o_ref[:, cs] = o.T.astype(o_ref.dtype)


def _flash(z, qblk, k, v, lam_p, g, *, mode, lam_init=0.0):
    B, Tq, _ = z.shape
    Tk, W = k.shape[1:]
    tq = _pick(Tq, (512, 256, 128))
    tk = next(c for c in (768, 512, 384, 256, 128) if Tk % c == 0 and (Tk // c) % 2 == 1)
    return pl.pallas_call(
        functools.partial(_flash_kernel, mode=mode, tk=tk, lam_init=lam_init),
        grid=(B, Tq // tq),
        in_specs=[pl.BlockSpec((None, tq, W), lambda b, i: (b, i, qblk)),
                  pl.BlockSpec((None, Tk, W), lambda b, i: (b, 0, 0)),
                  pl.BlockSpec((None, W, Tk), lambda b, i: (b, 0, 0)),
                  pl.BlockSpec(lam_p.shape, lambda b, i: (0, 0)),
                  pl.BlockSpec((LANES, 1), lambda b, i: (0, 0))],
        out_specs=pl.BlockSpec((None, tq, W), lambda b, i: (b, i, 0)),
        out_shape=jax.ShapeDtypeStruct((B, Tq, W), BF16),
        scratch_shapes=[pltpu.VMEM((1, 2 * tq), F32), pltpu.VMEM((LANES + ONES_ROWS, 2 * tq), F32),
                        pltpu.VMEM((tk, 2 * tq), F32), pltpu.VMEM((tk, 2 * tq), F32)],
        compiler_params=_params("parallel", "parallel"),
        name="flash_" + mode,
    )(z, k, jnp.swapaxes(v, 1, 2), lam_p, g.reshape(LANES, 1))


def _na_kernel(q_ref, k_ref, v_ref, kc_ref, vc_ref, u_ref, o_ref, b_ref, *, rows):
    rb = pl.program_id(2)
    nrb = pl.num_programs(2)
    nq = NA_QROWS * GRID_W
    nkk = NA_KROWS * GRID_W
    offs = _na_row_offsets(rows)
    lane1 = lax.broadcasted_iota(jnp.int32, (1, LANES), 1)

    def build(variant):
        for hh in range(2):
            for rl in range(NA_QROWS):
                for kp in range(NA_KROWS // 2):
                    a = int(offs[variant, rl, 2 * kp])
                    b2 = int(offs[variant, rl, 2 * kp + 1])
                    tile = u_ref[hh, a] if a == b2 else jnp.where(lane1 < GRID_W, u_ref[hh, a], u_ref[hh, b2])
                    b_ref[hh, rl * GRID_W:(rl + 1) * GRID_W, kp * LANES:(kp + 1) * LANES] = tile

    pl.when(rb == 0)(functools.partial(build, 0))
    if rows // NA_QROWS > 2:
        pl.when(rb == 1)(functools.partial(build, 1))
    pl.when(rb == nrb - 1)(functools.partial(build, 2))
    k0 = jnp.clip(rb * NA_QROWS - NA_WIN_R // 2, 0, rows - NA_KROWS) * GRID_W
    k0 = pl.multiple_of(k0, 4 * GRID_W)
    kl = k_ref[pl.ds(k0, nkk), :]
    vl = v_ref[pl.ds(k0, nkk), :]
    kc = kc_ref[...]
    vc = vc_ref[...]
    q = q_ref[...]
    zero = jnp.zeros_like(q)
    lane = lax.broadcasted_iota(jnp.int32, (1, LANES), 1)
    lo = lane < HALF
    scale = jnp.asarray(HALF ** -0.5, BF16)
    outs = []
    for hh in range(2):
        qm = (jnp.where(lo, q, zero) if hh == 0 else jnp.where(lo, zero, q)) * scale
        s_loc = lax.dot_general(qm, kl, _NT, preferred_element_type=F32) + b_ref[hh]
        s_ctx = lax.dot_general(qm, kc, _NT, preferred_element_type=F32)
        m = jnp.maximum(jnp.max(s_loc, axis=-1, keepdims=True), jnp.max(s_ctx, axis=-1, keepdims=True))
        p_loc = jnp.exp(s_loc - m)
        p_ctx = jnp.exp(s_ctx - m)
        l = jnp.sum(p_loc, axis=-1, keepdims=True) + jnp.sum(p_ctx, axis=-1, keepdims=True)
        o = jnp.dot(p_loc.astype(BF16), vl, preferred_element_type=F32) \
            + jnp.dot(p_ctx.astype(BF16), vc, preferred_element_type=F32)
        outs.append(o / l)
    del nq
    o_ref[...] = jnp.where(lo, outs[0], outs[1]).astype(o_ref.dtype)


def _na_tables(rpb, rows):
    H = rpb.shape[0]
    c = np.arange(GRID_W)
    cstart = np.clip(c - NA_WIN_C // 2, 0, GRID_W - NA_WIN_C)
    kcol = np.arange(GRID_W)
    col_ok = (kcol[None, :] >= cstart[:, None]) & (kcol[None, :] < cstart[:, None] + NA_WIN_C)
    nc = 2 * NA_WIN_C - 1
    dc = np.where(col_ok, kcol[None, :] - c[:, None] + NA_WIN_C - 1, nc)
    cm = np.arange(nc + 1)[None, None, :] == dc[:, :, None]
    rpb_ext = jnp.pad(rpb, ((0, 0), (0, 1), (0, 1)), constant_values=NEG)
    cm2 = np.concatenate([cm, cm], axis=1)
    return jnp.einsum('hij,cqj->hicq', rpb_ext, jnp.asarray(cm2, F32),
                      precision=lax.Precision.HIGHEST)


def _na_row_offsets(rows):
    nr = 2 * NA_WIN_R - 1
    out = []
    for q0, k0, clamp in ((0, 0, True), (NA_QROWS, NA_QROWS - NA_WIN_R // 2, False),
                          (rows - NA_QROWS, rows - NA_KROWS, True)):
        r = q0 + np.arange(NA_QROWS)
        kr = k0 + np.arange(NA_KROWS)
        r0 = r - NA_WIN_R // 2
        if clamp:
            r0 = np.clip(r0, 0, rows - NA_WIN_R)
        row_ok = (kr[None, :] >= r0[:, None]) & (kr[None, :] < r0[:, None] + NA_WIN_R)
        out.append(np.where(row_ok, kr[None, :] - r[:, None] + NA_WIN_R - 1, nr))
    return np.stack(out)


def _na_attn(z, zc, tables, *, qcol, kcol, vcol):
    B, T, _ = z.shape
    Lc = zc.shape[1]
    rows = T // GRID_W
    nrb = rows // NA_QROWS
    nq = NA_QROWS * GRID_W
    nkk = NA_KROWS * GRID_W
    nhp = BRANCH_W // LANES
    assert nrb >= 2
    return pl.pallas_call(
        functools.partial(_na_kernel, rows=rows),
        grid=(nhp, B, nrb),
        in_specs=[pl.BlockSpec((None, nq, LANES), lambda h, b, r: (b, r, qcol + h)),
                  pl.BlockSpec((None, T, LANES), lambda h, b, r: (b, 0, kcol + h)),
                  pl.BlockSpec((None, T, LANES), lambda h, b, r: (b, 0, vcol + h)),
                  pl.BlockSpec((None, Lc, LANES), lambda h, b, r: (b, 0, kcol + h)),
                  pl.BlockSpec((None, Lc, LANES), lambda h, b, r: (b, 0, vcol + h)),
                  pl.BlockSpec((2, 2 * NA_WIN_R, GRID_W, LANES), lambda h, b, r: (h, 0, 0, 0))],
        out_specs=pl.BlockSpec((None, nq, LANES), lambda h, b, r: (b, r, h)),
        out_shape=jax.ShapeDtypeStruct((B, T, BRANCH_W), BF16),
        scratch_shapes=[pltpu.VMEM((2, nq, nkk), F32)],
        compiler_params=_params("arbitrary", "arbitrary", "arbitrary"),
        name="na_attn",
    )(z, z, z, zc, zc, tables)


def _pool_kernel(prev_ref, cur_ref, next_ref, w_ref, sc_ref, o_ref, *, T):
    i = pl.program_id(1)
    tt = cur_ref.shape[0]
    E = tt + 2 * POOL_HALO
    ext = jnp.concatenate([prev_ref[...], cur_ref[...], next_ref[...]], axis=0).astype(F32)
    gi = i * tt - POOL_HALO + lax.broadcasted_iota(jnp.int32, (E, 1), 0)
    ext = jnp.where((gi >= 0) & (gi < T), ext, 0.0)
    t = i * tt + lax.broadcasted_iota(jnp.int32, (tt, 1), 0)
    for g, w in enumerate(POOL_WINDOWS):
        cs = slice(g * LANES, (g + 1) * LANES)
        zg = ext[:, cs]
        s = zg + pltpu.roll(zg, 1, 0)
        half = 1
        while 2 * half < w:
            s = pltpu.roll(s, half, 0) + pltpu.roll(s, E - half, 0)
            half *= 2
        cnt = (jnp.minimum(t + w // 2, T) - jnp.maximum(t - w // 2, 0)).astype(F32)
        c0 = slice(POOL_HALO, POOL_HALO + tt)
        d = s[c0] / cnt - zg[c0]
        y = jnp.dot(d.astype(BF16), w_ref[g].astype(BF16), preferred_element_type=F32)
        o_ref[:, cs] = (y * sc_ref[:, cs]).astype(o_ref.dtype)


def _pool(z, w_pool, scale, *, col):
    B, T, _ = z.shape
    tt = _pick(T, (1024, 512, 256))
    hb = tt // POOL_HALO
    nh = T // POOL_HALO
    return pl.pallas_call(
        functools.partial(_pool_kernel, T=T),
        grid=(B, T // tt),
        in_specs=[pl.BlockSpec((None, POOL_HALO, BRANCH_W), lambda b, i: (b, jnp.maximum(i * hb - 1, 0), col)),
                  pl.BlockSpec((None, tt, BRANCH_W), lambda b, i: (b, i, col)),
                  pl.BlockSpec((None, POOL_HALO, BRANCH_W), lambda b, i: (b, jnp.minimum((i + 1) * hb, nh - 1), col)),
                  pl.BlockSpec(w_pool.shape, lambda b, i: (0, 0, 0)),
                  pl.BlockSpec((1, BRANCH_W), lambda b, i: (0, 0))],
        out_specs=pl.BlockSpec((None, tt, BRANCH_W), lambda b, i: (b, i, 0)),
        out_shape=jax.ShapeDtypeStruct((B, T, BRANCH_W), BF16),
        compiler_params=_params("parallel", "parallel"),
        name="pool_mixer",
    )(z, z, z, w_pool, scale.reshape(1, BRANCH_W))


def _sgu_kernel(u_ref, v_ref, g_ref, w_ref, bt_ref, o_ref):
    ts = u_ref.shape[0]
    u = jax.nn.gelu(u_ref[...].astype(F32))
    v = jax.nn.gelu(v_ref[...].astype(F32))
    vc = v - jnp.mean(v, axis=-1, keepdims=True)
    v = vc * lax.rsqrt(jnp.mean(vc * vc, axis=-1, keepdims=True) + EPS) * g_ref[...]
    vb = v.astype(BF16)
    for g in range(BRANCH_W // LANES):
        cs = slice(g * LANES, (g + 1) * LANES)
        wg = w_ref[g].astype(BF16)
        bg = bt_ref[:, g:g + 1]
        for ck in range(ts // SGU_CHUNK):
            rs = slice(ck * SGU_CHUNK, (ck + 1) * SGU_CHUNK)
            f = jnp.dot(wg, vb[rs, cs], preferred_element_type=F32) + bg
            o_ref[rs, cs] = (u[rs, cs] * f).astype(o_ref.dtype)


def _sgu(z, g, w_s, b, *, ucol, vcol):
    B, T, _ = z.shape
    ts = _pick(T, (1024, 512, 256, 128))
    return pl.pallas_call(
        _sgu_kernel,
        grid=(B, T // ts),
        in_specs=[pl.BlockSpec((None, ts, BRANCH_W), lambda b_, i: (b_, i, ucol)),
                  pl.BlockSpec((None, ts, BRANCH_W), lambda b_, i: (b_, i, vcol)),
                  pl.BlockSpec((1, BRANCH_W), lambda b_, i: (0, 0)),
                  pl.BlockSpec(w_s.shape, lambda b_, i: (0, 0, 0)),
                  pl.BlockSpec((SGU_CHUNK, BRANCH_W // LANES), lambda b_, i: (0, 0))],
        out_specs=pl.BlockSpec((None, ts, BRANCH_W), lambda b_, i: (b_, i, 0)),
        out_shape=jax.ShapeDtypeStruct((B, T, BRANCH_W), BF16),
        compiler_params=_params("parallel", "parallel"),
        name="sgu_mixer",
    )(z, z, g.reshape(1, BRANCH_W), w_s, b.T)


def _merge_kernel(p_ref, d_ref, s_ref, n_ref, gz_ref, wbr_ref, wout_ref, x_ref, g1_ref,
                  n2_ref, sh_ref, sc_ref, rw_ref, xo_ref, h_ref, *, n_experts):
    D = x_ref.shape[1]
    acc = None
    for i, br in enumerate((p_ref, d_ref, s_ref, n_ref)):
        proj = jnp.dot(br[...], wbr_ref[i], preferred_element_type=F32)
        t = jax.nn.sigmoid(gz_ref[:, i * D:(i + 1) * D].astype(F32)) * proj
        acc = t if acc is None else acc + t
    mix = jnp.dot(acc.astype(BF16), wout_ref[...], preferred_element_type=F32)
    x = x_ref[...] + g1_ref[...] * mix
    xo_ref[...] = x
    y = x * lax.rsqrt(jnp.mean(x * x, axis=-1, keepdims=True) + EPS) * n2_ref[...]
    h = y * (1.0 + sc_ref[...]) + sh_ref[...]
    h_ref[:, 0:D] = h
    rw = rw_ref[...]
    h_hi = h.astype(BF16)
    h_lo = (h - h_hi.astype(F32)).astype(BF16)
    w_hi = rw.astype(BF16)
    w_lo = (rw - w_hi.astype(F32)).astype(BF16)
    lg = (jnp.dot(h_hi, w_hi, preferred_element_type=F32)
          + jnp.dot(h_lo, w_hi, preferred_element_type=F32)
          + jnp.dot(h_hi, w_lo, preferred_element_type=F32))
    lane = lax.broadcasted_iota(jnp.int32, (1, LANES), 1)
    lg = jnp.where(lane < n_experts, lg, -jnp.inf)
    ex = jnp.exp(lg - jnp.max(lg, axis=-1, keepdims=True))
    h_ref[:, D:D + LANES] = ex / jnp.sum(ex, axis=-1, keepdims=True)


def _merge(branches, zr, w_br, w_out, x2, g1, n2g, sh2, sc2, router_w):
    M, D = x2.shape
    G = g1.shape[0]
    rpg = M // G
    tm = _pick(rpg, (256,))
    E = router_w.shape[1]
    router_w = jnp.pad(router_w, ((0, 0), (0, LANES - E)))
    gb = 0
    grp = lambda i: (i * tm // rpg, 0, 0)
    row = lambda i: (i, 0)
    return pl.pallas_call(
        functools.partial(_merge_kernel, n_experts=E),
        grid=(M // tm,),
        in_specs=[pl.BlockSpec((tm, BRANCH_W), row)] * 4 + [
            pl.BlockSpec((tm, N_BRANCH * D), lambda i: (i, gb)),
            pl.BlockSpec(w_br.shape, lambda i: (0, 0, 0)),
            pl.BlockSpec(w_out.shape, lambda i: (0, 0)),
            pl.BlockSpec((tm, D), row),
            pl.BlockSpec((None, 1, D), grp),
            pl.BlockSpec((1, D), lambda i: (0, 0)),
            pl.BlockSpec((None, 1, D), grp),
            pl.BlockSpec((None, 1, D), grp),
            pl.BlockSpec(router_w.shape, lambda i: (0, 0))],
        out_specs=[pl.BlockSpec((tm, D), row), pl.BlockSpec((tm, D + LANES), row)],
        out_shape=[jax.ShapeDtypeStruct((M, D), F32), jax.ShapeDtypeStruct((M, D + LANES), F32)],
        compiler_params=_params("parallel"),
        name="merge_out",
    )(*branches, zr, w_br, w_out, x2, g1, n2g.reshape(1, D), sh2, sc2, router_w)


def _ffn_kernel(rows_ref, h_hbm, w1_ref, w3_ref, w2_ref, y_ref, w1b, w3b, w2b, xbuf, sem, *, B):
    e = pl.program_id(0)
    b = pl.program_id(1)
    C = xbuf.shape[1]
    D = w1_ref.shape[0]
    step = e * B + b
    n_exp = pl.num_programs(0)
    nsteps = n_exp * B
    slot = step % 2

    def start_row(base, slot_, j):
        pltpu.make_async_copy(h_hbm.at[pl.ds(rows_ref[base + j], 1)], xbuf.at[slot_, pl.ds(j, 1)],
                              sem.at[slot_]).start()

    def wait_slot(slot_):
        pltpu.make_async_copy(h_hbm.at[pl.ds(0, C)], xbuf.at[slot_], sem.at[slot_]).wait()

    @pl.when(step == 0)
    def _():
        def issue(j, carry):
            start_row((b * n_exp + e) * C, slot, j)
            return carry

        lax.fori_loop(0, C, issue, 0, unroll=8)

    @pl.when(b == 0)
    def _():
        w1b[...] = w1_ref[...].astype(BF16)
        w3b[...] = w3_ref[...].astype(BF16)
        w2b[...] = w2_ref[...].astype(BF16)

    last = step + 1 >= nsteps
    nxt_b = jnp.where(last, b, (b + 1) % B)
    nxt_e = jnp.where(last, e, e + (b + 1) // B)
    nxt_base = (nxt_b * n_exp + nxt_e) * C
    wait_slot(slot)
    lane = lax.broadcasted_iota(jnp.int32, (1, LANES), 1)
    rc = min(C, FFN_ROWS)
    for r in range(C // rc):
        for j in range(r * rc, (r + 1) * rc):
            start_row(nxt_base, 1 - slot, j)
        rs = pl.ds(r * rc, rc)
        xs = xbuf[slot, rs, 0:D].astype(BF16)
        gate = jnp.sum(jnp.where(lane == e, xbuf[slot, rs, D:D + LANES], 0.0), axis=-1, keepdims=True)
        a = jnp.dot(xs, w1b[...], preferred_element_type=F32)
        g = jnp.dot(xs, w3b[...], preferred_element_type=F32)
        hid = (a * jax.nn.sigmoid(a) * g).astype(BF16)
        y = jnp.dot(hid, w2b[...], preferred_element_type=F32) * gate
        y_ref[rs, :] = y.astype(y_ref.dtype)

    @pl.when(last)
    def _():
        wait_slot(1 - slot)


def _ffn(hext, idx, w1, w3, w2, layer, T):
    B, E, C = idx.shape
    D, Fd = w1.shape[2:]
    grid_spec = pltpu.PrefetchScalarGridSpec(
        num_scalar_prefetch=1,
        grid=(E, B),
        in_specs=[pl.BlockSpec(memory_space=pl.ANY),
                  pl.BlockSpec((None, None, D, Fd), lambda e, b, ix: (layer, e, 0, 0)),
                  pl.BlockSpec((None, None, D, Fd), lambda e, b, ix: (layer, e, 0, 0)),
                  pl.BlockSpec((None, None, Fd, D), lambda e, b, ix: (layer, e, 0, 0))],
        out_specs=pl.BlockSpec((None, None, C, D), lambda e, b, ix: (b, e, 0, 0)),
        scratch_shapes=[pltpu.VMEM((D, Fd), BF16), pltpu.VMEM((D, Fd), BF16), pltpu.VMEM((Fd, D), BF16),
                        pltpu.VMEM((2, C, D + LANES), F32), pltpu.SemaphoreType.DMA((2,))],
    )
    return pl.pallas_call(
        functools.partial(_ffn_kernel, B=B),
        grid_spec=grid_spec,
        out_shape=jax.ShapeDtypeStruct((B, E, C, D), BF16),
        compiler_params=_params("arbitrary", "arbitrary"),
        name="expert_ffn",
    )((idx + jnp.arange(B, dtype=jnp.int32)[:, None, None] * T).reshape(-1), hext, w1, w3, w2)


def _combine_kernel(starts_ref, x_ref, g2_ref, fg_ref, idx_ref, *rest, wb, nwin, final, spt):
    ys = rest[:-2]
    o_ref, acc_ref = rest[-2:]
    b = pl.program_id(0)
    t = pl.program_id(1)
    tt = x_ref.shape[0]
    E = idx_ref.shape[0]
    C = idx_ref.shape[1]
    tok = t * tt + lax.broadcasted_iota(jnp.int32, (tt, 1), 0)

    def window_onehot(e, wk):
        if wb == C:
            ids = idx_ref[e:e + 1, :]
        else:
            ids = idx_ref[e:e + 1, pl.ds(pl.multiple_of(wk * wb, wb), wb)]
        return jnp.where(ids == tok, 1.0, 0.0).astype(BF16)

    def window_dot(e, k, wk):
        return jnp.dot(window_onehot(e, wk), ys[e * nwin + k][...], preferred_element_type=F32)

    onehots = []
    for e in range(E):
        w0 = jnp.minimum(starts_ref[b, t * spt, e] // wb, C // wb - 1)
        onehots.append(window_onehot(e, w0))
    if wb % LANES == 0:
        acc_ref[...] = jnp.dot(jnp.concatenate(onehots, axis=1),
                               jnp.concatenate([ys[e * nwin][...] for e in range(E)], axis=0),
                               preferred_element_type=F32)
    else:
        acc = None
        for e in range(E):
            d = jnp.dot(onehots[e], ys[e * nwin][...], preferred_element_type=F32)
            acc = d if acc is None else acc + d
        acc_ref[...] = acc
    if nwin == 2:
        for e in range(E):
            s0 = starts_ref[b, t * spt, e]
            s1 = starts_ref[b, (t + 1) * spt, e]
            w0 = s0 // wb

            @pl.when((s1 > s0) & ((s1 - 1) // wb > w0))
            def _(e=e, w0=w0):
                acc_ref[...] += window_dot(e, 1, w0 + 1)

    x = x_ref[...] + g2_ref[...] * acc_ref[...]
    if final:
        x = x * lax.rsqrt(jnp.mean(x * x, axis=-1, keepdims=True) + EPS) * fg_ref[...]
    o_ref[...] = x


def _combine(x3, y, idx, starts, g2, final_g, *, final):
    B, T, D = x3.shape
    E, C = idx.shape[1:]
    if C >= COMBINE_TILE and T % COMBINE_TILE == 0:
        wb, nwin, tt = COMBINE_TILE, 2, COMBINE_TILE
    elif C >= LANES:
        wb, nwin, tt = LANES, 2, LANES
    else:
        wb, nwin, tt = C, 1, T
    nt = T // tt
    spt = tt // ROUTE_BLK
    G = g2.shape[0]

    def ymap(e, k):
        return lambda b, t, st: (b, e, jnp.minimum(st[b, t * spt, e] // wb + k, C // wb - 1), 0)

    y_specs = [pl.BlockSpec((None, None, wb, D), ymap(e, k)) for e in range(E) for k in range(nwin)]
    grid_spec = pltpu.PrefetchScalarGridSpec(
        num_scalar_prefetch=1,
        grid=(B, nt),
        in_specs=[pl.BlockSpec((None, tt, D), lambda b, t, st: (b, t, 0)),
                  pl.BlockSpec((None, 1, D), lambda b, t, st: (b * G // B, 0, 0)),
                  pl.BlockSpec((1, D), lambda b, t, st: (0, 0)),
                  pl.BlockSpec((None, E, C), lambda b, t, st: (b, 0, 0))] + y_specs,
        out_specs=pl.BlockSpec((None, tt, D), lambda b, t, st: (b, t, 0)),
        scratch_shapes=[pltpu.VMEM((tt, D), F32)],
    )
    return pl.pallas_call(
        functools.partial(_combine_kernel, wb=wb, nwin=nwin, final=final, spt=spt),
        grid_spec=grid_spec,
        out_shape=jax.ShapeDtypeStruct((B, T, D), F32),
        compiler_params=_params("parallel", "arbitrary"),
        name="moe_combine",
    )(starts, x3, g2, final_g.reshape(1, D), idx, *([y] * (E * nwin)))


def _route_kernel(aff_ref, idx_ref, starts_ref, cst_ref, st_ref, comp_ref, *, C, E):
    T = aff_ref.shape[0]
    nb = T // ROUTE_BLK

    def count(mask):
        ones = jnp.where(mask, 1.0, 0.0)
        ways = 16 if T % (16 * SUBLANES) == 0 else 1
        part = jnp.sum(ones.reshape(ways, T // ways, LANES), axis=1)
        return jnp.sum(part, axis=0, keepdims=True)

    def value_of(pattern):
        ex = jnp.right_shift(pattern, F32_MANT_BITS)
        frac = (pattern & ((1 << F32_MANT_BITS) - 1)).astype(F32) * (2.0 ** -F32_MANT_BITS)
        n = F32_EXP_BIAS - ex
        p = jnp.ones(pattern.shape, F32)
        for i in range(7):
            p = jnp.where((jnp.right_shift(n, i) & 1) == 1, p * (2.0 ** -(2 ** i)), p)
        return jnp.where(ex == 0, 0.0, p * (1.0 + frac))

    def bit_body(i, prefix):
        cand = prefix | jnp.left_shift(jnp.int32(1), 29 - i)
        cnt = count(aff_ref[...] >= value_of(cand))
        return jnp.where(cnt >= C, cand, prefix)

    vstar = value_of(lax.fori_loop(0, 30, bit_body, jnp.zeros((1, LANES), jnp.int32)))
    n_gt = count(aff_ref[...] > vstar)
    n_tie = C - n_gt

    r = lax.broadcasted_iota(jnp.int32, (ROUTE_BLK, ROUTE_BLK), 0)
    c = lax.broadcasted_iota(jnp.int32, (ROUTE_BLK, ROUTE_BLK), 1)
    lower_strict = jnp.where(c < r, 1.0, 0.0)
    lower_incl = jnp.where(c <= r, 1.0, 0.0)

    st_ref[...] = jnp.full(st_ref.shape, float(C), F32)
    ties_before = jnp.zeros((1, LANES), F32)
    chosen_before = jnp.zeros((1, LANES), F32)
    for k in range(nb):
        a = aff_ref[k * ROUTE_BLK:(k + 1) * ROUTE_BLK, :]
        gt = a > vstar
        eq = jnp.where(a == vstar, 1.0, 0.0)
        rank = jnp.dot(lower_strict, eq, preferred_element_type=F32) + ties_before
        sel = jnp.where(gt | ((eq > 0.0) & (rank < n_tie)), 1.0, 0.0)
        csum = jnp.dot(lower_incl, sel, preferred_element_type=F32) + chosen_before
        cst_ref[:, k * ROUTE_BLK:(k + 1) * ROUTE_BLK] = csum.T
        st_ref[k:k + 1, :] = chosen_before
        ties_before = ties_before + jnp.sum(eq, axis=0, keepdims=True)
        chosen_before = chosen_before + jnp.sum(sel, axis=0, keepdims=True)
    st_ref[nb:nb + 1, :] = chosen_before
    starts_ref[...] = st_ref[0:nb + 1, :].astype(jnp.int32)

    lane_i = lax.broadcasted_iota(jnp.int32, (1, LANES), 1)
    lane_f = lane_i.astype(F32)
    sub_f = lax.broadcasted_iota(jnp.int32, (LANES, 1), 0).astype(F32)
    blk_f = lax.broadcasted_iota(jnp.int32, (nb, 1), 0).astype(F32)
    comp_ref[...] = jnp.zeros(comp_ref.shape, F32)

    def expert(e, carry):
        mine = lane_i == e
        e8 = pl.multiple_of((e // SUBLANES) * SUBLANES, SUBLANES)
        my_row = lax.broadcasted_iota(jnp.int32, (SUBLANES, 1), 0) == e % SUBLANES
        begins = jnp.sum(jnp.where(mine, st_ref[0:nb, :], 0.0), axis=1, keepdims=True)
        ends = jnp.sum(jnp.where(mine, st_ref[1:nb + 1, :], 0.0), axis=1, keepdims=True)
        for k in range(nb):
            rows8 = cst_ref[pl.ds(e8, SUBLANES), k * ROUTE_BLK:(k + 1) * ROUTE_BLK]
            local = jnp.sum(jnp.where(my_row, rows8, 0.0), axis=0, keepdims=True) - begins[k:k + 1, :]
            comp_ref[:, k:k + 1] = jnp.sum(jnp.where(local <= sub_f, 1.0, 0.0), axis=1, keepdims=True)
        comp = comp_ref[...]
        for jt in range(-(-C // LANES)):
            j = lane_f + jt * LANES
            kb = jnp.sum(jnp.where(ends <= j, 1.0, 0.0), axis=0, keepdims=True)
            start = jnp.sum(jnp.where(blk_f == kb, begins, 0.0), axis=0, keepdims=True)
            onehot = jnp.where(sub_f == kb, 1.0, 0.0)
            per_rho = jnp.dot(comp, onehot, preferred_element_type=F32)
            local = jnp.sum(jnp.where(sub_f == j - start, per_rho, 0.0), axis=0, keepdims=True)
            w = min(LANES, C - jt * LANES)
            idx_ref[e, :, jt * LANES:jt * LANES + w] = jnp.broadcast_to(
                (kb * ROUTE_BLK + local)[:, 0:w].astype(jnp.int32), (SUBLANES, w))
        return carry

    lax.fori_loop(0, E, expert, 0)


def _route(hext3, E, C):
    B, T, W = hext3.shape
    nb = T // ROUTE_BLK
    assert nb <= LANES
    idx8, starts = pl.pallas_call(
        functools.partial(_route_kernel, C=C, E=E),
        grid=(B,),
        in_specs=[pl.BlockSpec((None, T, LANES), lambda b: (b, 0, W // LANES - 1))],
        out_specs=[pl.BlockSpec((None, E, SUBLANES, C), lambda b: (b, 0, 0, 0)),
                   pl.BlockSpec((None, nb + 1, LANES), lambda b: (b, 0, 0))],
        out_shape=[jax.ShapeDtypeStruct((B, E, SUBLANES, C), jnp.int32),
                   jax.ShapeDtypeStruct((B, nb + 1, LANES), jnp.int32)],
        scratch_shapes=[pltpu.VMEM((LANES, T), F32),
                        pltpu.VMEM((-(-(nb + 1) // SUBLANES) * SUBLANES, LANES), F32),
                        pltpu.VMEM((LANES, LANES), F32)],
        compiler_params=_params("parallel"),
        name="ec_route",
    )(hext3)
    return idx8[:, :, 0, :], starts[:, :, :E]


def _moe(x3, hext, w1, w3, w2, layer, g2, final_g, *, final):
    B, T, D = x3.shape
    E = w1.shape[1]
    C = EC_CAPACITY * T // E
    idx, starts = _route(hext.reshape(B, T, D + LANES), E, C)
    y = _ffn(hext, idx, w1, w3, w2, layer, T)
    return _combine(x3, y, idx, starts, g2, final_g, final=final)


_C_DK, _C_DV, _C_NK, _C_NV, _C_DQ, _C_NQ = 0, 4, 8, 12, 16, 20
ATT_COLS = 6 * BRANCH_W
KV_COLS = 4 * BRANCH_W
GATE_START = 9 * BRANCH_W
_R_POOL, _R_SU, _R_SV = 8, 9, 10


def _mixers(za, zr, zc_kv, T, p, tables, lam_init, *, latent):
    B = za.shape[0]
    sl = lambda a, c: a[:, :, c * LANES:c * LANES + BRANCH_W]
    if latent:
        k_all = jnp.concatenate([sl(zc_kv, _C_DK), sl(za, _C_DK)], axis=1)
        v_all = jnp.concatenate([sl(zc_kv, _C_DV), sl(za, _C_DV)], axis=1)
        diff = _flash(za, _C_DQ // 4, k_all, v_all, p['lam'], p['subln'], mode="diff", lam_init=lam_init)
        na = _na_attn(za, zc_kv, tables, qcol=_C_NQ, kcol=_C_NK, vcol=_C_NV)
    else:
        diff = _flash(za, _C_DQ // 4, sl(za, _C_DK), sl(za, _C_DV), p['lam'], p['subln'],
                      mode="diff", lam_init=lam_init)
        na = _flash(za, _C_NQ // 4, sl(za, _C_NK), sl(za, _C_NV), p['lam'], p['subln'], mode="dense")
    pool = _pool(zr, p['pool_w'], p['pool_scale'], col=_R_POOL)
    sgu = _sgu(zr, p['sgu_g'], p['sgu_w'], p['sgu_b'], ucol=_R_SU, vcol=_R_SV)
    M = B * T
    f2 = lambda a: a.reshape(M, a.shape[-1])
    return [f2(pool), f2(diff), f2(sgu), f2(na)]


def kernel(x, c, ctx, c_ctx, w_mod, b_mod, norm1_g, norm2_g, w_in, pool_w, pool_scale, diff_lambda,
           diff_subln_g, sgu_norm_g, sgu_w, sgu_b, na_rpb, w_br, w_out, router_w, moe_w1, moe_w3,
           moe_w2, final_g):
    B, T, D = x.shape
    Lc = ctx.shape[1]
    depth = w_in.shape[0]
    rows = T // GRID_W
    rope = _rope_tables(T)
    cvec = jnp.concatenate([c, c_ctx[None], jnp.zeros((8 - B - 1, D), F32)], axis=0)
    x_lat = x.reshape(B * T, D)
    x_ctx = ctx.reshape(B * Lc, D)
    for l in range(depth):
        last = l == depth - 1
        lam_init = 0.8 - 0.6 * math.exp(-0.3 * l)
        mod = _mod_vectors(cvec, w_mod, b_mod, l)
        mx = mod[:B].reshape(B, 6, 1, D)
        mc = mod[B:B + 1].reshape(1, 6, 1, D)
        sh1, sc1, g1, sh2, sc2, g2 = [mx[:, i] for i in range(6)]
        csh1, csc1, cg1, csh2, csc2, cg2 = [mc[:, i] for i in range(6)]
        w_in_b = w_in[l].astype(BF16)
        w_att = w_in_b[:, :ATT_COLS]
        w_rest = jnp.concatenate([w_in_b[:, GATE_START:], w_in_b[:, ATT_COLS:GATE_START]], axis=1)
        p = {'lam': diff_lambda[l], 'subln': diff_subln_g[l], 'pool_w': pool_w[l],
             'pool_scale': pool_scale[l], 'sgu_g': sgu_norm_g[l], 'sgu_w': sgu_w[l], 'sgu_b': sgu_b[l]}
        w_br_b = w_br[l].astype(BF16)
        w_out_b = w_out[l].astype(BF16)
        tables = _na_tables(na_rpb[l], rows)

        if last:
            zc_a = _norm_proj(x_ctx, norm1_g[l], csh1, csc1, w_att[:, :KV_COLS], BF16)
        else:
            zc_a = _norm_proj(x_ctx, norm1_g[l], csh1, csc1, w_att, BF16)
        zc_a = zc_a.reshape(B, Lc, -1)

        za = _norm_proj(x_lat, norm1_g[l], sh1, sc1, w_att, BF16, rope=rope,
                        rope_tiles=(_C_DK // 4, _C_DQ // 4)).reshape(B, T, ATT_COLS)
        zr = _norm_proj(x_lat, norm1_g[l], sh1, sc1, w_rest, F32).reshape(B, T, -1)
        br = _mixers(za, zr, zc_a, T, p, tables, lam_init, latent=True)
        x_lat, h2 = _merge(br, zr.reshape(B * T, -1), w_br_b, w_out_b, x_lat, g1,
                                   norm2_g[l], sh2, sc2, router_w[l])
        x_lat = _moe(x_lat.reshape(B, T, D), h2, moe_w1, moe_w3, moe_w2, l, g2,
                     final_g, final=last).reshape(B * T, D)

        if not last:
            zc_r = _norm_proj(x_ctx, norm1_g[l], csh1, csc1, w_rest, F32).reshape(B, Lc, -1)
            brc = _mixers(zc_a, zc_r, None, Lc, p, None, lam_init, latent=False)
            x_ctx, hc2 = _merge(brc, zc_r.reshape(B * Lc, -1), w_br_b, w_out_b, x_ctx, cg1,
                                     norm2_g[l], csh2, csc2, router_w[l])
            x_ctx = _moe(x_ctx.reshape(B, Lc, D), hc2, moe_w1, moe_w3, moe_w2, l, cg2,
                         final_g, final=False).reshape(B * Lc, D)
    return x_lat.reshape(B, T, D)
```

```python
import functools
import math

import numpy as np
import jax
import jax.numpy as jnp
from jax import lax
from jax.experimental import pallas as pl
from jax.experimental.pallas import tpu as pltpu

F32 = jnp.float32
BF16 = jnp.bfloat16

EPS = 1e-6
GRID_W = 64
LANES = 128
SUBLANES = 8
HALF = 64
ONES_ROWS = 16
N_BRANCH = 4
BRANCH_W = 512
POOL_WINDOWS = (2, 4, 8, 16)
POOL_HALO = 16
SGU_CHUNK = 128
NA_WIN_R = 8
NA_WIN_C = 16
NA_QROWS = 8
NA_KROWS = 16
N_EXPERTS = 16
EC_CAPACITY = 2
ROUTE_BLK = 128
FFN_ROWS = 256
COMBINE_TILE = 256
F32_MANT_BITS = 23
F32_EXP_BIAS = 127
ROPE_BASE = 10000.0
NEG = -1e30
VMEM_LIMIT = 56 * 1024 * 1024

_NT = (((1,), (1,)), ((), ()))


def _params(*sem):
    return pltpu.CompilerParams(dimension_semantics=sem, vmem_limit_bytes=VMEM_LIMIT)


def _pick(n, cands):
    for c in cands:
        if n % c == 0:
            return c
    raise ValueError(f"no tile for {n}")


def _mod_kernel(c_ref, w_ref, b_ref, o_ref):
    c = c_ref[...]
    s = c * jax.nn.sigmoid(c)
    o_ref[...] = jnp.dot(s, w_ref[...], preferred_element_type=F32,
                         precision=lax.Precision.HIGHEST) + b_ref[...]


def _mod_vectors(cvec, w, b, layer):
    L, D, N = w.shape
    tn = _pick(N, (1536, 1024, 512, 128))
    return pl.pallas_call(
        _mod_kernel,
        grid=(N // tn,),
        in_specs=[pl.BlockSpec((8, D), lambda j: (0, 0)),
                  pl.BlockSpec((None, D, tn), lambda j: (layer, 0, j)),
                  pl.BlockSpec((None, 1, tn), lambda j: (layer, 0, j))],
        out_specs=pl.BlockSpec((8, tn), lambda j: (0, j)),
        out_shape=jax.ShapeDtypeStruct((8, N), F32),
        compiler_params=_params("parallel"),
        name="mod_vectors",
    )(cvec, w, b.reshape(L, 1, N))


def _rope_store(acc, cos, sin, o_ref, col0):
    lane = lax.broadcasted_iota(jnp.int32, (1, LANES), 1)
    first = (lane % 32) < 16
    for h in range(BRANCH_W // LANES):
        cs = slice(col0 + h * LANES, col0 + (h + 1) * LANES)
        a = acc[:, cs]
        partner = jnp.where(first, pltpu.roll(a, LANES - 16, 1), pltpu.roll(a, 16, 1))
        o_ref[:, cs] = (a * cos + partner * sin).astype(o_ref.dtype)


def _norm_proj_kernel(x_ref, g_ref, sh_ref, sc_ref, w_ref, *rest, rope_tiles):
    if rope_tiles:
        cos_ref, sin_ref, o_ref, h_ref = rest
    else:
        o_ref, h_ref = rest
    j = pl.program_id(1)

    @pl.when(j == 0)
    def _():
        x = x_ref[...]
        y = x * lax.rsqrt(jnp.mean(x * x, axis=-1, keepdims=True) + EPS) * g_ref[...]
        h_ref[...] = (y * (1.0 + sc_ref[...]) + sh_ref[...]).astype(BF16)

    acc = jnp.dot(h_ref[...], w_ref[...], preferred_element_type=F32)
    o_ref[...] = acc.astype(o_ref.dtype)
    nsub = o_ref.shape[1] // BRANCH_W
    for g in rope_tiles:
        @pl.when(j == g // nsub)
        def _(g=g):
            _rope_store(acc, cos_ref[...], sin_ref[...], o_ref, (g % nsub) * BRANCH_W)


def _norm_proj(x2, g, shift, scale, w, out_dtype, rope=None, rope_tiles=()):
    M, D = x2.shape
    N = w.shape[1]
    G = shift.shape[0]
    rows_per_group = M // G
    tm = _pick(rows_per_group, (1024, 512, 256))
    tn = _pick(N, (3 * BRANCH_W, 11 * LANES, 2 * BRANCH_W, BRANCH_W))
    assert not rope_tiles or tn % BRANCH_W == 0
    in_specs = [pl.BlockSpec((tm, D), lambda i, j: (i, 0)),
                pl.BlockSpec((1, D), lambda i, j: (0, 0)),
                pl.BlockSpec((None, 1, D), lambda i, j: (i * tm // rows_per_group, 0, 0)),
                pl.BlockSpec((None, 1, D), lambda i, j: (i * tm // rows_per_group, 0, 0)),
                pl.BlockSpec((D, tn), lambda i, j: (0, j))]
    args = [x2, g.reshape(1, D), shift, scale, w]
    if rope_tiles:
        cos, sin = rope
        nt = cos.shape[0] // tm
        in_specs += [pl.BlockSpec((tm, LANES), lambda i, j: (i % nt, 0)),
                     pl.BlockSpec((tm, LANES), lambda i, j: (i % nt, 0))]
        args += [cos, sin]
    return pl.pallas_call(
        functools.partial(_norm_proj_kernel, rope_tiles=tuple(rope_tiles)),
        grid=(M // tm, N // tn),
        in_specs=in_specs,
        out_specs=pl.BlockSpec((tm, tn), lambda i, j: (i, j)),
        out_shape=jax.ShapeDtypeStruct((M, N), out_dtype),
        scratch_shapes=[pltpu.VMEM((tm, D), BF16)],
        compiler_params=_params("parallel", "arbitrary"),
        name="norm_proj",
    )(*args)


def _rope_tables(T):
    t = np.arange(T)
    row = (t // GRID_W).astype(np.float64)
    col = (t % GRID_W).astype(np.float64)
    nf = HALF // 4
    inv = ROPE_BASE ** (-np.arange(nf, dtype=np.float64) / nf)
    ar = row[:, None] * inv
    ac = col[:, None] * inv
    cos = np.concatenate([np.cos(ar), np.cos(ar), np.cos(ac), np.cos(ac)], axis=-1)
    sin = np.concatenate([-np.sin(ar), np.sin(ar), -np.sin(ac), np.sin(ac)], axis=-1)
    return (jnp.asarray(np.tile(cos, (1, 2)), F32), jnp.asarray(np.tile(sin, (1, 2)), F32))


def _flash_kernel(q_ref, k_ref, v_ref, lam_ref, g_ref, o_ref, m_ref, acc_ref, sa_ref, sb_ref, *,
                  mode, tk, lam_init):
    tq = q_ref.shape[0]
    nk = k_ref.shape[0] // tk
    ones = jnp.ones((ONES_ROWS, tk), BF16)
    sub = lax.broadcasted_iota(jnp.int32, (LANES, 1), 0)
    lo = sub < HALF
    if mode == "diff":
        lp = lam_ref[...]
        d1 = jnp.sum(lp[0:1, :] * lp[1:2, :], keepdims=True)
        d2 = jnp.sum(lp[2:3, :] * lp[3:4, :], keepdims=True)
        lam = jnp.exp(d1) - jnp.exp(d2) + lam_init
    for h in range(q_ref.shape[1] // LANES):
        cs = slice(h * LANES, (h + 1) * LANES)
        qh = q_ref[:, cs].astype(F32).T.astype(BF16)
        zero = jnp.zeros_like(qh)
        q2 = jnp.concatenate([jnp.where(lo, qh, zero), jnp.where(lo, zero, qh)], axis=1)
        q2 = q2 * jnp.asarray(HALF ** -0.5, BF16)
        m_ref[...] = jnp.full(m_ref.shape, -jnp.inf, F32)
        acc_ref[...] = jnp.zeros(acc_ref.shape, F32)

        def scores(c, s_ref, q2=q2, cs=cs):
            off = pl.multiple_of(c * tk, tk)
            s_ref[...] = jnp.dot(k_ref[pl.ds(off, tk), cs], q2, preferred_element_type=F32)

        def update(c, s_ref, cs=cs):
            off = pl.multiple_of(c * tk, tk)
            vc = jnp.concatenate([v_ref[cs, pl.ds(off, tk)], ones], axis=0)
            s = s_ref[...]
            m_old = m_ref[...]
            m_new = jnp.maximum(m_old, jnp.max(s, axis=0, keepdims=True))
            alpha = jnp.exp(m_old - m_new)
            p = jnp.exp(s - m_new).astype(BF16)
            acc_ref[...] = alpha * acc_ref[...] + jnp.dot(vc, p, preferred_element_type=F32)
            m_ref[...] = m_new

        scores(0, sa_ref)

        def body(j, carry):
            scores(2 * j + 1, sb_ref)
            update(2 * j, sa_ref)
            scores(2 * j + 2, sa_ref)
            update(2 * j + 1, sb_ref)
            return carry

        lax.fori_loop(0, (nk - 1) // 2, body, 0)
        update(nk - 1, sa_ref)
        a = acc_ref[0:LANES, :] / acc_ref[LANES:LANES + 1, :]
        a0 = a[:, :tq]
        a1 = a[:, tq:]
        if mode == "diff":
            o = a0 - lam * a1
            o = o * lax.rsqrt(jnp.mean(o * o, axis=0, keepdims=True) + EPS)
            o = o * g_ref[...] * (1.0 - lam_init)
        else:
            o = jnp.where(lo, a0, a1)
        o_ref[:, cs] = o.T.astype(o_ref.dtype)


def _flash(z, qblk, k, v, lam_p, g, *, mode, lam_init=0.0):
    B, Tq, _ = z.shape
    Tk, W = k.shape[1:]
    tq = _pick(Tq, (512, 256, 128))
    tk = next(c for c in (768, 512, 384, 256, 128) if Tk % c == 0 and (Tk // c) % 2 == 1)
    return pl.pallas_call(
        functools.partial(_flash_kernel, mode=mode, tk=tk, lam_init=lam_init),
        grid=(B, Tq // tq),
        in_specs=[pl.BlockSpec((None, tq, W), lambda b, i: (b, i, qblk)),
                  pl.BlockSpec((None, Tk, W), lambda b, i: (b, 0, 0)),
                  pl.BlockSpec((None, W, Tk), lambda b, i: (b, 0, 0)),
                  pl.BlockSpec(lam_p.shape, lambda b, i: (0, 0)),
                  pl.BlockSpec((LANES, 1), lambda b, i: (0, 0))],
        out_specs=pl.BlockSpec((None, tq, W), lambda b, i: (b, i, 0)),
        out_shape=jax.ShapeDtypeStruct((B, Tq, W), BF16),
        scratch_shapes=[pltpu.VMEM((1, 2 * tq), F32), pltpu.VMEM((LANES + ONES_ROWS, 2 * tq), F32),
                        pltpu.VMEM((tk, 2 * tq), F32), pltpu.VMEM((tk, 2 * tq), F32)],
        compiler_params=_params("parallel", "parallel"),
        name="flash_" + mode,
    )(z, k, jnp.swapaxes(v, 1, 2), lam_p, g.reshape(LANES, 1))


def _na_kernel(q_ref, k_ref, v_ref, kc_ref, vc_ref, u_ref, o_ref, b_ref, *, rows):
    rb = pl.program_id(2)
    nrb = pl.num_programs(2)
    nkk = NA_KROWS * GRID_W
    offs = _na_row_offsets(rows)
    lane1 = lax.broadcasted_iota(jnp.int32, (1, LANES), 1)

    def build(variant):
        for hh in range(2):
            for rl in range(NA_QROWS):
                for kp in range(NA_KROWS // 2):
                    a = int(offs[variant, rl, 2 * kp])
                    b2 = int(offs[variant, rl, 2 * kp + 1])
                    tile = u_ref[hh, a] if a == b2 else jnp.where(lane1 < GRID_W, u_ref[hh, a], u_ref[hh, b2])
                    b_ref[hh, rl * GRID_W:(rl + 1) * GRID_W, kp * LANES:(kp + 1) * LANES] = tile

    pl.when(rb == 0)(functools.partial(build, 0))
    if rows // NA_QROWS > 2:
        pl.when(rb == 1)(functools.partial(build, 1))
    pl.when(rb == nrb - 1)(functools.partial(build, 2))
    k0 = jnp.clip(rb * NA_QROWS - NA_WIN_R // 2, 0, rows - NA_KROWS) * GRID_W
    k0 = pl.multiple_of(k0, 4 * GRID_W)
    kl = k_ref[pl.ds(k0, nkk), :]
    vl = v_ref[pl.ds(k0, nkk), :]
    kc = kc_ref[...]
    vc = vc_ref[...]
    q = q_ref[...]
    zero = jnp.zeros_like(q)
    lane = lax.broadcasted_iota(jnp.int32, (1, LANES), 1)
    lo = lane < HALF
    scale = jnp.asarray(HALF ** -0.5, BF16)
    outs = []
    for hh in range(2):
        qm = (jnp.where(lo, q, zero) if hh == 0 else jnp.where(lo, zero, q)) * scale
        s_loc = lax.dot_general(qm, kl, _NT, preferred_element_type=F32) + b_ref[hh]
        s_ctx = lax.dot_general(qm, kc, _NT, preferred_element_type=F32)
        m = jnp.maximum(jnp.max(s_loc, axis=-1, keepdims=True), jnp.max(s_ctx, axis=-1, keepdims=True))
        p_loc = jnp.exp(s_loc - m)
        p_ctx = jnp.exp(s_ctx - m)
        l = jnp.sum(p_loc, axis=-1, keepdims=True) + jnp.sum(p_ctx, axis=-1, keepdims=True)
        o = jnp.dot(p_loc.astype(BF16), vl, preferred_element_type=F32) \
            + jnp.dot(p_ctx.astype(BF16), vc, preferred_element_type=F32)
        outs.append(o / l)
    o_ref[...] = jnp.where(lo, outs[0], outs[1]).astype(o_ref.dtype)


def _na_tables(rpb, rows):
    H = rpb.shape[0]
    c = np.arange(GRID_W)
    cstart = np.clip(c - NA_WIN_C // 2, 0, GRID_W - NA_WIN_C)
    kcol = np.arange(GRID_W)
    col_ok = (kcol[None, :] >= cstart[:, None]) & (kcol[None, :] < cstart[:, None] + NA_WIN_C)
    nc = 2 * NA_WIN_C - 1
    dc = np.where(col_ok, kcol[None, :] - c[:, None] + NA_WIN_C - 1, nc)
    cm = np.arange(nc + 1)[None, None, :] == dc[:, :, None]
    rpb_ext = jnp.pad(rpb, ((0, 0), (0, 1), (0, 1)), constant_values=NEG)
    cm2 = np.concatenate([cm, cm], axis=1)
    return jnp.einsum('hij,cqj->hicq', rpb_ext, jnp.asarray(cm2, F32),
                      precision=lax.Precision.HIGHEST)


def _na_row_offsets(rows):
    nr = 2 * NA_WIN_R - 1
    out = []
    for q0, k0, clamp in ((0, 0, True), (NA_QROWS, NA_QROWS - NA_WIN_R // 2, False),
                          (rows - NA_QROWS, rows - NA_KROWS, True)):
        r = q0 + np.arange(NA_QROWS)
        kr = k0 + np.arange(NA_KROWS)
        r0 = r - NA_WIN_R // 2
        if clamp:
            r0 = np.clip(r0, 0, rows - NA_WIN_R)
        row_ok = (kr[None, :] >= r0[:, None]) & (kr[None, :] < r0[:, None] + NA_WIN_R)
        out.append(np.where(row_ok, kr[None, :] - r[:, None] + NA_WIN_R - 1, nr))
    return np.stack(out)


def _na_attn(z, zc, tables, *, qcol, kcol, vcol):
    B, T, _ = z.shape
    Lc = zc.shape[1]
    rows = T // GRID_W
    nrb = rows // NA_QROWS
    nq = NA_QROWS * GRID_W
    nkk = NA_KROWS * GRID_W
    nhp = BRANCH_W // LANES
    assert nrb >= 2
    return pl.pallas_call(
        functools.partial(_na_kernel, rows=rows),
        grid=(nhp, B, nrb),
        in_specs=[pl.BlockSpec((None, nq, LANES), lambda h, b, r: (b, r, qcol + h)),
                  pl.BlockSpec((None, T, LANES), lambda h, b, r: (b, 0, kcol + h)),
                  pl.BlockSpec((None, T, LANES), lambda h, b, r: (b, 0, vcol + h)),
                  pl.BlockSpec((None, Lc, LANES), lambda h, b, r: (b, 0, kcol + h)),
                  pl.BlockSpec((None, Lc, LANES), lambda h, b, r: (b, 0, vcol + h)),
                  pl.BlockSpec((2, 2 * NA_WIN_R, GRID_W, LANES), lambda h, b, r: (h, 0, 0, 0))],
        out_specs=pl.BlockSpec((None, nq, LANES), lambda h, b, r: (b, r, h)),
        out_shape=jax.ShapeDtypeStruct((B, T, BRANCH_W), BF16),
        scratch_shapes=[pltpu.VMEM((2, nq, nkk), F32)],
        compiler_params=_params("arbitrary", "arbitrary", "arbitrary"),
        name="na_attn",
    )(z, z, z, zc, zc, tables)


def _pool_kernel(prev_ref, cur_ref, next_ref, w_ref, sc_ref, o_ref, *, T):
    i = pl.program_id(1)
    tt = cur_ref.shape[0]
    E = tt + 2 * POOL_HALO
    ext = jnp.concatenate([prev_ref[...], cur_ref[...], next_ref[...]], axis=0).astype(F32)
    gi = i * tt - POOL_HALO + lax.broadcasted_iota(jnp.int32, (E, 1), 0)
    ext = jnp.where((gi >= 0) & (gi < T), ext, 0.0)
    t = i * tt + lax.broadcasted_iota(jnp.int32, (tt, 1), 0)
    for g, w in enumerate(POOL_WINDOWS):
        cs = slice(g * LANES, (g + 1) * LANES)
        zg = ext[:, cs]
        s = zg + pltpu.roll(zg, 1, 0)
        half = 1
        while 2 * half < w:
            s = pltpu.roll(s, half, 0) + pltpu.roll(s, E - half, 0)
            half *= 2
        cnt = (jnp.minimum(t + w // 2, T) - jnp.maximum(t - w // 2, 0)).astype(F32)
        c0 = slice(POOL_HALO, POOL_HALO + tt)
        d = s[c0] / cnt - zg[c0]
        y = jnp.dot(d.astype(BF16), w_ref[g].astype(BF16), preferred_element_type=F32)
        o_ref[:, cs] = (y * sc_ref[:, cs]).astype(o_ref.dtype)


def _pool(z, w_pool, scale, *, col):
    B, T, _ = z.shape
    tt = _pick(T, (1024, 512, 256))
    hb = tt // POOL_HALO
    nh = T // POOL_HALO
    return pl.pallas_call(
        functools.partial(_pool_kernel, T=T),
        grid=(B, T // tt),
        in_specs=[pl.BlockSpec((None, POOL_HALO, BRANCH_W), lambda b, i: (b, jnp.maximum(i * hb - 1, 0), col)),
                  pl.BlockSpec((None, tt, BRANCH_W), lambda b, i: (b, i, col)),
                  pl.BlockSpec((None, POOL_HALO, BRANCH_W), lambda b, i: (b, jnp.minimum((i + 1) * hb, nh - 1), col)),
                  pl.BlockSpec(w_pool.shape, lambda b, i: (0, 0, 0)),
                  pl.BlockSpec((1, BRANCH_W), lambda b, i: (0, 0))],
        out_specs=pl.BlockSpec((None, tt, BRANCH_W), lambda b, i: (b, i, 0)),
        out_shape=jax.ShapeDtypeStruct((B, T, BRANCH_W), BF16),
        compiler_params=_params("parallel", "parallel"),
        name="pool_mixer",
    )(z, z, z, w_pool, scale.reshape(1, BRANCH_W))


def _sgu_kernel(u_ref, v_ref, g_ref, w_ref, bt_ref, o_ref):
    ts = u_ref.shape[0]
    u = jax.nn.gelu(u_ref[...].astype(F32))
    v = jax.nn.gelu(v_ref[...].astype(F32))
    vc = v - jnp.mean(v, axis=-1, keepdims=True)
    v = vc * lax.rsqrt(jnp.mean(vc * vc, axis=-1, keepdims=True) + EPS) * g_ref[...]
    vb = v.astype(BF16)
    for g in range(BRANCH_W // LANES):
        cs = slice(g * LANES, (g + 1) * LANES)
        wg = w_ref[g].astype(BF16)
        bg = bt_ref[:, g:g + 1]
        for ck in range(ts // SGU_CHUNK):
            rs = slice(ck * SGU_CHUNK, (ck + 1) * SGU_CHUNK)
            f = jnp.dot(wg, vb[rs, cs], preferred_element_type=F32) + bg
            o_ref[rs, cs] = (u[rs, cs] * f).astype(o_ref.dtype)


def _sgu(z, g, w_s, b, *, ucol, vcol):
    B, T, _ = z.shape
    ts = _pick(T, (1024, 512, 256, 128))
    return pl.pallas_call(
        _sgu_kernel,
        grid=(B, T // ts),
        in_specs=[pl.BlockSpec((None, ts, BRANCH_W), lambda b_, i: (b_, i, ucol)),
                  pl.BlockSpec((None, ts, BRANCH_W), lambda b_, i: (b_, i, vcol)),
                  pl.BlockSpec((1, BRANCH_W), lambda b_, i: (0, 0)),
                  pl.BlockSpec(w_s.shape, lambda b_, i: (0, 0, 0)),
                  pl.BlockSpec((SGU_CHUNK, BRANCH_W // LANES), lambda b_, i: (0, 0))],
        out_specs=pl.BlockSpec((None, ts, BRANCH_W), lambda b_, i: (b_, i, 0)),
        out_shape=jax.ShapeDtypeStruct((B, T, BRANCH_W), BF16),
        compiler_params=_params("parallel", "parallel"),
        name="sgu_mixer",
    )(z, z, g.reshape(1, BRANCH_W), w_s, b.T)


def _merge_kernel(p_ref, d_ref, s_ref, n_ref, gz_ref, wbr_ref, wout_ref, x_ref, g1_ref,
                  n2_ref, sh_ref, sc_ref, rw_ref, xo_ref, h_ref, *, n_experts):
    D = x_ref.shape[1]
    acc = None
    for i, br in enumerate((p_ref, d_ref, s_ref, n_ref)):
        proj = jnp.dot(br[...], wbr_ref[i], preferred_element_type=F32)
        t = jax.nn.sigmoid(gz_ref[:, i * D:(i + 1) * D].astype(F32)) * proj
        acc = t if acc is None else acc + t
    mix = jnp.dot(acc.astype(BF16), wout_ref[...], preferred_element_type=F32)
    x = x_ref[...] + g1_ref[...] * mix
    xo_ref[...] = x
    y = x * lax.rsqrt(jnp.mean(x * x, axis=-1, keepdims=True) + EPS) * n2_ref[...]
    h = y * (1.0 + sc_ref[...]) + sh_ref[...]
    h_ref[:, 0:D] = h
    rw = rw_ref[...]
    h_hi = h.astype(BF16)
    h_lo = (h - h_hi.astype(F32)).astype(BF16)
    w_hi = rw.astype(BF16)
    w_lo = (rw - w_hi.astype(F32)).astype(BF16)
    lg = (jnp.dot(h_hi, w_hi, preferred_element_type=F32)
          + jnp.dot(h_lo, w_hi, preferred_element_type=F32)
          + jnp.dot(h_hi, w_lo, preferred_element_type=F32))
    lane = lax.broadcasted_iota(jnp.int32, (1, LANES), 1)
    lg = jnp.where(lane < n_experts, lg, -jnp.inf)
    ex = jnp.exp(lg - jnp.max(lg, axis=-1, keepdims=True))
    h_ref[:, D:D + LANES] = ex / jnp.sum(ex, axis=-1, keepdims=True)


def _merge(branches, zr, w_br, w_out, x2, g1, n2g, sh2, sc2, router_w):
    M, D = x2.shape
    G = g1.shape[0]
    rpg = M // G
    tm = _pick(rpg, (256,))
    E = router_w.shape[1]
    router_w = jnp.pad(router_w, ((0, 0), (0, LANES - E)))
    gb = 0
    grp = lambda i: (i * tm // rpg, 0, 0)
    row = lambda i: (i, 0)
    return pl.pallas_call(
        functools.partial(_merge_kernel, n_experts=E),
        grid=(M // tm,),
        in_specs=[pl.BlockSpec((tm, BRANCH_W), row)] * 4 + [
            pl.BlockSpec((tm, N_BRANCH * D), lambda i: (i, gb)),
            pl.BlockSpec(w_br.shape, lambda i: (0, 0, 0)),
            pl.BlockSpec(w_out.shape, lambda i: (0, 0)),
            pl.BlockSpec((tm, D), row),
            pl.BlockSpec((None, 1, D), grp),
            pl.BlockSpec((1, D), lambda i: (0, 0)),
            pl.BlockSpec((None, 1, D), grp),
            pl.BlockSpec((None, 1, D), grp),
            pl.BlockSpec(router_w.shape, lambda i: (0, 0))],
        out_specs=[pl.BlockSpec((tm, D), row), pl.BlockSpec((tm, D + LANES), row)],
        out_shape=[jax.ShapeDtypeStruct((M, D), F32), jax.ShapeDtypeStruct((M, D + LANES), F32)],
        compiler_params=_params("parallel"),
        name="merge_out",
    )(*branches, zr, w_br, w_out, x2, g1, n2g.reshape(1, D), sh2, sc2, router_w)


def _ffn_kernel(rows_ref, h_hbm, w1_ref, w3_ref, w2_ref, y_ref, w1b, w3b, w2b, xbuf, sem, *, B):
    e = pl.program_id(0)
    b = pl.program_id(1)
    C = xbuf.shape[1]
    D = w1_ref.shape[0]
    step = e * B + b
    n_exp = pl.num_programs(0)
    nsteps = n_exp * B
    slot = step % 2

    def start_row(base, slot_, j):
        pltpu.make_async_copy(h_hbm.at[pl.ds(rows_ref[base + j], 1)], xbuf.at[slot_, pl.ds(j, 1)],
                              sem.at[slot_]).start()

    def wait_slot(slot_):
        pltpu.make_async_copy(h_hbm.at[pl.ds(0, C)], xbuf.at[slot_], sem.at[slot_]).wait()

    @pl.when(step == 0)
    def _():
        def issue(j, carry):
            start_row((b * n_exp + e) * C, slot, j)
            return carry

        lax.fori_loop(0, C, issue, 0, unroll=8)

    @pl.when(b == 0)
    def _():
        w1b[...] = w1_ref[...].astype(BF16)
        w3b[...] = w3_ref[...].astype(BF16)
        w2b[...] = w2_ref[...].astype(BF16)

    last = step + 1 >= nsteps
    nxt_b = jnp.where(last, b, (b + 1) % B)
    nxt_e = jnp.where(last, e, e + (b + 1) // B)
    nxt_base = (nxt_b * n_exp + nxt_e) * C
    wait_slot(slot)
    lane = lax.broadcasted_iota(jnp.int32, (1, LANES), 1)
    rc = min(C, FFN_ROWS)
    for r in range(C // rc):
        for j in range(r * rc, (r + 1) * rc):
            start_row(nxt_base, 1 - slot, j)
        rs = pl.ds(r * rc, rc)
        xs = xbuf[slot, rs, 0:D].astype(BF16)
        gate = jnp.sum(jnp.where(lane == e, xbuf[slot, rs, D:D + LANES], 0.0), axis=-1, keepdims=True)
        a = jnp.dot(xs, w1b[...], preferred_element_type=F32)
        g = jnp.dot(xs, w3b[...], preferred_element_type=F32)
        hid = (a * jax.nn.sigmoid(a) * g).astype(BF16)
        y = jnp.dot(hid, w2b[...], preferred_element_type=F32) * gate
        y_ref[rs, :] = y.astype(y_ref.dtype)

    @pl.when(last)
    def _():
        wait_slot(1 - slot)


def _ffn(hext, idx, w1, w3, w2, layer, T):
    B, E, C = idx.shape
    D, Fd = w1.shape[2:]
    grid_spec = pltpu.PrefetchScalarGridSpec(
        num_scalar_prefetch=1,
        grid=(E, B),
        in_specs=[pl.BlockSpec(memory_space=pl.ANY),
                  pl.BlockSpec((None, None, D, Fd), lambda e, b, ix: (layer, e, 0, 0)),
                  pl.BlockSpec((None, None, D, Fd), lambda e, b, ix: (layer, e, 0, 0)),
                  pl.BlockSpec((None, None, Fd, D), lambda e, b, ix: (layer, e, 0, 0))],
        out_specs=pl.BlockSpec((None, None, C, D), lambda e, b, ix: (b, e, 0, 0)),
        scratch_shapes=[pltpu.VMEM((D, Fd), BF16), pltpu.VMEM((D, Fd), BF16), pltpu.VMEM((Fd, D), BF16),
                        pltpu.VMEM((2, C, D + LANES), F32), pltpu.SemaphoreType.DMA((2,))],
    )
    return pl.pallas_call(
        functools.partial(_ffn_kernel, B=B),
        grid_spec=grid_spec,
        out_shape=jax.ShapeDtypeStruct((B, E, C, D), BF16),
        compiler_params=_params("arbitrary", "arbitrary"),
        name="expert_ffn",
    )((idx + jnp.arange(B, dtype=jnp.int32)[:, None, None] * T).reshape(-1), hext, w1, w3, w2)


def _combine_kernel(starts_ref, x_ref, g2_ref, fg_ref, idx_ref, *rest, wb, nwin, final, spt):
    ys = rest[:-2]
    o_ref, acc_ref = rest[-2:]
    b = pl.program_id(0)
    t = pl.program_id(1)
    tt = x_ref.shape[0]
    E = idx_ref.shape[0]
    C = idx_ref.shape[1]
    tok = t * tt + lax.broadcasted_iota(jnp.int32, (tt, 1), 0)

    def window_onehot(e, wk):
        if wb == C:
            ids = idx_ref[e:e + 1, :]
        else:
            ids = idx_ref[e:e + 1, pl.ds(pl.multiple_of(wk * wb, wb), wb)]
        return jnp.where(ids == tok, 1.0, 0.0).astype(BF16)

    def window_dot(e, k, wk):
        return jnp.dot(window_onehot(e, wk), ys[e * nwin + k][...], preferred_element_type=F32)

    onehots = []
    for e in range(E):
        w0 = jnp.minimum(starts_ref[b, t * spt, e] // wb, C // wb - 1)
        onehots.append(window_onehot(e, w0))
    if wb % LANES == 0:
        acc_ref[...] = jnp.dot(jnp.concatenate(onehots, axis=1),
                               jnp.concatenate([ys[e * nwin][...] for e in range(E)], axis=0),
                               preferred_element_type=F32)
    else:
        acc = None
        for e in range(E):
            d = jnp.dot(onehots[e], ys[e * nwin][...], preferred_element_type=F32)
            acc = d if acc is None else acc + d
        acc_ref[...] = acc
    if nwin == 2:
        for e in range(E):
            s0 = starts_ref[b, t * spt, e]
            s1 = starts_ref[b, (t + 1) * spt, e]
            w0 = s0 // wb

            @pl.when((s1 > s0) & ((s1 - 1) // wb > w0))
            def _(e=e, w0=w0):
                acc_ref[...] += window_dot(e, 1, w0 + 1)

    x = x_ref[...] + g2_ref[...] * acc_ref[...]
    if final:
        x = x * lax.rsqrt(jnp.mean(x * x, axis=-1, keepdims=True) + EPS) * fg_ref[...]
    o_ref[...] = x


def _combine(x3, y, idx, starts, g2, final_g, *, final):
    B, T, D = x3.shape
    E, C = idx.shape[1:]
    if C >= COMBINE_TILE and T % COMBINE_TILE == 0:
        wb, nwin, tt = COMBINE_TILE, 2, COMBINE_TILE
    elif C >= LANES:
        wb, nwin, tt = LANES, 2, LANES
    else:
        wb, nwin, tt = C, 1, T
    nt = T // tt
    spt = tt // ROUTE_BLK
    G = g2.shape[0]

    def ymap(e, k):
        return lambda b, t, st: (b, e, jnp.minimum(st[b, t * spt, e] // wb + k, C // wb - 1), 0)

    y_specs = [pl.BlockSpec((None, None, wb, D), ymap(e, k)) for e in range(E) for k in range(nwin)]
    grid_spec = pltpu.PrefetchScalarGridSpec(
        num_scalar_prefetch=1,
        grid=(B, nt),
        in_specs=[pl.BlockSpec((None, tt, D), lambda b, t, st: (b, t, 0)),
                  pl.BlockSpec((None, 1, D), lambda b, t, st: (b * G // B, 0, 0)),
                  pl.BlockSpec((1, D), lambda b, t, st: (0, 0)),
                  pl.BlockSpec((None, E, C), lambda b, t, st: (b, 0, 0))] + y_specs,
        out_specs=pl.BlockSpec((None, tt, D), lambda b, t, st: (b, t, 0)),
        scratch_shapes=[pltpu.VMEM((tt, D), F32)],
    )
    return pl.pallas_call(
        functools.partial(_combine_kernel, wb=wb, nwin=nwin, final=final, spt=spt),
        grid_spec=grid_spec,
        out_shape=jax.ShapeDtypeStruct((B, T, D), F32),
        compiler_params=_params("parallel", "arbitrary"),
        name="moe_combine",
    )(starts, x3, g2, final_g.reshape(1, D), idx, *([y] * (E * nwin)))


def _route_kernel(aff_ref, idx_ref, starts_ref, cst_ref, st_ref, comp_ref, pk_ref, *, C, E):
    T = aff_ref.shape[0]
    nb = T // ROUTE_BLK

    groups = LANES // E
    tg = T // groups
    packed = aff_ref[0:tg, :]
    for g in range(1, groups):
        packed = packed + pltpu.roll(aff_ref[g * tg:(g + 1) * tg, :], g * E, 1)
    pk_ref[...] = packed

    def count(mask):
        ones = jnp.where(mask, 1.0, 0.0)
        ways = 16 if tg % (16 * SUBLANES) == 0 else 1
        part = jnp.sum(ones.reshape(ways, tg // ways, LANES), axis=1)
        per_lane = jnp.sum(part, axis=0, keepdims=True)
        total = per_lane
        for g in range(1, groups):
            total = total + pltpu.roll(per_lane, g * E, 1)
        return total

    def value_of(pattern):
        ex = jnp.right_shift(pattern, F32_MANT_BITS)
        frac = (pattern & ((1 << F32_MANT_BITS) - 1)).astype(F32) * (2.0 ** -F32_MANT_BITS)
        n = F32_EXP_BIAS - ex
        p = jnp.ones(pattern.shape, F32)
        for i in range(7):
            p = jnp.where((jnp.right_shift(n, i) & 1) == 1, p * (2.0 ** -(2 ** i)), p)
        return jnp.where(ex == 0, 0.0, p * (1.0 + frac))

    def bit_body(i, prefix):
        cand = prefix | jnp.left_shift(jnp.int32(1), 29 - i)
        cnt = count(pk_ref[...] >= value_of(cand))
        return jnp.where(cnt >= C, cand, prefix)

    vstar = value_of(lax.fori_loop(0, 30, bit_body, jnp.zeros((1, LANES), jnp.int32)))
    n_gt = count(pk_ref[...] > vstar)
    n_tie = C - n_gt

    r = lax.broadcasted_iota(jnp.int32, (ROUTE_BLK, ROUTE_BLK), 0)
    c = lax.broadcasted_iota(jnp.int32, (ROUTE_BLK, ROUTE_BLK), 1)
    lower_strict = jnp.where(c < r, 1.0, 0.0)
    lower_incl = jnp.where(c <= r, 1.0, 0.0)

    st_ref[...] = jnp.full(st_ref.shape, float(C), F32)
    ties_before = jnp.zeros((1, LANES), F32)
    chosen_before = jnp.zeros((1, LANES), F32)
    for k in range(nb):
        a = aff_ref[k * ROUTE_BLK:(k + 1) * ROUTE_BLK, :]
        gt = a > vstar
        eq = jnp.where(a == vstar, 1.0, 0.0)
        rank = jnp.dot(lower_strict, eq, preferred_element_type=F32) + ties_before
        sel = jnp.where(gt | ((eq > 0.0) & (rank < n_tie)), 1.0, 0.0)
        csum = jnp.dot(lower_incl, sel, preferred_element_type=F32) + chosen_before
        cst_ref[:, k * ROUTE_BLK:(k + 1) * ROUTE_BLK] = csum.T
        st_ref[k:k + 1, :] = chosen_before
        ties_before = ties_before + jnp.sum(eq, axis=0, keepdims=True)
        chosen_before = chosen_before + jnp.sum(sel, axis=0, keepdims=True)
    st_ref[nb:nb + 1, :] = chosen_before
    starts_ref[...] = st_ref[0:nb + 1, :].astype(jnp.int32)

    lane_i = lax.broadcasted_iota(jnp.int32, (1, LANES), 1)
    lane_f = lane_i.astype(F32)
    sub_f = lax.broadcasted_iota(jnp.int32, (LANES, 1), 0).astype(F32)
    blk_f = lax.broadcasted_iota(jnp.int32, (nb, 1), 0).astype(F32)
    comp_ref[...] = jnp.zeros(comp_ref.shape, F32)

    def expert(e, carry):
        mine = lane_i == e
        e8 = pl.multiple_of((e // SUBLANES) * SUBLANES, SUBLANES)
        my_row = lax.broadcasted_iota(jnp.int32, (SUBLANES, 1), 0) == e % SUBLANES
        begins = jnp.sum(jnp.where(mine, st_ref[0:nb, :], 0.0), axis=1, keepdims=True)
        ends = jnp.sum(jnp.where(mine, st_ref[1:nb + 1, :], 0.0), axis=1, keepdims=True)
        for k in range(nb):
            rows8 = cst_ref[pl.ds(e8, SUBLANES), k * ROUTE_BLK:(k + 1) * ROUTE_BLK]
            local = jnp.sum(jnp.where(my_row, rows8, 0.0), axis=0, keepdims=True) - begins[k:k + 1, :]
            comp_ref[:, k:k + 1] = jnp.sum(jnp.where(local <= sub_f, 1.0, 0.0), axis=1, keepdims=True)
        comp = comp_ref[...]
        for jt in range(-(-C // LANES)):
            j = lane_f + jt * LANES
            kb = jnp.sum(jnp.where(ends <= j, 1.0, 0.0), axis=0, keepdims=True)
            start = jnp.sum(jnp.where(blk_f == kb, begins, 0.0), axis=0, keepdims=True)
            onehot = jnp.where(sub_f == kb, 1.0, 0.0)
            per_rho = jnp.dot(comp, onehot, preferred_element_type=F32)
            local = jnp.sum(jnp.where(sub_f == j - start, per_rho, 0.0), axis=0, keepdims=True)
            w = min(LANES, C - jt * LANES)
            idx_ref[e, :, jt * LANES:jt * LANES + w] = jnp.broadcast_to(
                (kb * ROUTE_BLK + local)[:, 0:w].astype(jnp.int32), (SUBLANES, w))
        return carry

    lax.fori_loop(0, E, expert, 0)


def _route(hext3, E, C):
    B, T, W = hext3.shape
    nb = T // ROUTE_BLK
    assert nb <= LANES and LANES % E == 0 and T % (LANES // E * SUBLANES) == 0
    idx8, starts = pl.pallas_call(
        functools.partial(_route_kernel, C=C, E=E),
        grid=(B,),
        in_specs=[pl.BlockSpec((None, T, LANES), lambda b: (b, 0, W // LANES - 1))],
        out_specs=[pl.BlockSpec((None, E, SUBLANES, C), lambda b: (b, 0, 0, 0)),
                   pl.BlockSpec((None, nb + 1, LANES), lambda b: (b, 0, 0))],
        out_shape=[jax.ShapeDtypeStruct((B, E, SUBLANES, C), jnp.int32),
                   jax.ShapeDtypeStruct((B, nb + 1, LANES), jnp.int32)],
        scratch_shapes=[pltpu.VMEM((LANES, T), F32),
                        pltpu.VMEM((-(-(nb + 1) // SUBLANES) * SUBLANES, LANES), F32),
                        pltpu.VMEM((LANES, LANES), F32),
                        pltpu.VMEM((T * E // LANES, LANES), F32)],
        compiler_params=_params("parallel"),
        name="ec_route",
    )(hext3)
    return idx8[:, :, 0, :], starts[:, :, :E]


def _moe(x3, hext, w1, w3, w2, layer, g2, final_g, *, final):
    B, T, D = x3.shape
    E = w1.shape[1]
    C = EC_CAPACITY * T // E
    idx, starts = _route(hext.reshape(B, T, D + LANES), E, C)
    y = _ffn(hext, idx, w1, w3, w2, layer, T)
    return _combine(x3, y, idx, starts, g2, final_g, final=final)


_C_DK, _C_DV, _C_NK, _C_NV, _C_DQ, _C_NQ = 0, 4, 8, 12, 16, 20
ATT_COLS = 6 * BRANCH_W
KV_COLS = 4 * BRANCH_W
GATE_START = 9 * BRANCH_W
_R_POOL, _R_SU, _R_SV = 8, 9, 10


def _mixers(za, zr, zc_kv, T, p, tables, lam_init, *, latent):
    B = za.shape[0]
    sl = lambda a, c: a[:, :, c * LANES:c * LANES + BRANCH_W]
    if latent:
        k_all = jnp.concatenate([sl(zc_kv, _C_DK), sl(za, _C_DK)], axis=1)
        v_all = jnp.concatenate([sl(zc_kv, _C_DV), sl(za, _C_DV)], axis=1)
        diff = _flash(za, _C_DQ // 4, k_all, v_all, p['lam'], p['subln'], mode="diff", lam_init=lam_init)
        na = _na_attn(za, zc_kv, tables, qcol=_C_NQ, kcol=_C_NK, vcol=_C_NV)
    else:
        diff = _flash(za, _C_DQ // 4, sl(za, _C_DK), sl(za, _C_DV), p['lam'], p['subln'],
                      mode="diff", lam_init=lam_init)
        na = _flash(za, _C_NQ // 4, sl(za, _C_NK), sl(za, _C_NV), p['lam'], p['subln'], mode="dense")
    pool = _pool(zr, p['pool_w'], p['pool_scale'], col=_R_POOL)
    sgu = _sgu(zr, p['sgu_g'], p['sgu_w'], p['sgu_b'], ucol=_R_SU, vcol=_R_SV)
    M = B * T
    f2 = lambda a: a.reshape(M, a.shape[-1])
    return [f2(pool), f2(diff), f2(sgu), f2(na)]


def kernel(x, c, ctx, c_ctx, w_mod, b_mod, norm1_g, norm2_g, w_in, pool_w, pool_scale, diff_lambda,
           diff_subln_g, sgu_norm_g, sgu_w, sgu_b, na_rpb, w_br, w_out, router_w, moe_w1, moe_w3,
           moe_w2, final_g):
    B, T, D = x.shape
    Lc = ctx.shape[1]
    depth = w_in.shape[0]
    rows = T // GRID_W
    rope = _rope_tables(T)
    cvec = jnp.concatenate([c, c_ctx[None], jnp.zeros((8 - B - 1, D), F32)], axis=0)
    x_lat = x.reshape(B * T, D)
    x_ctx = ctx.reshape(B * Lc, D)
    for l in range(depth):
        last = l == depth - 1
        lam_init = 0.8 - 0.6 * math.exp(-0.3 * l)
        mod = _mod_vectors(cvec, w_mod, b_mod, l)
        mx = mod[:B].reshape(B, 6, 1, D)
        mc = mod[B:B + 1].reshape(1, 6, 1, D)
        sh1, sc1, g1, sh2, sc2, g2 = [mx[:, i] for i in range(6)]
        csh1, csc1, cg1, csh2, csc2, cg2 = [mc[:, i] for i in range(6)]
        w_in_b = w_in[l].astype(BF16)
        w_att = w_in_b[:, :ATT_COLS]
        w_rest = jnp.concatenate([w_in_b[:, GATE_START:], w_in_b[:, ATT_COLS:GATE_START]], axis=1)
        p = {'lam': diff_lambda[l], 'subln': diff_subln_g[l], 'pool_w': pool_w[l],
             'pool_scale': pool_scale[l], 'sgu_g': sgu_norm_g[l], 'sgu_w': sgu_w[l], 'sgu_b': sgu_b[l]}
        w_br_b = w_br[l].astype(BF16)
        w_out_b = w_out[l].astype(BF16)
        tables = _na_tables(na_rpb[l], rows)

        if last:
            zc_a = _norm_proj(x_ctx, norm1_g[l], csh1, csc1, w_att[:, :KV_COLS], BF16)
        else:
            zc_a = _norm_proj(x_ctx, norm1_g[l], csh1, csc1, w_att, BF16)
        zc_a = zc_a.reshape(B, Lc, -1)

        za = _norm_proj(x_lat, norm1_g[l], sh1, sc1, w_att, BF16, rope=rope,
                        rope_tiles=(_C_DK // 4, _C_DQ // 4)).reshape(B, T, ATT_COLS)
        zr = _norm_proj(x_lat, norm1_g[l], sh1, sc1, w_rest, F32).reshape(B, T, -1)
        br = _mixers(za, zr, zc_a, T, p, tables, lam_init, latent=True)
        x_lat, h2 = _merge(br, zr.reshape(B * T, -1), w_br_b, w_out_b, x_lat, g1,
                                   norm2_g[l], sh2, sc2, router_w[l])
        x_lat = _moe(x_lat.reshape(B, T, D), h2, moe_w1, moe_w3, moe_w2, l, g2,
                     final_g, final=last).reshape(B * T, D)

        if not last:
            zc_r = _norm_proj(x_ctx, norm1_g[l], csh1, csc1, w_rest, F32).reshape(B, Lc, -1)
            brc = _mixers(zc_a, zc_r, None, Lc, p, None, lam_init, latent=False)
            x_ctx, hc2 = _merge(brc, zc_r.reshape(B * Lc, -1), w_br_b, w_out_b, x_ctx, cg1,
                                     norm2_g[l], csh2, csc2, router_w[l])
            x_ctx = _moe(x_ctx.reshape(B, Lc, D), hc2, moe_w1, moe_w3, moe_w2, l, cg2,
                         final_g, final=False).reshape(B * Lc, D)
    return x_lat.reshape(B, T, D)
```

```python
import functools
import math

import numpy as np
import jax
import jax.numpy as jnp
from jax import lax
from jax.experimental import pallas as pl
from jax.experimental.pallas import tpu as pltpu

F32 = jnp.float32
BF16 = jnp.bfloat16

EPS = 1e-6
GRID_W = 64
LANES = 128
SUBLANES = 8
HALF = 64
ONES_ROWS = 16
N_BRANCH = 4
BRANCH_W = 512
POOL_WINDOWS = (2, 4, 8, 16)
POOL_HALO = 16
SGU_CHUNK = 128
NA_WIN_R = 8
NA_WIN_C = 16
NA_QROWS = 8
NA_KROWS = 16
N_EXPERTS = 16
EC_CAPACITY = 2
ROUTE_BLK = 128
FFN_ROWS = 256
COMBINE_TILE = 256
F32_MANT_BITS = 23
F32_EXP_BIAS = 127
ROPE_BASE = 10000.0
NEG = -1e30
VMEM_LIMIT = 56 * 1024 * 1024

_NT = (((1,), (1,)), ((), ()))


def _params(*sem):
    return pltpu.CompilerParams(dimension_semantics=sem, vmem_limit_bytes=VMEM_LIMIT)


def _pick(n, cands):
    for c in cands:
        if n % c == 0:
            return c
    raise ValueError(f"no tile for {n}")


def _mod_kernel(c_ref, w_ref, b_ref, o_ref):
    c = c_ref[...]
    s = c * jax.nn.sigmoid(c)
    o_ref[...] = jnp.dot(s, w_ref[...], preferred_element_type=F32,
                         precision=lax.Precision.HIGHEST) + b_ref[...]


def _mod_vectors(cvec, w, b, layer):
    L, D, N = w.shape
    tn = _pick(N, (1536, 1024, 512, 128))
    return pl.pallas_call(
        _mod_kernel,
        grid=(N // tn,),
        in_specs=[pl.BlockSpec((8, D), lambda j: (0, 0)),
                  pl.BlockSpec((None, D, tn), lambda j: (layer, 0, j)),
                  pl.BlockSpec((None, 1, tn), lambda j: (layer, 0, j))],
        out_specs=pl.BlockSpec((8, tn), lambda j: (0, j)),
        out_shape=jax.ShapeDtypeStruct((8, N), F32),
        compiler_params=_params("parallel"),
        name="mod_vectors",
    )(cvec, w, b.reshape(L, 1, N))


def _rope_store(acc, cos, sin, o_ref, col0):
    lane = lax.broadcasted_iota(jnp.int32, (1, LANES), 1)
    first = (lane % 32) < 16
    for h in range(BRANCH_W // LANES):
        cs = slice(col0 + h * LANES, col0 + (h + 1) * LANES)
        a = acc[:, cs]
        partner = jnp.where(first, pltpu.roll(a, LANES - 16, 1), pltpu.roll(a, 16, 1))
        o_ref[:, cs] = (a * cos + partner * sin).astype(o_ref.dtype)


def _norm_proj_kernel(x_ref, g_ref, sh_ref, sc_ref, w_ref, *rest, rope_tiles):
    if rope_tiles:
        cos_ref, sin_ref, o_ref, h_ref = rest
    else:
        o_ref, h_ref = rest
    j = pl.program_id(1)

    @pl.when(j == 0)
    def _():
        x = x_ref[...]
        y = x * lax.rsqrt(jnp.mean(x * x, axis=-1, keepdims=True) + EPS) * g_ref[...]
        h_ref[...] = (y * (1.0 + sc_ref[...]) + sh_ref[...]).astype(BF16)

    acc = jnp.dot(h_ref[...], w_ref[...], preferred_element_type=F32)
    o_ref[...] = acc.astype(o_ref.dtype)
    nsub = o_ref.shape[1] // BRANCH_W
    for g in rope_tiles:
        @pl.when(j == g // nsub)
        def _(g=g):
            _rope_store(acc, cos_ref[...], sin_ref[...], o_ref, (g % nsub) * BRANCH_W)


def _norm_proj(x2, g, shift, scale, w, out_dtype, rope=None, rope_tiles=()):
    M, D = x2.shape
    N = w.shape[1]
    G = shift.shape[0]
    rows_per_group = M // G
    tm = _pick(rows_per_group, (1024, 512, 256))
    tn = _pick(N, (3 * BRANCH_W, 11 * LANES, 2 * BRANCH_W, BRANCH_W))
    assert not rope_tiles or tn % BRANCH_W == 0
    in_specs = [pl.BlockSpec((tm, D), lambda i, j: (i, 0)),
                pl.BlockSpec((1, D), lambda i, j: (0, 0)),
                pl.BlockSpec((None, 1, D), lambda i, j: (i * tm // rows_per_group, 0, 0)),
                pl.BlockSpec((None, 1, D), lambda i, j: (i * tm // rows_per_group, 0, 0)),
                pl.BlockSpec((D, tn), lambda i, j: (0, j))]
    args = [x2, g.reshape(1, D), shift, scale, w]
    if rope_tiles:
        cos, sin = rope
        nt = cos.shape[0] // tm
        in_specs += [pl.BlockSpec((tm, LANES), lambda i, j: (i % nt, 0)),
                     pl.BlockSpec((tm, LANES), lambda i, j: (i % nt, 0))]
        args += [cos, sin]
    return pl.pallas_call(
        functools.partial(_norm_proj_kernel, rope_tiles=tuple(rope_tiles)),
        grid=(M // tm, N // tn),
        in_specs=in_specs,
        out_specs=pl.BlockSpec((tm, tn), lambda i, j: (i, j)),
        out_shape=jax.ShapeDtypeStruct((M, N), out_dtype),
        scratch_shapes=[pltpu.VMEM((tm, D), BF16)],
        compiler_params=_params("parallel", "arbitrary"),
        name="norm_proj",
    )(*args)


def _rope_tables(T):
    t = np.arange(T)
    row = (t // GRID_W).astype(np.float64)
    col = (t % GRID_W).astype(np.float64)
    nf = HALF // 4
    inv = ROPE_BASE ** (-np.arange(nf, dtype=np.float64) / nf)
    ar = row[:, None] * inv
    ac = col[:, None] * inv
    cos = np.concatenate([np.cos(ar), np.cos(ar), np.cos(ac), np.cos(ac)], axis=-1)
    sin = np.concatenate([-np.sin(ar), np.sin(ar), -np.sin(ac), np.sin(ac)], axis=-1)
    return (jnp.asarray(np.tile(cos, (1, 2)), F32), jnp.asarray(np.tile(sin, (1, 2)), F32))


def _flash_kernel(q_ref, k_ref, v_ref, lam_ref, g_ref, o_ref, m_ref, acc_ref, sa_ref, sb_ref, *,
                  mode, tk, lam_init):
    tq = q_ref.shape[0]
    nk = k_ref.shape[0] // tk
    ones = jnp.ones((ONES_ROWS, tk), BF16)
    sub = lax.broadcasted_iota(jnp.int32, (LANES, 1), 0)
    lo = sub < HALF
    if mode == "diff":
        lp = lam_ref[...]
        d1 = jnp.sum(lp[0:1, :] * lp[1:2, :], keepdims=True)
        d2 = jnp.sum(lp[2:3, :] * lp[3:4, :], keepdims=True)
        lam = jnp.exp(d1) - jnp.exp(d2) + lam_init
    for h in range(q_ref.shape[1] // LANES):
        cs = slice(h * LANES, (h + 1) * LANES)
        qh = q_ref[:, cs].astype(F32).T.astype(BF16)
        zero = jnp.zeros_like(qh)
        q2 = jnp.concatenate([jnp.where(lo, qh, zero), jnp.where(lo, zero, qh)], axis=1)
        q2 = q2 * jnp.asarray(HALF ** -0.5, BF16)
        m_ref[...] = jnp.full(m_ref.shape, -jnp.inf, F32)
        acc_ref[...] = jnp.zeros(acc_ref.shape, F32)

        def scores(c, s_ref, q2=q2, cs=cs):
            off = pl.multiple_of(c * tk, tk)
            s_ref[...] = jnp.dot(k_ref[pl.ds(off, tk), cs], q2, preferred_element_type=F32)

        def update(c, s_ref, cs=cs):
            off = pl.multiple_of(c * tk, tk)
            vc = jnp.concatenate([v_ref[cs, pl.ds(off, tk)], ones], axis=0)
            s = s_ref[...]
            m_old = m_ref[...]
            m_new = jnp.maximum(m_old, jnp.max(s, axis=0, keepdims=True))
            alpha = jnp.exp(m_old - m_new)
            p = jnp.exp(s - m_new).astype(BF16)
            acc_ref[...] = alpha * acc_ref[...] + jnp.dot(vc, p, preferred_element_type=F32)
            m_ref[...] = m_new

        scores(0, sa_ref)

        def body(j, carry):
            scores(2 * j + 1, sb_ref)
            update(2 * j, sa_ref)
            scores(2 * j + 2, sa_ref)
            update(2 * j + 1, sb_ref)
            return carry

        lax.fori_loop(0, (nk - 1) // 2, body, 0)
        update(nk - 1, sa_ref)
        a = acc_ref[0:LANES, :] / acc_ref[LANES:LANES + 1, :]
        a0 = a[:, :tq]
        a1 = a[:, tq:]
        if mode == "diff":
            o = a0 - lam * a1
            o = o * lax.rsqrt(jnp.mean(o * o, axis=0, keepdims=True) + EPS)
            o = o * g_ref[...] * (1.0 - lam_init)
        else:
            o = jnp.where(lo, a0, a1)
        o_ref[:, cs] = o.T.astype(o_ref.dtype)


def _flash(z, qblk, k, v, lam_p, g, *, mode, lam_init=0.0):
    B, Tq, _ = z.shape
    Tk, W = k.shape[1:]
    tq = _pick(Tq, (512, 256, 128))
    tk = next(c for c in (768, 512, 384, 256, 128) if Tk % c == 0 and (Tk // c) % 2 == 1)
    return pl.pallas_call(
        functools.partial(_flash_kernel, mode=mode, tk=tk, lam_init=lam_init),
        grid=(B, Tq // tq),
        in_specs=[pl.BlockSpec((None, tq, W), lambda b, i: (b, i, qblk)),
                  pl.BlockSpec((None, Tk, W), lambda b, i: (b, 0, 0)),
                  pl.BlockSpec((None, W, Tk), lambda b, i: (b, 0, 0)),
                  pl.BlockSpec(lam_p.shape, lambda b, i: (0, 0)),
                  pl.BlockSpec((LANES, 1), lambda b, i: (0, 0))],
        out_specs=pl.BlockSpec((None, tq, W), lambda b, i: (b, i, 0)),
        out_shape=jax.ShapeDtypeStruct((B, Tq, W), BF16),
        scratch_shapes=[pltpu.VMEM((1, 2 * tq), F32), pltpu.VMEM((LANES + ONES_ROWS, 2 * tq), F32),
                        pltpu.VMEM((tk, 2 * tq), F32), pltpu.VMEM((tk, 2 * tq), F32)],
        compiler_params=_params("parallel", "parallel"),
        name="flash_" + mode,
    )(z, k, jnp.swapaxes(v, 1, 2), lam_p, g.reshape(LANES, 1))


def _na_kernel(q_ref, k_ref, v_ref, kc_ref, vc_ref, u_ref, o_ref, b_ref, *, rows):
    rb = pl.program_id(2)
    nrb = pl.num_programs(2)
    nkk = NA_KROWS * GRID_W
    offs = _na_row_offsets(rows)
    lane1 = lax.broadcasted_iota(jnp.int32, (1, LANES), 1)

    def build(variant):
        for hh in range(2):
            for rl in range(NA_QROWS):
                for kp in range(NA_KROWS // 2):
                    a = int(offs[variant, rl, 2 * kp])
                    b2 = int(offs[variant, rl, 2 * kp + 1])
                    tile = u_ref[hh, a] if a == b2 else jnp.where(lane1 < GRID_W, u_ref[hh, a], u_ref[hh, b2])
                    b_ref[hh, rl * GRID_W:(rl + 1) * GRID_W, kp * LANES:(kp + 1) * LANES] = tile

    pl.when(rb == 0)(functools.partial(build, 0))
    if rows // NA_QROWS > 2:
        pl.when(rb == 1)(functools.partial(build, 1))
    pl.when(rb == nrb - 1)(functools.partial(build, 2))
    k0 = jnp.clip(rb * NA_QROWS - NA_WIN_R // 2, 0, rows - NA_KROWS) * GRID_W
    k0 = pl.multiple_of(k0, 4 * GRID_W)
    kl = k_ref[pl.ds(k0, nkk), :]
    vl = v_ref[pl.ds(k0, nkk), :]
    kc = kc_ref[...]
    vc = vc_ref[...]
    q = q_ref[...]
    zero = jnp.zeros_like(q)
    lane = lax.broadcasted_iota(jnp.int32, (1, LANES), 1)
    lo = lane < HALF
    scale = jnp.asarray(HALF ** -0.5, BF16)
    vl1 = jnp.concatenate([vl, jnp.ones((nkk, LANES), BF16)], axis=1)
    vc1 = jnp.concatenate([vc, jnp.ones((kc.shape[0], LANES), BF16)], axis=1)
    outs = []
    for hh in range(2):
        qm = (jnp.where(lo, q, zero) if hh == 0 else jnp.where(lo, zero, q)) * scale
        s_loc = lax.dot_general(qm, kl, _NT, preferred_element_type=F32) + b_ref[hh]
        s_ctx = lax.dot_general(qm, kc, _NT, preferred_element_type=F32)
        m = jnp.maximum(jnp.max(s_loc, axis=-1, keepdims=True), jnp.max(s_ctx, axis=-1, keepdims=True))
        o = jnp.dot(jnp.exp(s_loc - m).astype(BF16), vl1, preferred_element_type=F32) \
            + jnp.dot(jnp.exp(s_ctx - m).astype(BF16), vc1, preferred_element_type=F32)
        outs.append(o[:, 0:LANES] / o[:, LANES:2 * LANES])
    o_ref[...] = jnp.where(lo, outs[0], outs[1]).astype(o_ref.dtype)


def _na_tables(rpb):
    H = rpb.shape[0]
    c = np.arange(GRID_W)
    cstart = np.clip(c - NA_WIN_C // 2, 0, GRID_W - NA_WIN_C)
    kcol = np.arange(GRID_W)
    col_ok = (kcol[None, :] >= cstart[:, None]) & (kcol[None, :] < cstart[:, None] + NA_WIN_C)
    nc = 2 * NA_WIN_C - 1
    dc = np.where(col_ok, kcol[None, :] - c[:, None] + NA_WIN_C - 1, nc)
    cm = np.arange(nc + 1)[None, None, :] == dc[:, :, None]
    rpb_ext = jnp.pad(rpb, ((0, 0), (0, 1), (0, 1)), constant_values=NEG)
    cm2 = np.concatenate([cm, cm], axis=1)
    return jnp.einsum('hij,cqj->hicq', rpb_ext, jnp.asarray(cm2, F32),
                      precision=lax.Precision.HIGHEST)


def _na_row_offsets(rows):
    nr = 2 * NA_WIN_R - 1
    out = []
    for q0, k0, clamp in ((0, 0, True), (NA_QROWS, NA_QROWS - NA_WIN_R // 2, False),
                          (rows - NA_QROWS, rows - NA_KROWS, True)):
        r = q0 + np.arange(NA_QROWS)
        kr = k0 + np.arange(NA_KROWS)
        r0 = r - NA_WIN_R // 2
        if clamp:
            r0 = np.clip(r0, 0, rows - NA_WIN_R)
        row_ok = (kr[None, :] >= r0[:, None]) & (kr[None, :] < r0[:, None] + NA_WIN_R)
        out.append(np.where(row_ok, kr[None, :] - r[:, None] + NA_WIN_R - 1, nr))
    return np.stack(out)


def _na_attn(z, zc, tables, *, qcol, kcol, vcol):
    B, T, _ = z.shape
    Lc = zc.shape[1]
    rows = T // GRID_W
    nrb = rows // NA_QROWS
    nq = NA_QROWS * GRID_W
    nkk = NA_KROWS * GRID_W
    nhp = BRANCH_W // LANES
    assert nrb >= 2
    return pl.pallas_call(
        functools.partial(_na_kernel, rows=rows),
        grid=(nhp, B, nrb),
        in_specs=[pl.BlockSpec((None, nq, LANES), lambda h, b, r: (b, r, qcol + h)),
                  pl.BlockSpec((None, T, LANES), lambda h, b, r: (b, 0, kcol + h)),
                  pl.BlockSpec((None, T, LANES), lambda h, b, r: (b, 0, vcol + h)),
                  pl.BlockSpec((None, Lc, LANES), lambda h, b, r: (b, 0, kcol + h)),
                  pl.BlockSpec((None, Lc, LANES), lambda h, b, r: (b, 0, vcol + h)),
                  pl.BlockSpec((2, 2 * NA_WIN_R, GRID_W, LANES), lambda h, b, r: (h, 0, 0, 0))],
        out_specs=pl.BlockSpec((None, nq, LANES), lambda h, b, r: (b, r, h)),
        out_shape=jax.ShapeDtypeStruct((B, T, BRANCH_W), BF16),
        scratch_shapes=[pltpu.VMEM((2, nq, nkk), F32)],
        compiler_params=_params("arbitrary", "arbitrary", "arbitrary"),
        name="na_attn",
    )(z, z, z, zc, zc, tables)


def _pool_kernel(prev_ref, cur_ref, next_ref, w_ref, sc_ref, o_ref, *, T):
    i = pl.program_id(1)
    tt = cur_ref.shape[0]
    E = tt + 2 * POOL_HALO
    ext = jnp.concatenate([prev_ref[...], cur_ref[...], next_ref[...]], axis=0).astype(F32)
    gi = i * tt - POOL_HALO + lax.broadcasted_iota(jnp.int32, (E, 1), 0)
    ext = jnp.where((gi >= 0) & (gi < T), ext, 0.0)
    t = i * tt + lax.broadcasted_iota(jnp.int32, (tt, 1), 0)
    for g, w in enumerate(POOL_WINDOWS):
        cs = slice(g * LANES, (g + 1) * LANES)
        zg = ext[:, cs]
        s = zg + pltpu.roll(zg, 1, 0)
        half = 1
        while 2 * half < w:
            s = pltpu.roll(s, half, 0) + pltpu.roll(s, E - half, 0)
            half *= 2
        cnt = (jnp.minimum(t + w // 2, T) - jnp.maximum(t - w // 2, 0)).astype(F32)
        c0 = slice(POOL_HALO, POOL_HALO + tt)
        d = s[c0] / cnt - zg[c0]
        y = jnp.dot(d.astype(BF16), w_ref[g].astype(BF16), preferred_element_type=F32)
        o_ref[:, cs] = (y * sc_ref[:, cs]).astype(o_ref.dtype)


def _pool(z, w_pool, scale, *, col):
    B, T, _ = z.shape
    tt = _pick(T, (2048, 1024, 512, 256))
    hb = tt // POOL_HALO
    nh = T // POOL_HALO
    return pl.pallas_call(
        functools.partial(_pool_kernel, T=T),
        grid=(B, T // tt),
        in_specs=[pl.BlockSpec((None, POOL_HALO, BRANCH_W), lambda b, i: (b, jnp.maximum(i * hb - 1, 0), col)),
                  pl.BlockSpec((None, tt, BRANCH_W), lambda b, i: (b, i, col)),
                  pl.BlockSpec((None, POOL_HALO, BRANCH_W), lambda b, i: (b, jnp.minimum((i + 1) * hb, nh - 1), col)),
                  pl.BlockSpec(w_pool.shape, lambda b, i: (0, 0, 0)),
                  pl.BlockSpec((1, BRANCH_W), lambda b, i: (0, 0))],
        out_specs=pl.BlockSpec((None, tt, BRANCH_W), lambda b, i: (b, i, 0)),
        out_shape=jax.ShapeDtypeStruct((B, T, BRANCH_W), BF16),
        compiler_params=_params("parallel", "parallel"),
        name="pool_mixer",
    )(z, z, z, w_pool, scale.reshape(1, BRANCH_W))


def _sgu_kernel(u_ref, v_ref, g_ref, w_ref, bt_ref, o_ref):
    ts = u_ref.shape[0]
    u = jax.nn.gelu(u_ref[...].astype(F32))
    v = jax.nn.gelu(v_ref[...].astype(F32))
    vc = v - jnp.mean(v, axis=-1, keepdims=True)
    v = vc * lax.rsqrt(jnp.mean(vc * vc, axis=-1, keepdims=True) + EPS) * g_ref[...]
    vb = v.astype(BF16)
    for g in range(BRANCH_W // LANES):
        cs = slice(g * LANES, (g + 1) * LANES)
        wg = w_ref[g].astype(BF16)
        bg = bt_ref[:, g:g + 1]
        for ck in range(ts // SGU_CHUNK):
            rs = slice(ck * SGU_CHUNK, (ck + 1) * SGU_CHUNK)
            f = jnp.dot(wg, vb[rs, cs], preferred_element_type=F32) + bg
            o_ref[rs, cs] = (u[rs, cs] * f).astype(o_ref.dtype)


def _sgu(z, g, w_s, b, *, ucol, vcol):
    B, T, _ = z.shape
    ts = _pick(T, (2048, 1024, 512, 256, 128))
    return pl.pallas_call(
        _sgu_kernel,
        grid=(B, T // ts),
        in_specs=[pl.BlockSpec((None, ts, BRANCH_W), lambda b_, i: (b_, i, ucol)),
                  pl.BlockSpec((None, ts, BRANCH_W), lambda b_, i: (b_, i, vcol)),
                  pl.BlockSpec((1, BRANCH_W), lambda b_, i: (0, 0)),
                  pl.BlockSpec(w_s.shape, lambda b_, i: (0, 0, 0)),
                  pl.BlockSpec((SGU_CHUNK, BRANCH_W // LANES), lambda b_, i: (0, 0))],
        out_specs=pl.BlockSpec((None, ts, BRANCH_W), lambda b_, i: (b_, i, 0)),
        out_shape=jax.ShapeDtypeStruct((B, T, BRANCH_W), BF16),
        compiler_params=_params("parallel", "parallel"),
        name="sgu_mixer",
    )(z, z, g.reshape(1, BRANCH_W), w_s, b.T)


def _merge_kernel(p_ref, d_ref, s_ref, n_ref, gz_ref, wbr_ref, wout_ref, x_ref, g1_ref,
                  n2_ref, sh_ref, sc_ref, rw_ref, xo_ref, h_ref, *, n_experts):
    D = x_ref.shape[1]
    acc = None
    for i, br in enumerate((p_ref, d_ref, s_ref, n_ref)):
        proj = jnp.dot(br[...], wbr_ref[i], preferred_element_type=F32)
        t = jax.nn.sigmoid(gz_ref[:, i * D:(i + 1) * D].astype(F32)) * proj
        acc = t if acc is None else acc + t
    mix = jnp.dot(acc.astype(BF16), wout_ref[...], preferred_element_type=F32)
    x = x_ref[...] + g1_ref[...] * mix
    xo_ref[...] = x
    y = x * lax.rsqrt(jnp.mean(x * x, axis=-1, keepdims=True) + EPS) * n2_ref[...]
    h = y * (1.0 + sc_ref[...]) + sh_ref[...]
    h_ref[:, 0:D] = h
    rw = rw_ref[...]
    h_hi = h.astype(BF16)
    h_lo = (h - h_hi.astype(F32)).astype(BF16)
    w_hi = rw.astype(BF16)
    w_lo = (rw - w_hi.astype(F32)).astype(BF16)
    lg = (jnp.dot(h_hi, w_hi, preferred_element_type=F32)
          + jnp.dot(h_lo, w_hi, preferred_element_type=F32)
          + jnp.dot(h_hi, w_lo, preferred_element_type=F32))
    lane = lax.broadcasted_iota(jnp.int32, (1, LANES), 1)
    lg = jnp.where(lane < n_experts, lg, -jnp.inf)
    ex = jnp.exp(lg - jnp.max(lg, axis=-1, keepdims=True))
    h_ref[:, D:D + LANES] = ex / jnp.sum(ex, axis=-1, keepdims=True)


def _merge(branches, zr, w_br, w_out, x2, g1, n2g, sh2, sc2, router_w):
    M, D = x2.shape
    G = g1.shape[0]
    rpg = M // G
    tm = _pick(rpg, (256,))
    E = router_w.shape[1]
    router_w = jnp.pad(router_w, ((0, 0), (0, LANES - E)))
    gb = 0
    grp = lambda i: (i * tm // rpg, 0, 0)
    row = lambda i: (i, 0)
    return pl.pallas_call(
        functools.partial(_merge_kernel, n_experts=E),
        grid=(M // tm,),
        in_specs=[pl.BlockSpec((tm, BRANCH_W), row)] * 4 + [
            pl.BlockSpec((tm, N_BRANCH * D), lambda i: (i, gb)),
            pl.BlockSpec(w_br.shape, lambda i: (0, 0, 0)),
            pl.BlockSpec(w_out.shape, lambda i: (0, 0)),
            pl.BlockSpec((tm, D), row),
            pl.BlockSpec((None, 1, D), grp),
            pl.BlockSpec((1, D), lambda i: (0, 0)),
            pl.BlockSpec((None, 1, D), grp),
            pl.BlockSpec((None, 1, D), grp),
            pl.BlockSpec(router_w.shape, lambda i: (0, 0))],
        out_specs=[pl.BlockSpec((tm, D), row), pl.BlockSpec((tm, D + LANES), row)],
        out_shape=[jax.ShapeDtypeStruct((M, D), F32), jax.ShapeDtypeStruct((M, D + LANES), F32)],
        compiler_params=_params("parallel"),
        name="merge_out",
    )(*branches, zr, w_br, w_out, x2, g1, n2g.reshape(1, D), sh2, sc2, router_w)


def _ffn_kernel(rows_ref, h_hbm, w1_ref, w3_ref, w2_ref, y_ref, w1b, w3b, w2b, xbuf, sem, *, B):
    e = pl.program_id(0)
    b = pl.program_id(1)
    C = xbuf.shape[1]
    D = w1_ref.shape[0]
    step = e * B + b
    n_exp = pl.num_programs(0)
    nsteps = n_exp * B
    slot = step % 2

    def start_row(base, slot_, j):
        pltpu.make_async_copy(h_hbm.at[pl.ds(rows_ref[base + j], 1)], xbuf.at[slot_, pl.ds(j, 1)],
                              sem.at[slot_]).start()

    def wait_slot(slot_):
        pltpu.make_async_copy(h_hbm.at[pl.ds(0, C)], xbuf.at[slot_], sem.at[slot_]).wait()

    @pl.when(step == 0)
    def _():
        def issue(j, carry):
            start_row((b * n_exp + e) * C, slot, j)
            return carry

        lax.fori_loop(0, C, issue, 0, unroll=8)

    @pl.when(b == 0)
    def _():
        w1b[...] = w1_ref[...].astype(BF16)
        w3b[...] = w3_ref[...].astype(BF16)
        w2b[...] = w2_ref[...].astype(BF16)

    last = step + 1 >= nsteps
    nxt_b = jnp.where(last, b, (b + 1) % B)
    nxt_e = jnp.where(last, e, e + (b + 1) // B)
    nxt_base = (nxt_b * n_exp + nxt_e) * C
    wait_slot(slot)
    lane = lax.broadcasted_iota(jnp.int32, (1, LANES), 1)
    rc = min(C, FFN_ROWS)
    for r in range(C // rc):
        for j in range(r * rc, (r + 1) * rc):
            start_row(nxt_base, 1 - slot, j)
        rs = pl.ds(r * rc, rc)
        xs = xbuf[slot, rs, 0:D].astype(BF16)
        gate = jnp.sum(jnp.where(lane == e, xbuf[slot, rs, D:D + LANES], 0.0), axis=-1, keepdims=True)
        a = jnp.dot(xs, w1b[...], preferred_element_type=F32)
        g = jnp.dot(xs, w3b[...], preferred_element_type=F32)
        hid = (a * jax.nn.sigmoid(a) * g).astype(BF16)
        y = jnp.dot(hid, w2b[...], preferred_element_type=F32) * gate
        y_ref[rs, :] = y.astype(y_ref.dtype)

    @pl.when(last)
    def _():
        wait_slot(1 - slot)


def _ffn(hext, idx, w1, w3, w2, layer, T):
    B, E, C = idx.shape
    D, Fd = w1.shape[2:]
    grid_spec = pltpu.PrefetchScalarGridSpec(
        num_scalar_prefetch=1,
        grid=(E, B),
        in_specs=[pl.BlockSpec(memory_space=pl.ANY),
                  pl.BlockSpec((None, None, D, Fd), lambda e, b, ix: (layer, e, 0, 0)),
                  pl.BlockSpec((None, None, D, Fd), lambda e, b, ix: (layer, e, 0, 0)),
                  pl.BlockSpec((None, None, Fd, D), lambda e, b, ix: (layer, e, 0, 0))],
        out_specs=pl.BlockSpec((None, None, C, D), lambda e, b, ix: (b, e, 0, 0)),
        scratch_shapes=[pltpu.VMEM((D, Fd), BF16), pltpu.VMEM((D, Fd), BF16), pltpu.VMEM((Fd, D), BF16),
                        pltpu.VMEM((2, C, D + LANES), F32), pltpu.SemaphoreType.DMA((2,))],
    )
    return pl.pallas_call(
        functools.partial(_ffn_kernel, B=B),
        grid_spec=grid_spec,
        out_shape=jax.ShapeDtypeStruct((B, E, C, D), BF16),
        compiler_params=_params("arbitrary", "arbitrary"),
        name="expert_ffn",
    )((idx + jnp.arange(B, dtype=jnp.int32)[:, None, None] * T).reshape(-1), hext, w1, w3, w2)


def _combine_kernel(starts_ref, x_ref, g2_ref, fg_ref, idx_ref, *rest, wb, nwin, final, spt):
    ys = rest[:-2]
    o_ref, acc_ref = rest[-2:]
    b = pl.program_id(0)
    t = pl.program_id(1)
    tt = x_ref.shape[0]
    E = idx_ref.shape[0]
    C = idx_ref.shape[1]
    tok = t * tt + lax.broadcasted_iota(jnp.int32, (tt, 1), 0)

    def window_onehot(e, wk):
        if wb == C:
            ids = idx_ref[e:e + 1, :]
        else:
            ids = idx_ref[e:e + 1, pl.ds(pl.multiple_of(wk * wb, wb), wb)]
        return jnp.where(ids == tok, 1.0, 0.0).astype(BF16)

    def window_dot(e, k, wk):
        return jnp.dot(window_onehot(e, wk), ys[e * nwin + k][...], preferred_element_type=F32)

    onehots = []
    for e in range(E):
        w0 = jnp.minimum(starts_ref[b, t * spt, e] // wb, C // wb - 1)
        onehots.append(window_onehot(e, w0))
    if wb % LANES == 0:
        acc_ref[...] = jnp.dot(jnp.concatenate(onehots, axis=1),
                               jnp.concatenate([ys[e * nwin][...] for e in range(E)], axis=0),
                               preferred_element_type=F32)
    else:
        acc = None
        for e in range(E):
            d = jnp.dot(onehots[e], ys[e * nwin][...], preferred_element_type=F32)
            acc = d if acc is None else acc + d
        acc_ref[...] = acc
    if nwin == 2:
        for e in range(E):
            s0 = starts_ref[b, t * spt, e]
            s1 = starts_ref[b, (t + 1) * spt, e]
            w0 = s0 // wb

            @pl.when((s1 > s0) & ((s1 - 1) // wb > w0))
            def _(e=e, w0=w0):
                acc_ref[...] += window_dot(e, 1, w0 + 1)

    x = x_ref[...] + g2_ref[...] * acc_ref[...]
    if final:
        x = x * lax.rsqrt(jnp.mean(x * x, axis=-1, keepdims=True) + EPS) * fg_ref[...]
    o_ref[...] = x


def _combine(x3, y, idx, starts, g2, final_g, *, final):
    B, T, D = x3.shape
    E, C = idx.shape[1:]
    if C >= COMBINE_TILE and T % COMBINE_TILE == 0:
        wb, nwin, tt = COMBINE_TILE, 2, COMBINE_TILE
    elif C >= LANES:
        wb, nwin, tt = LANES, 2, LANES
    else:
        wb, nwin, tt = C, 1, T
    nt = T // tt
    spt = tt // ROUTE_BLK
    G = g2.shape[0]

    def ymap(e, k):
        return lambda b, t, st: (b, e, jnp.minimum(st[b, t * spt, e] // wb + k, C // wb - 1), 0)

    y_specs = [pl.BlockSpec((None, None, wb, D), ymap(e, k)) for e in range(E) for k in range(nwin)]
    grid_spec = pltpu.PrefetchScalarGridSpec(
        num_scalar_prefetch=1,
        grid=(B, nt),
        in_specs=[pl.BlockSpec((None, tt, D), lambda b, t, st: (b, t, 0)),
                  pl.BlockSpec((None, 1, D), lambda b, t, st: (b * G // B, 0, 0)),
                  pl.BlockSpec((1, D), lambda b, t, st: (0, 0)),
                  pl.BlockSpec((None, E, C), lambda b, t, st: (b, 0, 0))] + y_specs,
        out_specs=pl.BlockSpec((None, tt, D), lambda b, t, st: (b, t, 0)),
        scratch_shapes=[pltpu.VMEM((tt, D), F32)],
    )
    return pl.pallas_call(
        functools.partial(_combine_kernel, wb=wb, nwin=nwin, final=final, spt=spt),
        grid_spec=grid_spec,
        out_shape=jax.ShapeDtypeStruct((B, T, D), F32),
        compiler_params=_params("parallel", "arbitrary"),
        name="moe_combine",
    )(starts, x3, g2, final_g.reshape(1, D), idx, *([y] * (E * nwin)))


def _route_kernel(aff_ref, idx_ref, starts_ref, cst_ref, st_ref, comp_ref, pk_ref, *, C, E):
    T = aff_ref.shape[0]
    nb = T // ROUTE_BLK

    groups = LANES // E
    tg = T // groups
    packed = aff_ref[0:tg, :]
    for g in range(1, groups):
        packed = packed + pltpu.roll(aff_ref[g * tg:(g + 1) * tg, :], g * E, 1)
    pk_ref[...] = packed

    def count(mask):
        ones = jnp.where(mask, 1.0, 0.0)
        ways = 16 if tg % (16 * SUBLANES) == 0 else 1
        part = jnp.sum(ones.reshape(ways, tg // ways, LANES), axis=1)
        per_lane = jnp.sum(part, axis=0, keepdims=True)
        total = per_lane
        for g in range(1, groups):
            total = total + pltpu.roll(per_lane, g * E, 1)
        return total

    def value_of(pattern):
        ex = jnp.right_shift(pattern, F32_MANT_BITS)
        frac = (pattern & ((1 << F32_MANT_BITS) - 1)).astype(F32) * (2.0 ** -F32_MANT_BITS)
        n = F32_EXP_BIAS - ex
        p = jnp.ones(pattern.shape, F32)
        for i in range(7):
            p = jnp.where((jnp.right_shift(n, i) & 1) == 1, p * (2.0 ** -(2 ** i)), p)
        return jnp.where(ex == 0, 0.0, p * (1.0 + frac))

    def bit_body(i, prefix):
        cand = prefix | jnp.left_shift(jnp.int32(1), 29 - i)
        cnt = count(pk_ref[...] >= value_of(cand))
        return jnp.where(cnt >= C, cand, prefix)

    vstar = value_of(lax.fori_loop(0, 30, bit_body, jnp.zeros((1, LANES), jnp.int32)))
    n_gt = count(pk_ref[...] > vstar)
    n_tie = C - n_gt

    r = lax.broadcasted_iota(jnp.int32, (ROUTE_BLK, ROUTE_BLK), 0)
    c = lax.broadcasted_iota(jnp.int32, (ROUTE_BLK, ROUTE_BLK), 1)
    lower_strict = jnp.where(c < r, 1.0, 0.0)
    lower_incl = jnp.where(c <= r, 1.0, 0.0)

    st_ref[...] = jnp.full(st_ref.shape, float(C), F32)
    ties_before = jnp.zeros((1, LANES), F32)
    chosen_before = jnp.zeros((1, LANES), F32)
    for k in range(nb):
        a = aff_ref[k * ROUTE_BLK:(k + 1) * ROUTE_BLK, :]
        gt = a > vstar
        eq = jnp.where(a == vstar, 1.0, 0.0)
        rank = jnp.dot(lower_strict, eq, preferred_element_type=F32) + ties_before
        sel = jnp.where(gt | ((eq > 0.0) & (rank < n_tie)), 1.0, 0.0)
        csum = jnp.dot(lower_incl, sel, preferred_element_type=F32) + chosen_before
        cst_ref[:, k * ROUTE_BLK:(k + 1) * ROUTE_BLK] = csum.T
        st_ref[k:k + 1, :] = chosen_before
        ties_before = ties_before + jnp.sum(eq, axis=0, keepdims=True)
        chosen_before = chosen_before + jnp.sum(sel, axis=0, keepdims=True)
    st_ref[nb:nb + 1, :] = chosen_before
    starts_ref[...] = st_ref[0:nb + 1, :].astype(jnp.int32)

    lane_i = lax.broadcasted_iota(jnp.int32, (1, LANES), 1)
    lane_f = lane_i.astype(F32)
    sub_f = lax.broadcasted_iota(jnp.int32, (LANES, 1), 0).astype(F32)
    blk_f = lax.broadcasted_iota(jnp.int32, (nb, 1), 0).astype(F32)
    comp_ref[...] = jnp.zeros(comp_ref.shape, F32)

    def expert(e, carry):
        mine = lane_i == e
        e8 = pl.multiple_of((e // SUBLANES) * SUBLANES, SUBLANES)
        my_row = lax.broadcasted_iota(jnp.int32, (SUBLANES, 1), 0) == e % SUBLANES
        begins = jnp.sum(jnp.where(mine, st_ref[0:nb, :], 0.0), axis=1, keepdims=True)
        ends = jnp.sum(jnp.where(mine, st_ref[1:nb + 1, :], 0.0), axis=1, keepdims=True)
        for k in range(nb):
            rows8 = cst_ref[pl.ds(e8, SUBLANES), k * ROUTE_BLK:(k + 1) * ROUTE_BLK]
            local = jnp.sum(jnp.where(my_row, rows8, 0.0), axis=0, keepdims=True) - begins[k:k + 1, :]
            comp_ref[:, k:k + 1] = jnp.sum(jnp.where(local <= sub_f, 1.0, 0.0), axis=1, keepdims=True)
        comp = comp_ref[...]
        for jt in range(-(-C // LANES)):
            j = lane_f + jt * LANES
            kb = jnp.sum(jnp.where(ends <= j, 1.0, 0.0), axis=0, keepdims=True)
            start = jnp.sum(jnp.where(blk_f == kb, begins, 0.0), axis=0, keepdims=True)
            onehot = jnp.where(sub_f == kb, 1.0, 0.0)
            per_rho = jnp.dot(comp, onehot, preferred_element_type=F32)
            local = jnp.sum(jnp.where(sub_f == j - start, per_rho, 0.0), axis=0, keepdims=True)
            w = min(LANES, C - jt * LANES)
            idx_ref[e, :, jt * LANES:jt * LANES + w] = jnp.broadcast_to(
                (kb * ROUTE_BLK + local)[:, 0:w].astype(jnp.int32), (SUBLANES, w))
        return carry

    lax.fori_loop(0, E, expert, 0)


def _route(hext3, E, C):
    B, T, W = hext3.shape
    nb = T // ROUTE_BLK
    assert nb <= LANES and LANES % E == 0 and T % (LANES // E * SUBLANES) == 0
    idx8, starts = pl.pallas_call(
        functools.partial(_route_kernel, C=C, E=E),
        grid=(B,),
        in_specs=[pl.BlockSpec((None, T, LANES), lambda b: (b, 0, W // LANES - 1))],
        out_specs=[pl.BlockSpec((None, E, SUBLANES, C), lambda b: (b, 0, 0, 0)),
                   pl.BlockSpec((None, nb + 1, LANES), lambda b: (b, 0, 0))],
        out_shape=[jax.ShapeDtypeStruct((B, E, SUBLANES, C), jnp.int32),
                   jax.ShapeDtypeStruct((B, nb + 1, LANES), jnp.int32)],
        scratch_shapes=[pltpu.VMEM((LANES, T), F32),
                        pltpu.VMEM((-(-(nb + 1) // SUBLANES) * SUBLANES, LANES), F32),
                        pltpu.VMEM((LANES, LANES), F32),
                        pltpu.VMEM((T * E // LANES, LANES), F32)],
        compiler_params=_params("parallel"),
        name="ec_route",
    )(hext3)
    return idx8[:, :, 0, :], starts[:, :, :E]


def _moe(x3, hext, w1, w3, w2, layer, g2, final_g, *, final):
    B, T, D = x3.shape
    E = w1.shape[1]
    C = EC_CAPACITY * T // E
    idx, starts = _route(hext.reshape(B, T, D + LANES), E, C)
    y = _ffn(hext, idx, w1, w3, w2, layer, T)
    return _combine(x3, y, idx, starts, g2, final_g, final=final)


_C_DK, _C_DV, _C_NK, _C_NV, _C_DQ, _C_NQ = 0, 4, 8, 12, 16, 20
ATT_COLS = 6 * BRANCH_W
KV_COLS = 4 * BRANCH_W
GATE_START = 9 * BRANCH_W
_R_POOL, _R_SU, _R_SV = 8, 9, 10


def _mixers(za, zr, zc_kv, T, p, tables, lam_init, *, latent):
    B = za.shape[0]
    sl = lambda a, c: a[:, :, c * LANES:c * LANES + BRANCH_W]
    if latent:
        k_all = jnp.concatenate([sl(zc_kv, _C_DK), sl(za, _C_DK)], axis=1)
        v_all = jnp.concatenate([sl(zc_kv, _C_DV), sl(za, _C_DV)], axis=1)
        diff = _flash(za, _C_DQ // 4, k_all, v_all, p['lam'], p['subln'], mode="diff", lam_init=lam_init)
        na = _na_attn(za, zc_kv, tables, qcol=_C_NQ, kcol=_C_NK, vcol=_C_NV)
    else:
        diff = _flash(za, _C_DQ // 4, sl(za, _C_DK), sl(za, _C_DV), p['lam'], p['subln'],
                      mode="diff", lam_init=lam_init)
        na = _flash(za, _C_NQ // 4, sl(za, _C_NK), sl(za, _C_NV), p['lam'], p['subln'], mode="dense")
    pool = _pool(zr, p['pool_w'], p['pool_scale'], col=_R_POOL)
    sgu = _sgu(zr, p['sgu_g'], p['sgu_w'], p['sgu_b'], ucol=_R_SU, vcol=_R_SV)
    M = B * T
    f2 = lambda a: a.reshape(M, a.shape[-1])
    return [f2(pool), f2(diff), f2(sgu), f2(na)]


def kernel(x, c, ctx, c_ctx, w_mod, b_mod, norm1_g, norm2_g, w_in, pool_w, pool_scale, diff_lambda,
           diff_subln_g, sgu_norm_g, sgu_w, sgu_b, na_rpb, w_br, w_out, router_w, moe_w1, moe_w3,
           moe_w2, final_g):
    B, T, D = x.shape
    Lc = ctx.shape[1]
    depth = w_in.shape[0]
    rows = T // GRID_W
    rope = _rope_tables(T)
    cvec = jnp.concatenate([c, c_ctx[None], jnp.zeros((8 - B - 1, D), F32)], axis=0)
    x_lat = x.reshape(B * T, D)
    x_ctx = ctx.reshape(B * Lc, D)
    for l in range(depth):
        last = l == depth - 1
        lam_init = 0.8 - 0.6 * math.exp(-0.3 * l)
        mod = _mod_vectors(cvec, w_mod, b_mod, l)
        mx = mod[:B].reshape(B, 6, 1, D)
        mc = mod[B:B + 1].reshape(1, 6, 1, D)
        sh1, sc1, g1, sh2, sc2, g2 = [mx[:, i] for i in range(6)]
        csh1, csc1, cg1, csh2, csc2, cg2 = [mc[:, i] for i in range(6)]
        w_in_b = w_in[l].astype(BF16)
        w_att = w_in_b[:, :ATT_COLS]
        w_rest = jnp.concatenate([w_in_b[:, GATE_START:], w_in_b[:, ATT_COLS:GATE_START]], axis=1)
        p = {'lam': diff_lambda[l], 'subln': diff_subln_g[l], 'pool_w': pool_w[l],
             'pool_scale': pool_scale[l], 'sgu_g': sgu_norm_g[l], 'sgu_w': sgu_w[l], 'sgu_b': sgu_b[l]}
        w_br_b = w_br[l].astype(BF16)
        w_out_b = w_out[l].astype(BF16)
        tables = _na_tables(na_rpb[l])

        if last:
            zc_a = _norm_proj(x_ctx, norm1_g[l], csh1, csc1, w_att[:, :KV_COLS], BF16)
        else:
            zc_a = _norm_proj(x_ctx, norm1_g[l], csh1, csc1, w_att, BF16)
        zc_a = zc_a.reshape(B, Lc, -1)

        za = _norm_proj(x_lat, norm1_g[l], sh1, sc1, w_att, BF16, rope=rope,
                        rope_tiles=(_C_DK // 4, _C_DQ // 4)).reshape(B, T, ATT_COLS)
        zr = _norm_proj(x_lat, norm1_g[l], sh1, sc1, w_rest, F32).reshape(B, T, -1)
        br = _mixers(za, zr, zc_a, T, p, tables, lam_init, latent=True)
        x_lat, h2 = _merge(br, zr.reshape(B * T, -1), w_br_b, w_out_b, x_lat, g1,
                                   norm2_g[l], sh2, sc2, router_w[l])
        x_lat = _moe(x_lat.reshape(B, T, D), h2, moe_w1, moe_w3, moe_w2, l, g2,
                     final_g, final=last).reshape(B * T, D)

        if not last:
            zc_r = _norm_proj(x_ctx, norm1_g[l], csh1, csc1, w_rest, F32).reshape(B, Lc, -1)
            brc = _mixers(zc_a, zc_r, None, Lc, p, None, lam_init, latent=False)
            x_ctx, hc2 = _merge(brc, zc_r.reshape(B * Lc, -1), w_br_b, w_out_b, x_ctx, cg1,
                                     norm2_g[l], csh2, csc2, router_w[l])
            x_ctx = _moe(x_ctx.reshape(B, Lc, D), hc2, moe_w1, moe_w3, moe_w2, l, cg2,
                         final_g, final=False).reshape(B * Lc, D)
    return x_lat.reshape(B, T, D)
```
